```python
import math
import jax, jax.numpy as jnp
from jax import lax
import numpy as np

D_MODEL = 1024
BATCH = 8
SEQ = 2048
DEPTH = 1
DEC_BATCH = 128
DEC_SEQ = 4
PAST_LEN = 16384
PAGE_SIZE = 128

A_HEAD_DIM = 64
A_HEADS = D_MODEL // A_HEAD_DIM
A_WIDTH = A_HEADS * A_HEAD_DIM
A_DECAY_LORA = 64
A_ICL_LORA = 64
A_GN_EPS = 64e-5
B_HEADS = 4
B_QK_DIM = D_MODEL // 8
B_V_DIM = 2 * B_QK_DIM
B_QK_WIDTH = B_HEADS * B_QK_DIM
B_V_WIDTH = B_HEADS * B_V_DIM
RET_CHUNK = 128
ROPE_BASE = 10000.0
LN_EPS = 1e-5
SHIFT_WIDTH = 3 * A_WIDTH + A_DECAY_LORA + A_ICL_LORA
IN_SPLITS = (A_WIDTH, A_WIDTH, A_WIDTH, A_DECAY_LORA, A_ICL_LORA,
             A_WIDTH,
             B_QK_WIDTH, B_QK_WIDTH, B_V_WIDTH, B_V_WIDTH,
             D_MODEL, D_MODEL)
IN_WIDTH = sum(IN_SPLITS)
DEEPNORM_ALPHA = (2.0 * DEPTH) ** 0.25
DEEPNORM_BETA = (8.0 * DEPTH) ** -0.25

kernel_name = 'rwkv7_retention_gated_hybrid_step'


def _split(p, sizes):
    out, o = [], 0
    for s in sizes:
        out.append(p[..., o:o + s])
        o += s
    return out


def _norm(x, eps):
    xf = x.astype(jnp.float32)
    m = jnp.mean(xf, axis=-1, keepdims=True)
    v = jnp.mean(jnp.square(xf - m), axis=-1, keepdims=True)
    return ((xf - m) * lax.rsqrt(v + eps)).astype(x.dtype)


def _rotary(x, pos):
    half = x.shape[-1] // 2
    theta = 1.0 / (ROPE_BASE ** jnp.linspace(0.0, 1.0, half, dtype=jnp.float32))
    ang = pos[:, None] * theta[None, :]
    cos = jnp.cos(ang)[None, :, None, :].astype(x.dtype)
    sin = jnp.sin(ang)[None, :, None, :].astype(x.dtype)
    x1, x2 = x[..., :half], x[..., half:]
    return jnp.concatenate([x1 * cos - x2 * sin, x1 * sin + x2 * cos], axis=-1)


def _rwkv7_recurrence(s0, r, w, k, v, kk, a):
    def step(s, inp):
        r_t, w_t, k_t, v_t, kk_t, a_t = inp
        s_kk = jnp.einsum('nhvk,nhk->nhv', s, kk_t)
        s = (s * w_t[:, :, None, :] - s_kk[..., None] * (kk_t * a_t)[:, :, None, :]
             + v_t[..., None] * k_t[:, :, None, :])
        return s, jnp.einsum('nhvk,nhk->nhv', s, r_t)
    xs = tuple(jnp.swapaxes(t, 0, 1) for t in (r, w, k, v, kk, a))
    s, y = lax.scan(step, s0, xs)
    return s, jnp.swapaxes(y, 0, 1)


def _retention_chunk(s, q, k, v, log_gamma):
    L = q.shape[2]
    idx = jnp.arange(L, dtype=jnp.float32)
    diff = idx[:, None] - idx[None, :]
    decay = jnp.exp(jnp.maximum(diff, 0.0)[None] * log_gamma[:, None, None]) * (diff >= 0)[None]
    scores = jnp.einsum('nhid,nhjd->nhij', q, k) * decay[None]
    inner = jnp.einsum('nhij,nhje->nhie', scores, v)
    cross = jnp.einsum('nhid,nhde->nhie', q, s) * jnp.exp((idx[None, :] + 1.0) * log_gamma[:, None])[None, :, :, None]
    k_dec = k * jnp.exp((L - 1.0 - idx)[None, :] * log_gamma[:, None])[None, :, :, None]
    s_new = s * jnp.exp(L * log_gamma)[None, :, None, None] + jnp.einsum('nhjd,nhje->nhde', k_dec, v)
    return s_new, inner + cross


def _retention(s0, q, k, v):
    n, t = q.shape[0], q.shape[1]
    log_gamma = jnp.log1p(-jnp.exp2(-5.0 - jnp.arange(B_HEADS, dtype=jnp.float32)))
    L = RET_CHUNK if t % RET_CHUNK == 0 else t
    nc = t // L

    def blocks(z):
        return z.reshape(n, nc, L, B_HEADS, z.shape[-1]).transpose(1, 0, 3, 2, 4)

    def body(s, qkv):
        return _retention_chunk(s, qkv[0], qkv[1], qkv[2], log_gamma)

    s, o = lax.scan(body, s0, (blocks(q), blocks(k), blocks(v)))
    return s, o.transpose(1, 0, 3, 2, 4).reshape(n, t, B_HEADS, B_V_DIM)


def _layer(x, c, shift_prev, s_wkv, s_ret, pos0,
           w_ada, b_ada, w_in, mu_shift, w0, w_decay_up, a0, w_icl_up, k_k, k_a, r_k,
           gn_a_gain, gn_a_bias, w_out, ln_gain, ln_bias):
    n, t, _ = x.shape
    dt = x.dtype
    ada = jax.nn.silu(c) @ w_ada + b_ada
    shift_c, scale_c, gate_c = _split(ada, (D_MODEL, D_MODEL, D_MODEL))
    u = x * (1.0 + scale_c[:, None, :]) + shift_c[:, None, :]
    proj = u @ w_in
    p_sh = proj[..., :SHIFT_WIDTH]
    prev = jnp.concatenate([shift_prev[:, None, :].astype(dt), p_sh[:, :-1]], axis=1)
    p_mix = p_sh + (prev - p_sh) * mu_shift
    new_shift = p_sh[:, -1]
    r, k, v, wd, ad = _split(p_mix, IN_SPLITS[:5])
    z_a, q_b, k_b, v_b, z_b, g_a, g_b = _split(proj[..., SHIFT_WIDTH:], IN_SPLITS[5:])

    w_log = -jax.nn.softplus(-(w0 + jnp.tanh(wd) @ w_decay_up)) - 0.5
    decay = jnp.exp(-jnp.exp(w_log.astype(jnp.float32)))
    a = jax.nn.sigmoid(a0 + ad @ w_icl_up)
    hs = (n, t, A_HEADS, A_HEAD_DIM)
    kk = (k * k_k).reshape(hs).astype(jnp.float32)
    kk = kk / jnp.maximum(jnp.sqrt(jnp.sum(kk * kk, axis=-1, keepdims=True)), 1e-12)
    k_mod = k * (1.0 + (a - 1.0) * k_a)
    f32 = lambda z: z.reshape(hs).astype(jnp.float32)
    r_h, k_h, v_h, a_h = f32(r), f32(k_mod), f32(v), f32(a)
    s_wkv_new, y_a = _rwkv7_recurrence(s_wkv.astype(jnp.float32), r_h, decay.reshape(hs), k_h, v_h, kk, a_h)
    y_a = _norm(y_a, A_GN_EPS).reshape(n, t, A_WIDTH).astype(dt) * gn_a_gain + gn_a_bias
    bonus = (jnp.sum(r_h * k_h * r_k.astype(jnp.float32), axis=-1, keepdims=True) * v_h).reshape(n, t, A_WIDTH)
    o_a = (y_a + bonus.astype(dt)) * jax.nn.silu(z_a)

    pos = jnp.arange(t, dtype=jnp.float32) + pos0
    q_h = _rotary(q_b.reshape(n, t, B_HEADS, B_QK_DIM), pos).astype(jnp.float32)
    k_hb = _rotary(k_b.reshape(n, t, B_HEADS, B_QK_DIM), pos).astype(jnp.float32) * (B_QK_DIM ** -0.5)
    v_hb = v_b.reshape(n, t, B_HEADS, B_V_DIM).astype(jnp.float32)
    s_ret_new, y_b = _retention(s_ret.astype(jnp.float32), q_h, k_hb, v_hb)
    o_b = _norm(y_b, LN_EPS).reshape(n, t, B_V_WIDTH).astype(dt) * jax.nn.silu(z_b)

    merged = jax.nn.sigmoid(g_a) * o_a + jax.nn.sigmoid(g_b) * o_b
    sub = merged @ w_out
    y = _norm(DEEPNORM_ALPHA * x + gate_c[:, None, :] * sub, LN_EPS) * ln_gain + ln_bias
    return y, new_shift, s_wkv_new, s_ret_new


def setup_inputs(seed: int = 0) -> dict:
    key = jax.random.key(seed)
    ks = jax.random.split(key, 24)

    def nrm(k, shape, s):
        return jax.random.normal(k, shape, jnp.float32) * s

    col_scale = np.ones((IN_WIDTH,), np.float32)
    col_scale[2 * A_WIDTH:3 * A_WIDTH] = DEEPNORM_BETA
    vb0 = SHIFT_WIDTH + A_WIDTH + 2 * B_QK_WIDTH
    col_scale[vb0:vb0 + B_V_WIDTH] = DEEPNORM_BETA
    w0_base = jnp.tile(jnp.linspace(-6.5, -1.5, A_HEAD_DIM, dtype=jnp.float32), A_HEADS)
    return {
        'x_prompt': nrm(ks[0], (BATCH, SEQ, D_MODEL), 1.0),
        'x_sample': nrm(ks[1], (DEC_BATCH, DEC_SEQ, D_MODEL), 1.0),
        'c_prompt': nrm(ks[2], (BATCH, D_MODEL), 1.0),
        'c_sample': nrm(ks[3], (DEC_BATCH, D_MODEL), 1.0),
        'state_shift': nrm(ks[4], (DEPTH, DEC_BATCH, SHIFT_WIDTH), 1.0),
        'state_wkv': nrm(ks[5], (DEPTH, DEC_BATCH, A_HEADS, A_HEAD_DIM, A_HEAD_DIM), 1.0),
        'state_ret': nrm(ks[6], (DEPTH, DEC_BATCH, B_HEADS, B_QK_DIM, B_V_DIM), 0.3),
        'w_ada': nrm(ks[7], (DEPTH, D_MODEL, 3 * D_MODEL), 0.5 * D_MODEL ** -0.5),
        'b_ada': nrm(ks[8], (DEPTH, 3 * D_MODEL), 0.01),
        'w_in': nrm(ks[9], (DEPTH, D_MODEL, IN_WIDTH), D_MODEL ** -0.5) * jnp.asarray(col_scale),
        'mu_shift': jax.random.uniform(ks[10], (DEPTH, SHIFT_WIDTH), jnp.float32),
        'w0': w0_base[None, :] + nrm(ks[11], (DEPTH, A_WIDTH), 0.1),
        'w_decay_up': nrm(ks[12], (DEPTH, A_DECAY_LORA, A_WIDTH), 0.5 * A_DECAY_LORA ** -0.5),
        'a0': nrm(ks[13], (DEPTH, A_WIDTH), 0.1),
        'w_icl_up': nrm(ks[14], (DEPTH, A_ICL_LORA, A_WIDTH), A_ICL_LORA ** -0.5),
        'k_k': 0.85 + nrm(ks[15], (DEPTH, A_WIDTH), 0.02),
        'k_a': 1.0 + nrm(ks[16], (DEPTH, A_WIDTH), 0.02),
        'r_k': nrm(ks[17], (DEPTH, A_HEADS, A_HEAD_DIM), 0.1),
        'gn_a_gain': 1.0 + nrm(ks[18], (DEPTH, A_WIDTH), 0.02),
        'gn_a_bias': nrm(ks[19], (DEPTH, A_WIDTH), 0.01),
        'w_out': nrm(ks[20], (DEPTH, D_MODEL, D_MODEL), DEEPNORM_BETA * D_MODEL ** -0.5),
        'ln_gain': 1.0 + nrm(ks[21], (DEPTH, D_MODEL), 0.02),
        'ln_bias': nrm(ks[22], (DEPTH, D_MODEL), 0.01),
    }


def reference(x_prompt, x_sample, c_prompt, c_sample, state_shift, state_wkv, state_ret,
              w_ada, b_ada, w_in, mu_shift, w0, w_decay_up, a0, w_icl_up, k_k, k_a, r_k,
              gn_a_gain, gn_a_bias, w_out, ln_gain, ln_bias):
    h_p, h_s = x_prompt, x_sample
    sh_p, wkv_p, ret_p, sh_s, wkv_s, ret_s = [], [], [], [], [], []
    dt_p = x_prompt.dtype
    for l in range(DEPTH):
        params = (w_ada[l], b_ada[l], w_in[l], mu_shift[l], w0[l], w_decay_up[l], a0[l], w_icl_up[l],
                  k_k[l], k_a[l], r_k[l], gn_a_gain[l], gn_a_bias[l], w_out[l], ln_gain[l], ln_bias[l])
        zero_shift = jnp.zeros((BATCH, SHIFT_WIDTH), dt_p)
        zero_wkv = jnp.zeros((BATCH, A_HEADS, A_HEAD_DIM, A_HEAD_DIM), jnp.float32)
        zero_ret = jnp.zeros((BATCH, B_HEADS, B_QK_DIM, B_V_DIM), jnp.float32)
        h_p, s1, s2, s3 = _layer(h_p, c_prompt, zero_shift, zero_wkv, zero_ret, 0, *params)
        h_s, t1, t2, t3 = _layer(h_s, c_sample, state_shift[l], state_wkv[l], state_ret[l], PAST_LEN, *params)
        sh_p.append(s1.astype(dt_p))
        wkv_p.append(s2.astype(dt_p))
        ret_p.append(s3.astype(dt_p))
        sh_s.append(t1.astype(state_shift.dtype))
        wkv_s.append(t2.astype(state_wkv.dtype))
        ret_s.append(t3.astype(state_ret.dtype))
    return (h_p, h_s, jnp.stack(sh_p), jnp.stack(wkv_p), jnp.stack(ret_p),
            jnp.stack(sh_s), jnp.stack(wkv_s), jnp.stack(ret_s))
```

```python
import functools
import math

import numpy as np
import jax
import jax.numpy as jnp
from jax import lax
from jax.experimental import pallas as pl
from jax.experimental.pallas import tpu as pltpu

F32 = jnp.float32
BF16 = jnp.bfloat16

D = 1024
HA = 16
KA = 64
LORA = 64
GN_EPS = 64e-5
HB = 4
DK = 128
DV = 256
RET_CHUNK = 128
ROPE_BASE = 10000.0
LN_EPS = 1e-5
SHIFT_W = 3 * D + 2 * LORA
REST_W = 6 * D
LANES = 128
SUBLANES = 8
VMEM_LIMIT = 56 * 1024 * 1024


def _sigmoid(x):
    return 1.0 / (1.0 + jnp.exp(-x))


def _bdot(a, b):
    return jnp.dot(a.astype(BF16), b.astype(BF16), preferred_element_type=F32)


def _const_spec(shape):
    nd = len(shape)
    return pl.BlockSpec(shape, lambda *_: (0,) * nd, pipeline_mode=pl.Buffered(1))


def _ada_kernel(c_ref, w_ref, b_ref, o_ref):
    c = c_ref[...]
    s = c * _sigmoid(c)
    o_ref[...] = jnp.dot(s, w_ref[...], preferred_element_type=F32,
                         precision=lax.Precision.HIGHEST) + b_ref[...]


def _ada(c_all, w_ada, b_ada):
    n = c_all.shape[0]
    return pl.pallas_call(
        _ada_kernel,
        grid=(3,),
        in_specs=[pl.BlockSpec((n, D), lambda j: (0, 0)),
                  pl.BlockSpec((D, D), lambda j: (0, j)),
                  pl.BlockSpec((1, D), lambda j: (0, j))],
        out_specs=pl.BlockSpec((n, D), lambda j: (0, j)),
        out_shape=jax.ShapeDtypeStruct((n, 3 * D), F32),
        compiler_params=pltpu.CompilerParams(dimension_semantics=("arbitrary",)),
        name="ada",
    )(c_all, w_ada, b_ada.reshape(1, 3 * D))


def _modulate(x, scale, shift, gs):
    if gs == 1:
        return x * (1.0 + scale) + shift
    tt = x.shape[0]
    x3 = x.reshape(tt // gs, gs, D)
    return (x3 * (1.0 + scale)[None] + shift[None]).reshape(tt, D)


def _pre_a_kernel(x_ref, scale_ref, shift_ref, prev_ref, wsh_ref, mu_ref, w0_ref, a0_ref,
                  kk_ref, ka_ref, wd_ref, wa_ref,
                  r_out, dec_out, kmod_out, v_out, kkraw_out, a_out, ns_out,
                  p_scr, *, gs, tt, pad):
    j = pl.program_id(1)
    nj = pl.num_programs(1)
    lo = pad - gs

    @pl.when(j == 0)
    def _():
        p_scr[lo:pad, :] = prev_ref[...]

    u = _modulate(x_ref[...], scale_ref[...], shift_ref[...], gs).astype(BF16)

    def mixed(c0, c1):
        p_scr[pad:pad + tt, c0:c1] = jnp.dot(u, wsh_ref[:, c0:c1], preferred_element_type=F32)
        p = p_scr[pad:pad + tt, c0:c1]
        prev = p_scr[lo:lo + tt, c0:c1]
        return p + (prev - p) * mu_ref[:, c0:c1]

    lora_in = mixed(3 * D, SHIFT_W)
    lora_w = _bdot(jnp.tanh(lora_in), wd_ref[...])
    lora_a = _bdot(lora_in, wa_ref[...])
    zw = -(w0_ref[...] + lora_w)
    softplus = jnp.maximum(zw, 0.0) + jnp.log(1.0 + jnp.exp(-jnp.abs(zw)))
    w_log = -softplus - 0.5
    dec_out[...] = jnp.exp(-jnp.exp(w_log))
    a = _sigmoid(a0_ref[...] + lora_a)
    a_out[...] = a
    r_out[...] = mixed(0, D)
    k = mixed(D, 2 * D)
    kkraw_out[...] = k * kk_ref[...]
    kmod_out[...] = k * (1.0 + (a - 1.0) * ka_ref[...])
    v_out[...] = mixed(2 * D, 3 * D)

    @pl.when(j == nj - 1)
    def _():
        ns_out[...] = p_scr[pad + tt - gs:pad + tt, :]

    p_scr[lo:pad, :] = p_scr[pad + tt - gs:pad + tt, :]


def _pre_a(x2, scale, shift, first_prev, wsh, params, *, nseq_grid, tiles, tt, gs):
    rows = x2.shape[0]
    pad = max(SUBLANES, gs)
    mu, w0, a0, k_k, k_a, wd_pad, wa_pad = params
    row_spec = pl.BlockSpec((tt, D), lambda n, j: (n * tiles + j, 0))
    seq_spec = pl.BlockSpec((None, gs, D), lambda n, j: (n, 0, 0))
    out_sds = jax.ShapeDtypeStruct((rows, D), F32)
    kern = functools.partial(_pre_a_kernel, gs=gs, tt=tt, pad=pad)
    return pl.pallas_call(
        kern,
        grid=(nseq_grid, tiles),
        in_specs=[row_spec, seq_spec, seq_spec,
                  pl.BlockSpec((None, gs, SHIFT_W), lambda n, j: (n, 0, 0)),
                  _const_spec((D, SHIFT_W)), _const_spec((1, SHIFT_W)),
                  _const_spec((1, D)), _const_spec((1, D)), _const_spec((1, D)), _const_spec((1, D)),
                  _const_spec((LANES, D)), _const_spec((LANES, D))],
        out_specs=[row_spec] * 6 + [pl.BlockSpec((None, gs, SHIFT_W), lambda n, j: (n, 0, 0))],
        out_shape=[out_sds] * 6 + [jax.ShapeDtypeStruct((nseq_grid, gs, SHIFT_W), F32)],
        scratch_shapes=[pltpu.VMEM((pad + tt, SHIFT_W), F32)],
        compiler_params=pltpu.CompilerParams(dimension_semantics=("arbitrary", "arbitrary"),
                                             vmem_limit_bytes=VMEM_LIMIT),
        name="pre_a",
    )(x2, scale, shift, first_prev, wsh, mu, w0, a0, k_k, k_a, wd_pad, wa_pad)


def _pre_b_kernel(x_ref, scale_ref, shift_ref, w_ref, ga_out, gb_out, qk_out, vb_out, *, gs):
    u = _modulate(x_ref[...], scale_ref[...], shift_ref[...], gs).astype(BF16)

    def proj(c0, c1):
        return jnp.dot(u, w_ref[:, c0:c1], preferred_element_type=F32)

    z_a = proj(0, D)
    g_a = proj(4 * D, 5 * D)
    ga_out[...] = _sigmoid(g_a) * (z_a * _sigmoid(z_a))
    z_b = proj(3 * D, 4 * D)
    g_b = proj(5 * D, 6 * D)
    gb_out[...] = _sigmoid(g_b) * (z_b * _sigmoid(z_b))
    qk_out[...] = proj(D, 2 * D)
    vb_out[...] = proj(2 * D, 3 * D)


def _pre_b(x2, scale, shift, wrest, *, nseq_grid, tiles, tt, gs):
    rows = x2.shape[0]
    row_spec = pl.BlockSpec((tt, D), lambda n, j: (n * tiles + j, 0))
    seq_spec = pl.BlockSpec((None, gs, D), lambda n, j: (n, 0, 0))
    out_sds = jax.ShapeDtypeStruct((rows, D), F32)
    return pl.pallas_call(
        functools.partial(_pre_b_kernel, gs=gs),
        grid=(nseq_grid, tiles),
        in_specs=[row_spec, seq_spec, seq_spec, _const_spec((D, REST_W))],
        out_specs=[row_spec] * 4,
        out_shape=[out_sds] * 4,
        compiler_params=pltpu.CompilerParams(dimension_semantics=("arbitrary", "arbitrary"),
                                             vmem_limit_bytes=VMEM_LIMIT),
        name="pre_b",
    )(x2, scale, shift, wrest)


def _wkv_kernel(r_ref, w_ref, k_ref, v_ref, kk_ref, a_ref, s0_ref, gain_ref, bias_ref, rk_ref,
                o_ref, sfin_ref, s_scr, kkn_scr, b_scr, *, tc):
    j = pl.program_id(1)

    @pl.when(j == 0)
    def _():
        s_scr[...] = s0_ref[...]

    def token(t, carry):
        kkraw = kk_ref[t]
        ss = jnp.sum(kkraw * kkraw, axis=0, keepdims=True)
        kkn = kkraw / jnp.maximum(jnp.sqrt(ss), 1e-12)
        kkn_scr[...] = kkn
        b_scr[...] = kkn * a_ref[t]
        skk = jnp.zeros((KA, LANES), F32)
        for k in range(KA):
            skk = skk + s_scr[k] * kkn_scr[k:k + 1, :]
        vv = v_ref[t]
        y = jnp.zeros((KA, LANES), F32)
        for k in range(KA):
            s_new = (s_scr[k] * w_ref[t, k:k + 1, :] - skk * b_scr[k:k + 1, :]
                     + vv * k_ref[t, k:k + 1, :])
            s_scr[k] = s_new
            y = y + s_new * r_ref[t, k:k + 1, :]
        m = jnp.mean(y, axis=0, keepdims=True)
        yc = y - m
        var = jnp.mean(yc * yc, axis=0, keepdims=True)
        yn = yc * lax.rsqrt(var + GN_EPS) * gain_ref[...] + bias_ref[...]
        bonus = jnp.sum(r_ref[t] * k_ref[t] * rk_ref[...], axis=0, keepdims=True) * vv
        o_ref[t] = yn + bonus
        return carry

    lax.fori_loop(0, tc, token, 0)

    @pl.when(j == pl.num_programs(1) - 1)
    def _():
        sfin_ref[...] = s_scr[...]


def _wkv(vecs, s0, gain_l, bias_l, rk_l, *, tc):
    t, _, lanes = vecs[0].shape
    groups = lanes // LANES
    vec_spec = pl.BlockSpec((tc, KA, LANES), lambda g, j: (j, 0, g))
    st_spec = pl.BlockSpec((KA, KA, LANES), lambda g, j: (0, 0, g))
    par_spec = pl.BlockSpec((KA, LANES), lambda g, j: (0, g))
    return pl.pallas_call(
        functools.partial(_wkv_kernel, tc=tc),
        grid=(groups, t // tc),
        in_specs=[vec_spec] * 6 + [st_spec, par_spec, par_spec, par_spec],
        out_specs=[vec_spec, st_spec],
        out_shape=[jax.ShapeDtypeStruct((t, KA, lanes), F32),
                   jax.ShapeDtypeStruct((KA, KA, lanes), F32)],
        scratch_shapes=[pltpu.VMEM((KA, KA, LANES), F32),
                        pltpu.VMEM((KA, LANES), F32), pltpu.VMEM((KA, LANES), F32)],
        compiler_params=pltpu.CompilerParams(dimension_semantics=("arbitrary", "arbitrary"),
                                             vmem_limit_bytes=VMEM_LIMIT),
        name="wkv",
    )(*vecs, s0, gain_l, bias_l, rk_l)


def _log_gamma():
    return [math.log1p(-2.0 ** (-5.0 - h)) for h in range(HB)]


def _rope_tables(pos):
    half = DK // 2
    theta = 1.0 / (ROPE_BASE ** jnp.linspace(0.0, 1.0, half, dtype=F32))
    ang = pos[:, None] * theta[None, :]
    cos, sin = jnp.cos(ang), jnp.sin(ang)
    return jnp.concatenate([cos, cos], axis=1), jnp.concatenate([-sin, sin], axis=1)


def _decay_tables(chunk):
    lg = jnp.asarray(_log_gamma(), F32)
    idx = jnp.arange(chunk, dtype=F32)
    diff = idx[:, None] - idx[None, :]
    dmat = jnp.exp(jnp.maximum(diff, 0.0)[None] * lg[:, None, None]) * (diff >= 0)[None]
    cross = jnp.exp((idx[None, :] + 1.0) * lg[:, None])[:, :, None]
    kdec = jnp.exp((chunk - 1.0 - idx)[None, :] * lg[:, None])[:, :, None]
    sdec = jnp.exp(chunk * lg)
    return dmat.astype(F32), cross, kdec, sdec


def _rotary(x, cos2, sin2):
    return x * cos2 + pltpu.roll(x, DK // 2, axis=1) * sin2


def _ln_rows(y, eps):
    m = jnp.mean(y, axis=-1, keepdims=True)
    yc = y - m
    var = jnp.mean(yc * yc, axis=-1, keepdims=True)
    return yc * lax.rsqrt(var + eps)


def _ret_prompt_kernel(qk_ref, vb_ref, cos_ref, sin_ref, dmat_ref, cross_ref, kdec_ref, sdec_ref,
                       o_ref, sfin_ref, s_scr):
    j = pl.program_id(1)

    @pl.when(j == 0)
    def _():
        s_scr[...] = jnp.zeros_like(s_scr)

    cos2, sin2 = cos_ref[...], sin_ref[...]
    for h in range(HB):
        q = _rotary(qk_ref[:, h * DK:(h + 1) * DK], cos2, sin2)
        k = _rotary(qk_ref[:, HB * DK + h * DK:HB * DK + (h + 1) * DK], cos2, sin2) * (DK ** -0.5)
        v = vb_ref[:, h * DV:(h + 1) * DV].astype(BF16)
        qb = q.astype(BF16)
        scores = lax.dot_general(qb, k.astype(BF16), (((1,), (1,)), ((), ())),
                                 preferred_element_type=F32) * dmat_ref[h]
        s = s_scr[h]
        y = (jnp.dot(scores.astype(BF16), v, preferred_element_type=F32)
             + jnp.dot(qb, s.astype(BF16), preferred_element_type=F32) * cross_ref[h])
        kd = (k * kdec_ref[h]).astype(BF16)
        s_scr[h] = s * sdec_ref[h] + lax.dot_general(kd, v, (((0,), (0,)), ((), ())),
                                                     preferred_element_type=F32)
        o_ref[:, h * DV:(h + 1) * DV] = _ln_rows(y, LN_EPS)

    @pl.when(j == pl.num_programs(1) - 1)
    def _():
        sfin_ref[...] = s_scr[...]


def _ret_prompt(qk, vb, nseq, t):
    chunks = t // RET_CHUNK
    cos2, sin2 = _rope_tables(jnp.arange(t, dtype=F32) + 0)
    dmat, cross, kdec, sdec = _decay_tables(RET_CHUNK)
    row_spec = pl.BlockSpec((RET_CHUNK, D), lambda n, j: (n * chunks + j, 0))
    tab_spec = pl.BlockSpec((RET_CHUNK, DK), lambda n, j: (j, 0))
    return pl.pallas_call(
        _ret_prompt_kernel,
        grid=(nseq, chunks),
        in_specs=[row_spec, row_spec, tab_spec, tab_spec,
                  _const_spec((HB, RET_CHUNK, RET_CHUNK)), _const_spec((HB, RET_CHUNK, 1)),
                  _const_spec((HB, RET_CHUNK, 1)),
                  pl.BlockSpec(memory_space=pltpu.SMEM)],
        out_specs=[row_spec, pl.BlockSpec((None, HB, DK, DV), lambda n, j: (n, 0, 0, 0))],
        out_shape=[jax.ShapeDtypeStruct(qk.shape, F32),
                   jax.ShapeDtypeStruct((nseq, HB, DK, DV), F32)],
        scratch_shapes=[pltpu.VMEM((HB, DK, DV), F32)],
        compiler_params=pltpu.CompilerParams(dimension_semantics=("arbitrary", "arbitrary"),
                                             vmem_limit_bytes=VMEM_LIMIT),
        name="ret_prompt",
    )(qk, vb, cos2, sin2, dmat, cross, kdec, sdec)


def _ret_sample_kernel(qk_ref, vb_ref, cos_ref, sin_ref, s_ref, dmat_ref, cross_ref, kdec_ref, sdec_ref,
                       o_ref, snew_ref, *, nb):
    cos2, sin2 = cos_ref[...], sin_ref[...]
    for n in range(nb):
        for h in range(HB):
            q = _rotary(qk_ref[:, n, h * DK:(h + 1) * DK], cos2, sin2)
            k = _rotary(qk_ref[:, n, HB * DK + h * DK:HB * DK + (h + 1) * DK], cos2, sin2) * (DK ** -0.5)
            v = vb_ref[:, n, h * DV:(h + 1) * DV].astype(BF16)
            qb = q.astype(BF16)
            scores = lax.dot_general(qb, k.astype(BF16), (((1,), (1,)), ((), ())),
                                     preferred_element_type=F32) * dmat_ref[h]
            s = s_ref[n, h]
            y = (jnp.dot(scores.astype(BF16), v, preferred_element_type=F32)
                 + jnp.dot(qb, s.astype(BF16), preferred_element_type=F32) * cross_ref[h])
            kd = (k * kdec_ref[h]).astype(BF16)
            snew_ref[n, h] = s * sdec_ref[h] + lax.dot_general(kd, v, (((0,), (0,)), ((), ())),
                                                               preferred_element_type=F32)
            o_ref[:, n, h * DV:(h + 1) * DV] = _ln_rows(y, LN_EPS)


def _ret_sample(qk3, vb3, s0, pos0, *, nb):
    t, nseq, _ = qk3.shape
    cos2, sin2 = _rope_tables(jnp.arange(t, dtype=F32) + pos0)
    dmat, cross, kdec, sdec = _decay_tables(t)
    row_spec = pl.BlockSpec((t, nb, D), lambda i: (0, i, 0))
    st_spec = pl.BlockSpec((nb, HB, DK, DV), lambda i: (i, 0, 0, 0))
    return pl.pallas_call(
        functools.partial(_ret_sample_kernel, nb=nb),
        grid=(nseq // nb,),
        in_specs=[row_spec, row_spec, _const_spec((t, DK)), _const_spec((t, DK)), st_spec,
                  _const_spec((HB, t, t)), _const_spec((HB, t, 1)), _const_spec((HB, t, 1)),
                  pl.BlockSpec(memory_space=pltpu.SMEM)],
        out_specs=[row_spec, st_spec],
        out_shape=[jax.ShapeDtypeStruct(qk3.shape, F32), jax.ShapeDtypeStruct(s0.shape, F32)],
        compiler_params=pltpu.CompilerParams(dimension_semantics=("arbitrary",),
                                             vmem_limit_bytes=VMEM_LIMIT),
        name="ret_sample",
    )(qk3, vb3, cos2, sin2, s0, dmat, cross, kdec, sdec)


def _post_kernel(ga_ref, ya_ref, gb_ref, ob_ref, x_ref, gate_ref, wout_ref, lng_ref, lnb_ref, y_ref,
                 *, gs, alpha):
    merged = ga_ref[...] * ya_ref[...] + gb_ref[...] * ob_ref[...]
    sub = jnp.dot(merged.astype(BF16), wout_ref[...], preferred_element_type=F32)
    x = x_ref[...]
    gate = gate_ref[...]
    if gs == 1:
        z = alpha * x + gate * sub
    else:
        tt = x.shape[0]
        z = (alpha * x.reshape(tt // gs, gs, D) + gate[None] * sub.reshape(tt // gs, gs, D)).reshape(tt, D)
    y_ref[...] = _ln_rows(z, LN_EPS) * lng_ref[...] + lnb_ref[...]


def _post(ga, ya, gb, ob, x2, gate, wout, ln_gain, ln_bias, *, nseq_grid, tiles, tt, gs, alpha):
    row_spec = pl.BlockSpec((tt, D), lambda n, j: (n * tiles + j, 0))
    return pl.pallas_call(
        functools.partial(_post_kernel, gs=gs, alpha=alpha),
        grid=(nseq_grid, tiles),
        in_specs=[row_spec] * 5 + [pl.BlockSpec((None, gs, D), lambda n, j: (n, 0, 0)),
                                   _const_spec((D, D)), _const_spec((1, D)), _const_spec((1, D))],
        out_specs=row_spec,
        out_shape=jax.ShapeDtypeStruct(x2.shape, F32),
        compiler_params=pltpu.CompilerParams(dimension_semantics=("arbitrary", "arbitrary"),
                                             vmem_limit_bytes=VMEM_LIMIT),
        name="post",
    )(ga, ya, gb, ob, x2, gate, wout, ln_gain, ln_bias)


def _to_lanes(a2, nseq, t, seq_major):
    if seq_major:
        return a2.reshape(nseq, t, HA, KA).transpose(1, 3, 0, 2).reshape(t, KA, nseq * HA)
    return a2.reshape(t, nseq, HA, KA).transpose(0, 3, 1, 2).reshape(t, KA, nseq * HA)


def _from_lanes(o, nseq, t, seq_major):
    o4 = o.reshape(t, KA, nseq, HA)
    if seq_major:
        return o4.transpose(2, 0, 3, 1).reshape(nseq * t, D)
    return o4.transpose(0, 2, 3, 1).reshape(t * nseq, D)


def _head_param_lanes(p, nseq):
    return jnp.tile(p.reshape(HA, KA).T, (1, nseq))


def _layer(x2, ada, first_prev, s_wkv, s_ret, pos0, weights, *, nseq, t, seq_major, tt, tc, depth):
    (wsh, wrest, pre_params, gn_gain, gn_bias, r_k, wout, ln_gain, ln_bias) = weights
    if seq_major:
        nseq_grid, gs, tiles = nseq, 1, t // tt
    else:
        nseq_grid, gs, tiles = 1, nseq, (t * nseq) // tt
    shift_c = ada[:, :D].reshape(nseq_grid, gs, D)
    scale_c = ada[:, D:2 * D].reshape(nseq_grid, gs, D)
    gate_c = ada[:, 2 * D:].reshape(nseq_grid, gs, D)
    cfg = dict(nseq_grid=nseq_grid, tiles=tiles, tt=tt, gs=gs)

    r, dec, kmod, v, kkraw, a, new_shift = _pre_a(
        x2, scale_c, shift_c, first_prev.reshape(nseq_grid, gs, SHIFT_W), wsh, pre_params, **cfg)
    ga, gb, qk, vb = _pre_b(x2, scale_c, shift_c, wrest, **cfg)

    vecs = [_to_lanes(z, nseq, t, seq_major) for z in (r, dec, kmod, v, kkraw, a)]
    s0 = s_wkv.transpose(3, 2, 0, 1).reshape(KA, KA, nseq * HA)
    ya_l, s_fin = _wkv(vecs, s0, _head_param_lanes(gn_gain, nseq), _head_param_lanes(gn_bias, nseq),
                       _head_param_lanes(r_k.reshape(-1), nseq), tc=tc)
    ya = _from_lanes(ya_l, nseq, t, seq_major)
    new_wkv = s_fin.reshape(KA, KA, nseq, HA).transpose(2, 3, 1, 0)

    if seq_major:
        ob, new_ret = _ret_prompt(qk, vb, nseq, t)
    else:
        ob3, new_ret = _ret_sample(qk.reshape(t, nseq, D), vb.reshape(t, nseq, D), s_ret, pos0, nb=8)
        ob = ob3.reshape(t * nseq, D)

    alpha = (2.0 * depth) ** 0.25
    y = _post(ga, ya, gb, ob, x2, gate_c, wout, ln_gain, ln_bias, alpha=alpha, **cfg)
    return y, new_shift.reshape(nseq, SHIFT_W), new_wkv, new_ret


def kernel(x_prompt, x_sample, c_prompt, c_sample, state_shift, state_wkv, state_ret, w_ada, b_ada, w_in,
           mu_shift, w0, w_decay_up, a0, w_icl_up, k_k, k_a, r_k, gn_a_gain, gn_a_bias, w_out, ln_gain, ln_bias):
    depth = w_in.shape[0]
    nb, t_p, _ = x_prompt.shape
    ns, t_s, _ = x_sample.shape
    past_len = 16384
    h_p = x_prompt.reshape(nb * t_p, D)
    h_s = x_sample.transpose(1, 0, 2).reshape(t_s * ns, D)
    outs = [[] for _ in range(6)]
    zero_lora = jnp.zeros((LORA, D), F32)
    for l in range(depth):
        row = lambda p: p[l].reshape(1, -1)
        wsh = w_in[l][:, :SHIFT_W].astype(BF16)
        wrest = w_in[l][:, SHIFT_W:].astype(BF16)
        wd_pad = jnp.concatenate([w_decay_up[l], zero_lora], axis=0).astype(BF16)
        wa_pad = jnp.concatenate([zero_lora, w_icl_up[l]], axis=0).astype(BF16)
        pre_params = (row(mu_shift), row(w0), row(a0), row(k_k), row(k_a), wd_pad, wa_pad)
        weights = (wsh, wrest, pre_params, gn_a_gain[l], gn_a_bias[l], r_k[l], w_out[l].astype(BF16),
                   row(ln_gain), row(ln_bias))
        ada = _ada(jnp.concatenate([c_prompt, c_sample], axis=0), w_ada[l], b_ada[l])
        h_p, s1, s2, s3 = _layer(
            h_p, ada[:nb], jnp.zeros((nb, SHIFT_W), F32), jnp.zeros((nb, HA, KA, KA), F32), None, 0,
            weights, nseq=nb, t=t_p, seq_major=True, tt=256, tc=16, depth=depth)
        h_s, t1, t2, t3 = _layer(
            h_s, ada[nb:], state_shift[l], state_wkv[l], state_ret[l], past_len,
            weights, nseq=ns, t=t_s, seq_major=False, tt=t_s * ns, tc=t_s, depth=depth)
        for lst, val in zip(outs, (s1, s2, s3, t1, t2, t3)):
            lst.append(val)
    y_p = h_p.reshape(nb, t_p, D)
    y_s = h_s.reshape(t_s, ns, D).transpose(1, 0, 2)
    return (y_p, y_s) + tuple(jnp.stack(o) for o in outs)
```

```python
import functools
import math

import numpy as np
import jax
import jax.numpy as jnp
from jax import lax
from jax.experimental import pallas as pl
from jax.experimental.pallas import tpu as pltpu

F32 = jnp.float32
BF16 = jnp.bfloat16

D = 1024
HA = 16
KA = 64
LORA = 64
GN_EPS = 64e-5
HB = 4
DK = 128
DV = 256
RET_CHUNK = 128
ROPE_BASE = 10000.0
LN_EPS = 1e-5
SHIFT_W = 3 * D + 2 * LORA
REST_W = 6 * D
LANES = 128
SUBLANES = 8
VMEM_LIMIT = 56 * 1024 * 1024


def _sigmoid(x):
    return 1.0 / (1.0 + jnp.exp(-x))


def _bdot(a, b):
    return jnp.dot(a.astype(BF16), b.astype(BF16), preferred_element_type=F32)


def _const_spec(shape):
    nd = len(shape)
    return pl.BlockSpec(shape, lambda *_: (0,) * nd, pipeline_mode=pl.Buffered(1))


def _ada_kernel(c_ref, w_ref, b_ref, o_ref):
    c = c_ref[...]
    s = c * _sigmoid(c)
    o_ref[...] = jnp.dot(s, w_ref[...], preferred_element_type=F32,
                         precision=lax.Precision.HIGHEST) + b_ref[...]


def _ada(c_all, w_ada, b_ada):
    n = c_all.shape[0]
    return pl.pallas_call(
        _ada_kernel,
        grid=(3,),
        in_specs=[pl.BlockSpec((n, D), lambda j: (0, 0)),
                  pl.BlockSpec((D, D), lambda j: (0, j)),
                  pl.BlockSpec((1, D), lambda j: (0, j))],
        out_specs=pl.BlockSpec((n, D), lambda j: (0, j)),
        out_shape=jax.ShapeDtypeStruct((n, 3 * D), F32),
        compiler_params=pltpu.CompilerParams(dimension_semantics=("arbitrary",)),
        name="ada",
    )(c_all, w_ada, b_ada.reshape(1, 3 * D))


def _modulate(x, scale, shift, gs):
    if gs == 1:
        return x * (1.0 + scale) + shift
    tt = x.shape[0]
    x3 = x.reshape(tt // gs, gs, D)
    return (x3 * (1.0 + scale)[None] + shift[None]).reshape(tt, D)


def _pre_a_kernel(x_ref, scale_ref, shift_ref, prev_ref, wsh_ref, mu_ref, w0_ref, a0_ref,
                  kk_ref, ka_ref, wd_ref, wa_ref,
                  r_out, dec_out, kmod_out, v_out, kkraw_out, a_out, ns_out,
                  p_scr, *, gs, tt, pad):
    j = pl.program_id(1)
    nj = pl.num_programs(1)
    lo = pad - gs

    @pl.when(j == 0)
    def _():
        p_scr[lo:pad, :] = prev_ref[...]

    u = _modulate(x_ref[...], scale_ref[...], shift_ref[...], gs).astype(BF16)

    def mixed(c0, c1):
        p_scr[pad:pad + tt, c0:c1] = jnp.dot(u, wsh_ref[:, c0:c1], preferred_element_type=F32)
        p = p_scr[pad:pad + tt, c0:c1]
        prev = p_scr[lo:lo + tt, c0:c1]
        return p + (prev - p) * mu_ref[:, c0:c1]

    lora_in = mixed(3 * D, SHIFT_W)
    lora_w = _bdot(jnp.tanh(lora_in), wd_ref[...])
    lora_a = _bdot(lora_in, wa_ref[...])
    zw = -(w0_ref[...] + lora_w)
    softplus = jnp.maximum(zw, 0.0) + jnp.log(1.0 + jnp.exp(-jnp.abs(zw)))
    w_log = -softplus - 0.5
    dec_out[...] = jnp.exp(-jnp.exp(w_log))
    a = _sigmoid(a0_ref[...] + lora_a)
    a_out[...] = a
    r_out[...] = mixed(0, D)
    k = mixed(D, 2 * D)
    kkraw_out[...] = k * kk_ref[...]
    kmod_out[...] = k * (1.0 + (a - 1.0) * ka_ref[...])
    v_out[...] = mixed(2 * D, 3 * D)

    @pl.when(j == nj - 1)
    def _():
        ns_out[...] = p_scr[pad + tt - gs:pad + tt, :]

    p_scr[lo:pad, :] = p_scr[pad + tt - gs:pad + tt, :]


def _pre_a(x2, scale, shift, first_prev, wsh, params, *, nseq_grid, tiles, tt, gs):
    rows = x2.shape[0]
    pad = max(SUBLANES, gs)
    mu, w0, a0, k_k, k_a, wd_pad, wa_pad = params
    row_spec = pl.BlockSpec((tt, D), lambda n, j: (n * tiles + j, 0))
    seq_spec = pl.BlockSpec((None, gs, D), lambda n, j: (n, 0, 0))
    out_sds = jax.ShapeDtypeStruct((rows, D), F32)
    kern = functools.partial(_pre_a_kernel, gs=gs, tt=tt, pad=pad)
    return pl.pallas_call(
        kern,
        grid=(nseq_grid, tiles),
        in_specs=[row_spec, seq_spec, seq_spec,
                  pl.BlockSpec((None, gs, SHIFT_W), lambda n, j: (n, 0, 0)),
                  _const_spec((D, SHIFT_W)), _const_spec((1, SHIFT_W)),
                  _const_spec((1, D)), _const_spec((1, D)), _const_spec((1, D)), _const_spec((1, D)),
                  _const_spec((LANES, D)), _const_spec((LANES, D))],
        out_specs=[row_spec] * 6 + [pl.BlockSpec((None, gs, SHIFT_W), lambda n, j: (n, 0, 0))],
        out_shape=[out_sds] * 6 + [jax.ShapeDtypeStruct((nseq_grid, gs, SHIFT_W), F32)],
        scratch_shapes=[pltpu.VMEM((pad + tt, SHIFT_W), F32)],
        compiler_params=pltpu.CompilerParams(dimension_semantics=("arbitrary", "arbitrary"),
                                             vmem_limit_bytes=VMEM_LIMIT),
        name="pre_a",
    )(x2, scale, shift, first_prev, wsh, mu, w0, a0, k_k, k_a, wd_pad, wa_pad)


def _pre_b_kernel(x_ref, scale_ref, shift_ref, w_ref, ga_out, gb_out, qk_out, vb_out, *, gs):
    u = _modulate(x_ref[...], scale_ref[...], shift_ref[...], gs).astype(BF16)

    def proj(c0, c1):
        return jnp.dot(u, w_ref[:, c0:c1], preferred_element_type=F32)

    z_a = proj(0, D)
    g_a = proj(4 * D, 5 * D)
    ga_out[...] = _sigmoid(g_a) * (z_a * _sigmoid(z_a))
    z_b = proj(3 * D, 4 * D)
    g_b = proj(5 * D, 6 * D)
    gb_out[...] = _sigmoid(g_b) * (z_b * _sigmoid(z_b))
    qk_out[...] = proj(D, 2 * D)
    vb_out[...] = proj(2 * D, 3 * D)


def _pre_b(x2, scale, shift, wrest, *, nseq_grid, tiles, tt, gs):
    rows = x2.shape[0]
    row_spec = pl.BlockSpec((tt, D), lambda n, j: (n * tiles + j, 0))
    seq_spec = pl.BlockSpec((None, gs, D), lambda n, j: (n, 0, 0))
    out_sds = jax.ShapeDtypeStruct((rows, D), F32)
    return pl.pallas_call(
        functools.partial(_pre_b_kernel, gs=gs),
        grid=(nseq_grid, tiles),
        in_specs=[row_spec, seq_spec, seq_spec, _const_spec((D, REST_W))],
        out_specs=[row_spec] * 4,
        out_shape=[out_sds] * 4,
        compiler_params=pltpu.CompilerParams(dimension_semantics=("arbitrary", "arbitrary"),
                                             vmem_limit_bytes=VMEM_LIMIT),
        name="pre_b",
    )(x2, scale, shift, wrest)


V_KKN, V_B, V_W, V_K, V_R, V_V = range(6)


def _wkv_kernel(r_ref, w_ref, k_ref, v_ref, kk_ref, a_ref, s0_ref, gain_ref, bias_ref, rk_ref,
                o_ref, sfin_ref, s_scr, vec_scr, row_scr, y_scr, cur_scr, *, tc, seq_major):
    j = pl.program_id(1)

    @pl.when(j == 0)
    def _():
        s_scr[...] = s0_ref[...]

    def load(ref, t):
        x = ref[:, t] if seq_major else ref[t]
        return x.reshape(LANES, KA).T

    def prep(t, carry):
        kkraw = load(kk_ref, t)
        ss = jnp.sum(kkraw * kkraw, axis=0, keepdims=True)
        kkn = kkraw * (1.0 / jnp.maximum(jnp.sqrt(ss), 1e-12))
        vec_scr[t, V_KKN] = kkn
        vec_scr[t, V_B] = kkn * load(a_ref, t)
        vec_scr[t, V_W] = load(w_ref, t)
        kmod = load(k_ref, t)
        vec_scr[t, V_K] = kmod
        r = load(r_ref, t)
        vec_scr[t, V_R] = r
        vec_scr[t, V_V] = load(v_ref, t)
        rk = jnp.sum(r * kmod * rk_ref[...], axis=0, keepdims=True)
        row_scr[t] = jnp.broadcast_to(rk, (SUBLANES, LANES))
        return carry

    def state_pass(t, carry):
        cur_scr[...] = vec_scr[t]
        skk = jnp.zeros((KA, LANES), F32)
        for k in range(KA):
            skk = skk + s_scr[k] * cur_scr[V_KKN, k:k + 1, :]
        vv = cur_scr[V_V]
        y = jnp.zeros((KA, LANES), F32)
        for k in range(KA):
            s_new = (s_scr[k] * cur_scr[V_W, k:k + 1, :] - skk * cur_scr[V_B, k:k + 1, :]
                     + vv * cur_scr[V_K, k:k + 1, :])
            s_scr[k] = s_new
            y = y + s_new * cur_scr[V_R, k:k + 1, :]
        y_scr[t] = y
        return carry

    def finish(t, carry):
        y = y_scr[t]
        m = jnp.mean(y, axis=0, keepdims=True)
        yc = y - m
        var = jnp.mean(yc * yc, axis=0, keepdims=True)
        yn = yc * lax.rsqrt(var + GN_EPS) * gain_ref[...] + bias_ref[...]
        o = (yn + row_scr[t, 0:1, :] * vec_scr[t, V_V]).T.reshape(LANES // HA, HA, KA)
        if seq_major:
            o_ref[:, t] = o
        else:
            o_ref[t] = o
        return carry

    lax.fori_loop(0, tc, prep, 0, unroll=2)
    lax.fori_loop(0, tc, state_pass, 0)
    lax.fori_loop(0, tc, finish, 0, unroll=2)

    @pl.when(j == pl.num_programs(1) - 1)
    def _():
        sfin_ref[...] = s_scr[...]


def _wkv(vecs, s0, gain_l, bias_l, rk_l, *, nseq, t, tc, seq_major):
    nsl = LANES // HA
    groups = nseq // nsl
    if seq_major:
        shape4 = (nseq, t, HA, KA)
        vec_spec = pl.BlockSpec((nsl, tc, HA, KA), lambda g, j: (g, j, 0, 0))
    else:
        shape4 = (t, nseq, HA, KA)
        vec_spec = pl.BlockSpec((tc, nsl, HA, KA), lambda g, j: (j, g, 0, 0))
    st_spec = pl.BlockSpec((KA, KA, LANES), lambda g, j: (0, 0, g))
    par_spec = pl.BlockSpec((KA, LANES), lambda g, j: (0, g))
    o4, s_fin = pl.pallas_call(
        functools.partial(_wkv_kernel, tc=tc, seq_major=seq_major),
        grid=(groups, t // tc),
        in_specs=[vec_spec] * 6 + [st_spec, par_spec, par_spec, par_spec],
        out_specs=[vec_spec, st_spec],
        out_shape=[jax.ShapeDtypeStruct(shape4, F32),
                   jax.ShapeDtypeStruct((KA, KA, nseq * HA), F32)],
        scratch_shapes=[pltpu.VMEM((KA, KA, LANES), F32), pltpu.VMEM((tc, 6, KA, LANES), F32),
                        pltpu.VMEM((tc, SUBLANES, LANES), F32), pltpu.VMEM((tc, KA, LANES), F32),
                        pltpu.VMEM((6, KA, LANES), F32)],
        compiler_params=pltpu.CompilerParams(dimension_semantics=("arbitrary", "arbitrary"),
                                             vmem_limit_bytes=VMEM_LIMIT),
        name="wkv",
    )(*[z.reshape(shape4) for z in vecs], s0, gain_l, bias_l, rk_l)
    return o4.reshape(nseq * t, D), s_fin


def _log_gamma():
    return [math.log1p(-2.0 ** (-5.0 - h)) for h in range(HB)]


def _rope_tables(pos):
    half = DK // 2
    theta = 1.0 / (ROPE_BASE ** jnp.linspace(0.0, 1.0, half, dtype=F32))
    ang = pos[:, None] * theta[None, :]
    cos, sin = jnp.cos(ang), jnp.sin(ang)
    return jnp.concatenate([cos, cos], axis=1), jnp.concatenate([-sin, sin], axis=1)


def _decay_tables(chunk):
    lg = jnp.asarray(_log_gamma(), F32)
    idx = jnp.arange(chunk, dtype=F32)
    diff = idx[:, None] - idx[None, :]
    dmat = jnp.exp(jnp.maximum(diff, 0.0)[None] * lg[:, None, None]) * (diff >= 0)[None]
    cross = jnp.exp((idx[None, :] + 1.0) * lg[:, None])[:, :, None]
    kdec = jnp.exp((chunk - 1.0 - idx)[None, :] * lg[:, None])[:, :, None]
    sdec = jnp.exp(chunk * lg)
    return dmat.astype(F32), cross, kdec, sdec


def _rotary(x, cos2, sin2):
    return x * cos2 + pltpu.roll(x, DK // 2, axis=1) * sin2


def _ln_rows(y, eps):
    m = jnp.mean(y, axis=-1, keepdims=True)
    yc = y - m
    var = jnp.mean(yc * yc, axis=-1, keepdims=True)
    return yc * lax.rsqrt(var + eps)


def _ret_prompt_kernel(qk_ref, vb_ref, cos_ref, sin_ref, dmat_ref, cross_ref, kdec_ref, sdec_ref,
                       o_ref, sfin_ref, s_scr):
    j = pl.program_id(1)

    @pl.when(j == 0)
    def _():
        s_scr[...] = jnp.zeros_like(s_scr)

    cos2, sin2 = cos_ref[...], sin_ref[...]
    for h in range(HB):
        q = _rotary(qk_ref[:, h * DK:(h + 1) * DK], cos2, sin2)
        k = _rotary(qk_ref[:, HB * DK + h * DK:HB * DK + (h + 1) * DK], cos2, sin2) * (DK ** -0.5)
        v = vb_ref[:, h * DV:(h + 1) * DV].astype(BF16)
        qb = q.astype(BF16)
        scores = lax.dot_general(qb, k.astype(BF16), (((1,), (1,)), ((), ())),
                                 preferred_element_type=F32) * dmat_ref[h]
        s = s_scr[h]
        y = (jnp.dot(scores.astype(BF16), v, preferred_element_type=F32)
             + jnp.dot(qb, s.astype(BF16), preferred_element_type=F32) * cross_ref[h])
        kd = (k * kdec_ref[h]).astype(BF16)
        s_scr[h] = s * sdec_ref[h] + lax.dot_general(kd, v, (((0,), (0,)), ((), ())),
                                                     preferred_element_type=F32)
        o_ref[:, h * DV:(h + 1) * DV] = _ln_rows(y, LN_EPS)

    @pl.when(j == pl.num_programs(1) - 1)
    def _():
        sfin_ref[...] = s_scr[...]


def _ret_prompt(qk, vb, nseq, t):
    chunks = t // RET_CHUNK
    cos2, sin2 = _rope_tables(jnp.arange(t, dtype=F32) + 0)
    dmat, cross, kdec, sdec = _decay_tables(RET_CHUNK)
    row_spec = pl.BlockSpec((RET_CHUNK, D), lambda n, j: (n * chunks + j, 0))
    tab_spec = pl.BlockSpec((RET_CHUNK, DK), lambda n, j: (j, 0))
    return pl.pallas_call(
        _ret_prompt_kernel,
        grid=(nseq, chunks),
        in_specs=[row_spec, row_spec, tab_spec, tab_spec,
                  _const_spec((HB, RET_CHUNK, RET_CHUNK)), _const_spec((HB, RET_CHUNK, 1)),
                  _const_spec((HB, RET_CHUNK, 1)),
                  pl.BlockSpec(memory_space=pltpu.SMEM)],
        out_specs=[row_spec, pl.BlockSpec((None, HB, DK, DV), lambda n, j: (n, 0, 0, 0))],
        out_shape=[jax.ShapeDtypeStruct(qk.shape, F32),
                   jax.ShapeDtypeStruct((nseq, HB, DK, DV), F32)],
        scratch_shapes=[pltpu.VMEM((HB, DK, DV), F32)],
        compiler_params=pltpu.CompilerParams(dimension_semantics=("arbitrary", "arbitrary"),
                                             vmem_limit_bytes=VMEM_LIMIT),
        name="ret_prompt",
    )(qk, vb, cos2, sin2, dmat, cross, kdec, sdec)


def _ret_sample_kernel(qk_ref, vb_ref, cos_ref, sin_ref, s_ref, dmat_ref, cross_ref, kdec_ref, sdec_ref,
                       o_ref, snew_ref, *, nb):
    cos2, sin2 = cos_ref[...], sin_ref[...]
    for n in range(nb):
        for h in range(HB):
            q = _rotary(qk_ref[:, n, h * DK:(h + 1) * DK], cos2, sin2)
            k = _rotary(qk_ref[:, n, HB * DK + h * DK:HB * DK + (h + 1) * DK], cos2, sin2) * (DK ** -0.5)
            v = vb_ref[:, n, h * DV:(h + 1) * DV].astype(BF16)
            qb = q.astype(BF16)
            scores = lax.dot_general(qb, k.astype(BF16), (((1,), (1,)), ((), ())),
                                     preferred_element_type=F32) * dmat_ref[h]
            s = s_ref[n, h]
            y = (jnp.dot(scores.astype(BF16), v, preferred_element_type=F32)
                 + jnp.dot(qb, s.astype(BF16), preferred_element_type=F32) * cross_ref[h])
            kd = (k * kdec_ref[h]).astype(BF16)
            snew_ref[n, h] = s * sdec_ref[h] + lax.dot_general(kd, v, (((0,), (0,)), ((), ())),
                                                               preferred_element_type=F32)
            o_ref[:, n, h * DV:(h + 1) * DV] = _ln_rows(y, LN_EPS)


def _ret_sample(qk3, vb3, s0, pos0, *, nb):
    t, nseq, _ = qk3.shape
    cos2, sin2 = _rope_tables(jnp.arange(t, dtype=F32) + pos0)
    dmat, cross, kdec, sdec = _decay_tables(t)
    row_spec = pl.BlockSpec((t, nb, D), lambda i: (0, i, 0))
    st_spec = pl.BlockSpec((nb, HB, DK, DV), lambda i: (i, 0, 0, 0))
    return pl.pallas_call(
        functools.partial(_ret_sample_kernel, nb=nb),
        grid=(nseq // nb,),
        in_specs=[row_spec, row_spec, _const_spec((t, DK)), _const_spec((t, DK)), st_spec,
                  _const_spec((HB, t, t)), _const_spec((HB, t, 1)), _const_spec((HB, t, 1)),
                  pl.BlockSpec(memory_space=pltpu.SMEM)],
        out_specs=[row_spec, st_spec],
        out_shape=[jax.ShapeDtypeStruct(qk3.shape, F32), jax.ShapeDtypeStruct(s0.shape, F32)],
        compiler_params=pltpu.CompilerParams(dimension_semantics=("arbitrary",),
                                             vmem_limit_bytes=VMEM_LIMIT),
        name="ret_sample",
    )(qk3, vb3, cos2, sin2, s0, dmat, cross, kdec, sdec)


def _post_kernel(ga_ref, ya_ref, gb_ref, ob_ref, x_ref, gate_ref, wout_ref, lng_ref, lnb_ref, y_ref,
                 *, gs, alpha):
    merged = ga_ref[...] * ya_ref[...] + gb_ref[...] * ob_ref[...]
    sub = jnp.dot(merged.astype(BF16), wout_ref[...], preferred_element_type=F32)
    x = x_ref[...]
    gate = gate_ref[...]
    if gs == 1:
        z = alpha * x + gate * sub
    else:
        tt = x.shape[0]
        z = (alpha * x.reshape(tt // gs, gs, D) + gate[None] * sub.reshape(tt // gs, gs, D)).reshape(tt, D)
    y_ref[...] = _ln_rows(z, LN_EPS) * lng_ref[...] + lnb_ref[...]


def _post(ga, ya, gb, ob, x2, gate, wout, ln_gain, ln_bias, *, nseq_grid, tiles, tt, gs, alpha):
    row_spec = pl.BlockSpec((tt, D), lambda n, j: (n * tiles + j, 0))
    return pl.pallas_call(
        functools.partial(_post_kernel, gs=gs, alpha=alpha),
        grid=(nseq_grid, tiles),
        in_specs=[row_spec] * 5 + [pl.BlockSpec((None, gs, D), lambda n, j: (n, 0, 0)),
                                   _const_spec((D, D)), _const_spec((1, D)), _const_spec((1, D))],
        out_specs=row_spec,
        out_shape=jax.ShapeDtypeStruct(x2.shape, F32),
        compiler_params=pltpu.CompilerParams(dimension_semantics=("arbitrary", "arbitrary"),
                                             vmem_limit_bytes=VMEM_LIMIT),
        name="post",
    )(ga, ya, gb, ob, x2, gate, wout, ln_gain, ln_bias)


def _head_param_lanes(p, nseq):
    return jnp.tile(p.reshape(HA, KA).T, (1, nseq))


def _layer(x2, ada, first_prev, s_wkv, s_ret, pos0, weights, *, nseq, t, seq_major, tt, tc, depth):
    (wsh, wrest, pre_params, gn_gain, gn_bias, r_k, wout, ln_gain, ln_bias) = weights
    if seq_major:
        nseq_grid, gs, tiles = nseq, 1, t // tt
    else:
        nseq_grid, gs, tiles = 1, nseq, (t * nseq) // tt
    shift_c = ada[:, :D].reshape(nseq_grid, gs, D)
    scale_c = ada[:, D:2 * D].reshape(nseq_grid, gs, D)
    gate_c = ada[:, 2 * D:].reshape(nseq_grid, gs, D)
    cfg = dict(nseq_grid=nseq_grid, tiles=tiles, tt=tt, gs=gs)

    r, dec, kmod, v, kkraw, a, new_shift = _pre_a(
        x2, scale_c, shift_c, first_prev.reshape(nseq_grid, gs, SHIFT_W), wsh, pre_params, **cfg)
    ga, gb, qk, vb = _pre_b(x2, scale_c, shift_c, wrest, **cfg)

    s0 = s_wkv.transpose(3, 2, 0, 1).reshape(KA, KA, nseq * HA)
    ya, s_fin = _wkv((r, dec, kmod, v, kkraw, a), s0, _head_param_lanes(gn_gain, nseq),
                     _head_param_lanes(gn_bias, nseq), _head_param_lanes(r_k.reshape(-1), nseq),
                     nseq=nseq, t=t, tc=tc, seq_major=seq_major)
    new_wkv = s_fin.reshape(KA, KA, nseq, HA).transpose(2, 3, 1, 0)

    if seq_major:
        ob, new_ret = _ret_prompt(qk, vb, nseq, t)
    else:
        ob3, new_ret = _ret_sample(qk.reshape(t, nseq, D), vb.reshape(t, nseq, D), s_ret, pos0, nb=8)
        ob = ob3.reshape(t * nseq, D)

    alpha = (2.0 * depth) ** 0.25
    y = _post(ga, ya, gb, ob, x2, gate_c, wout, ln_gain, ln_bias, alpha=alpha, **cfg)
    return y, new_shift.reshape(nseq, SHIFT_W), new_wkv, new_ret


def kernel(x_prompt, x_sample, c_prompt, c_sample, state_shift, state_wkv, state_ret, w_ada, b_ada, w_in,
           mu_shift, w0, w_decay_up, a0, w_icl_up, k_k, k_a, r_k, gn_a_gain, gn_a_bias, w_out, ln_gain, ln_bias):
    depth = w_in.shape[0]
    nb, t_p, _ = x_prompt.shape
    ns, t_s, _ = x_sample.shape
    past_len = 16384
    h_p = x_prompt.reshape(nb * t_p, D)
    h_s = x_sample.transpose(1, 0, 2).reshape(t_s * ns, D)
    outs = [[] for _ in range(6)]
    zero_lora = jnp.zeros((LORA, D), F32)
    for l in range(depth):
        row = lambda p: p[l].reshape(1, -1)
        wsh = w_in[l][:, :SHIFT_W].astype(BF16)
        wrest = w_in[l][:, SHIFT_W:].astype(BF16)
        wd_pad = jnp.concatenate([w_decay_up[l], zero_lora], axis=0).astype(BF16)
        wa_pad = jnp.concatenate([zero_lora, w_icl_up[l]], axis=0).astype(BF16)
        pre_params = (row(mu_shift), row(w0), row(a0), row(k_k), row(k_a), wd_pad, wa_pad)
        weights = (wsh, wrest, pre_params, gn_a_gain[l], gn_a_bias[l], r_k[l], w_out[l].astype(BF16),
                   row(ln_gain), row(ln_bias))
        ada = _ada(jnp.concatenate([c_prompt, c_sample], axis=0), w_ada[l], b_ada[l])
        h_p, s1, s2, s3 = _layer(
            h_p, ada[:nb], jnp.zeros((nb, SHIFT_W), F32), jnp.zeros((nb, HA, KA, KA), F32), None, 0,
            weights, nseq=nb, t=t_p, seq_major=True, tt=256, tc=16, depth=depth)
        h_s, t1, t2, t3 = _layer(
            h_s, ada[nb:], state_shift[l], state_wkv[l], state_ret[l], past_len,
            weights, nseq=ns, t=t_s, seq_major=False, tt=t_s * ns, tc=t_s, depth=depth)
        for lst, val in zip(outs, (s1, s2, s3, t1, t2, t3)):
            lst.append(val)
    y_p = h_p.reshape(nb, t_p, D)
    y_s = h_s.reshape(t_s, ns, D).transpose(1, 0, 2)
    return (y_p, y_s) + tuple(jnp.stack(o) for o in outs)
```

```python
import functools
import math

import numpy as np
import jax
import jax.numpy as jnp
from jax import lax
from jax.experimental import pallas as pl
from jax.experimental.pallas import tpu as pltpu

F32 = jnp.float32
BF16 = jnp.bfloat16

D = 1024
HA = 16
KA = 64
LORA = 64
GN_EPS = 64e-5
HB = 4
DK = 128
DV = 256
RET_CHUNK = 128
ROPE_BASE = 10000.0
LN_EPS = 1e-5
SHIFT_W = 3 * D + 2 * LORA
REST_W = 6 * D
LANES = 128
SUBLANES = 8
VMEM_LIMIT = 56 * 1024 * 1024


def _sigmoid(x):
    return 1.0 / (1.0 + jnp.exp(-x))


def _bdot(a, b):
    return jnp.dot(a.astype(BF16), b.astype(BF16), preferred_element_type=F32)


def _const_spec(shape):
    nd = len(shape)
    return pl.BlockSpec(shape, lambda *_: (0,) * nd, pipeline_mode=pl.Buffered(1))


def _ada_kernel(c_ref, w_ref, b_ref, o_ref):
    c = c_ref[...]
    s = c * _sigmoid(c)
    o_ref[...] = jnp.dot(s, w_ref[...], preferred_element_type=F32,
                         precision=lax.Precision.HIGHEST) + b_ref[...]


def _ada(c_all, w_ada, b_ada):
    n = c_all.shape[0]
    return pl.pallas_call(
        _ada_kernel,
        grid=(3,),
        in_specs=[pl.BlockSpec((n, D), lambda j: (0, 0)),
                  pl.BlockSpec((D, D), lambda j: (0, j)),
                  pl.BlockSpec((1, D), lambda j: (0, j))],
        out_specs=pl.BlockSpec((n, D), lambda j: (0, j)),
        out_shape=jax.ShapeDtypeStruct((n, 3 * D), F32),
        compiler_params=pltpu.CompilerParams(dimension_semantics=("arbitrary",)),
        name="ada",
    )(c_all, w_ada, b_ada.reshape(1, 3 * D))


def _modulate(x, scale, shift, gs):
    if gs == 1:
        return x * (1.0 + scale) + shift
    tt = x.shape[0]
    x3 = x.reshape(tt // gs, gs, D)
    return (x3 * (1.0 + scale)[None] + shift[None]).reshape(tt, D)


def _put_head_pairs(out_ref, scr, x, y):
    tt = x.shape[0]
    low = lax.broadcasted_iota(jnp.int32, (tt, LANES), 1) < KA
    for m in range(D // LANES):
        xc = x[:, m * LANES:(m + 1) * LANES]
        yc = y[:, m * LANES:(m + 1) * LANES]
        even = jnp.where(low, xc, pltpu.roll(yc, KA, axis=1))
        odd = jnp.where(low, pltpu.roll(xc, KA, axis=1), yc)
        for r in range(tt // SUBLANES):
            rows = slice(r * SUBLANES, (r + 1) * SUBLANES)
            scr[pl.ds((r * HA + 2 * m) * SUBLANES, SUBLANES), :] = even[rows]
            scr[pl.ds((r * HA + 2 * m + 1) * SUBLANES, SUBLANES), :] = odd[rows]

    def row_block(r, carry):
        for s in range(SUBLANES):
            out_ref[r * SUBLANES + s] = scr[pl.ds(r * (HA * SUBLANES) + s, HA, stride=SUBLANES), :]
        return carry

    lax.fori_loop(0, tt // SUBLANES, row_block, 0)


def _pre_a_kernel(x_ref, scale_ref, shift_ref, prev_ref, wsh_ref, mu_ref, w0_ref, a0_ref,
                  kk_ref, ka_ref, wd_ref, wa_ref,
                  rk_out, wk_out, va_out, ns_out,
                  p_scr, pair_scr, *, gs, tt, pad):
    j = pl.program_id(1)
    nj = pl.num_programs(1)
    lo = pad - gs

    @pl.when(j == 0)
    def _():
        p_scr[lo:pad, :] = prev_ref[...]

    u = _modulate(x_ref[...], scale_ref[...], shift_ref[...], gs).astype(BF16)

    def mixed(c0, c1):
        p_scr[pad:pad + tt, c0:c1] = jnp.dot(u, wsh_ref[:, c0:c1], preferred_element_type=F32)
        p = p_scr[pad:pad + tt, c0:c1]
        prev = p_scr[lo:lo + tt, c0:c1]
        return p + (prev - p) * mu_ref[:, c0:c1]

    lora_in = mixed(3 * D, SHIFT_W)
    lora_w = _bdot(jnp.tanh(lora_in), wd_ref[...])
    lora_a = _bdot(lora_in, wa_ref[...])
    zw = -(w0_ref[...] + lora_w)
    softplus = jnp.maximum(zw, 0.0) + jnp.log(1.0 + jnp.exp(-jnp.abs(zw)))
    w_log = -softplus - 0.5
    dec = jnp.exp(-jnp.exp(w_log))
    a = _sigmoid(a0_ref[...] + lora_a)
    k = mixed(D, 2 * D)
    _put_head_pairs(rk_out, pair_scr.at[0], mixed(0, D), k * (1.0 + (a - 1.0) * ka_ref[...]))
    _put_head_pairs(wk_out, pair_scr.at[1], dec, k * kk_ref[...])
    _put_head_pairs(va_out, pair_scr.at[2], mixed(2 * D, 3 * D), a)

    @pl.when(j == nj - 1)
    def _():
        ns_out[...] = p_scr[pad + tt - gs:pad + tt, :]

    p_scr[lo:pad, :] = p_scr[pad + tt - gs:pad + tt, :]


def _pre_a(x2, scale, shift, first_prev, wsh, params, *, nseq_grid, tiles, tt, gs):
    rows = x2.shape[0]
    pad = max(SUBLANES, gs)
    mu, w0, a0, k_k, k_a, wd_pad, wa_pad = params
    row_spec = pl.BlockSpec((tt, D), lambda n, j: (n * tiles + j, 0))
    seq_spec = pl.BlockSpec((None, gs, D), lambda n, j: (n, 0, 0))
    pair_spec = pl.BlockSpec((tt, HA, LANES), lambda n, j: (n * tiles + j, 0, 0))
    pair_sds = jax.ShapeDtypeStruct((rows, HA, LANES), F32)
    kern = functools.partial(_pre_a_kernel, gs=gs, tt=tt, pad=pad)
    return pl.pallas_call(
        kern,
        grid=(nseq_grid, tiles),
        in_specs=[row_spec, seq_spec, seq_spec,
                  pl.BlockSpec((None, gs, SHIFT_W), lambda n, j: (n, 0, 0)),
                  _const_spec((D, SHIFT_W)), _const_spec((1, SHIFT_W)),
                  _const_spec((1, D)), _const_spec((1, D)), _const_spec((1, D)), _const_spec((1, D)),
                  _const_spec((LANES, D)), _const_spec((LANES, D))],
        out_specs=[pair_spec] * 3 + [pl.BlockSpec((None, gs, SHIFT_W), lambda n, j: (n, 0, 0))],
        out_shape=[pair_sds] * 3 + [jax.ShapeDtypeStruct((nseq_grid, gs, SHIFT_W), F32)],
        scratch_shapes=[pltpu.VMEM((pad + tt, SHIFT_W), F32), pltpu.VMEM((3, tt * HA, LANES), F32)],
        compiler_params=pltpu.CompilerParams(dimension_semantics=("arbitrary", "arbitrary"),
                                             vmem_limit_bytes=VMEM_LIMIT),
        name="pre_a",
    )(x2, scale, shift, first_prev, wsh, mu, w0, a0, k_k, k_a, wd_pad, wa_pad)


def _pre_b_kernel(x_ref, scale_ref, shift_ref, w_ref, ga_out, gb_out, qk_out, vb_out, *, gs):
    u = _modulate(x_ref[...], scale_ref[...], shift_ref[...], gs).astype(BF16)

    def proj(c0, c1):
        return jnp.dot(u, w_ref[:, c0:c1], preferred_element_type=F32)

    z_a = proj(0, D)
    g_a = proj(4 * D, 5 * D)
    ga_out[...] = _sigmoid(g_a) * (z_a * _sigmoid(z_a))
    z_b = proj(3 * D, 4 * D)
    g_b = proj(5 * D, 6 * D)
    gb_out[...] = _sigmoid(g_b) * (z_b * _sigmoid(z_b))
    qk_out[...] = proj(D, 2 * D)
    vb_out[...] = proj(2 * D, 3 * D)


def _pre_b(x2, scale, shift, wrest, *, nseq_grid, tiles, tt, gs):
    rows = x2.shape[0]
    row_spec = pl.BlockSpec((tt, D), lambda n, j: (n * tiles + j, 0))
    seq_spec = pl.BlockSpec((None, gs, D), lambda n, j: (n, 0, 0))
    out_sds = jax.ShapeDtypeStruct((rows, D), F32)
    return pl.pallas_call(
        functools.partial(_pre_b_kernel, gs=gs),
        grid=(nseq_grid, tiles),
        in_specs=[row_spec, seq_spec, seq_spec, _const_spec((D, REST_W))],
        out_specs=[row_spec] * 4,
        out_shape=[out_sds] * 4,
        compiler_params=pltpu.CompilerParams(dimension_semantics=("arbitrary", "arbitrary"),
                                             vmem_limit_bytes=VMEM_LIMIT),
        name="pre_b",
    )(x2, scale, shift, wrest)


V_KKN, V_B, V_W, V_K, V_R, V_V = range(6)


def _wkv_kernel(rk_ref, wk_ref, va_ref, s0_ref, gain_ref, bias_ref, rkp_ref,
                o_ref, sfin_ref, s_scr, vec_scr, row_scr, y_scr, cur_scr, *, tc, seq_major):
    j = pl.program_id(1)

    @pl.when(j == 0)
    def _():
        s_scr[...] = s0_ref[...]

    def load_pair(ref, t):
        x = ref[:, t] if seq_major else ref[t]
        xt = x.reshape(LANES, LANES).T
        return xt[:KA], xt[KA:]

    def prep(t, carry):
        dec, kkraw = load_pair(wk_ref, t)
        ss = jnp.sum(kkraw * kkraw, axis=0, keepdims=True)
        kkn = kkraw * (1.0 / jnp.maximum(jnp.sqrt(ss), 1e-12))
        v, a = load_pair(va_ref, t)
        r, kmod = load_pair(rk_ref, t)
        vec_scr[t, V_KKN] = kkn
        vec_scr[t, V_B] = kkn * a
        vec_scr[t, V_W] = dec
        vec_scr[t, V_K] = kmod
        vec_scr[t, V_R] = r
        vec_scr[t, V_V] = v
        rk = jnp.sum(r * kmod * rkp_ref[...], axis=0, keepdims=True)
        row_scr[t] = jnp.broadcast_to(rk, (SUBLANES, LANES))
        return carry

    def state_pass(t, carry):
        cur_scr[...] = vec_scr[t]
        skk = jnp.zeros((KA, LANES), F32)
        for k in range(KA):
            skk = skk + s_scr[k] * cur_scr[V_KKN, k:k + 1, :]
        vv = cur_scr[V_V]
        y = jnp.zeros((KA, LANES), F32)
        for k in range(KA):
            s_new = (s_scr[k] * cur_scr[V_W, k:k + 1, :] - skk * cur_scr[V_B, k:k + 1, :]
                     + vv * cur_scr[V_K, k:k + 1, :])
            s_scr[k] = s_new
            y = y + s_new * cur_scr[V_R, k:k + 1, :]
        y_scr[t] = y
        return carry

    def out_token(t):
        y = y_scr[t]
        m = jnp.mean(y, axis=0, keepdims=True)
        yc = y - m
        var = jnp.mean(yc * yc, axis=0, keepdims=True)
        yn = yc * lax.rsqrt(var + GN_EPS) * gain_ref[...] + bias_ref[...]
        return yn + row_scr[t, 0:1, :] * vec_scr[t, V_V]

    def finish(i, carry):
        o2 = jnp.concatenate([out_token(i), out_token(i + tc // 2)], axis=0)
        o = o2.T.reshape(LANES // HA, HA, LANES)
        if seq_major:
            o_ref[:, i] = o
        else:
            o_ref[i] = o
        return carry

    lax.fori_loop(0, tc, prep, 0, unroll=2)
    lax.fori_loop(0, tc, state_pass, 0)
    lax.fori_loop(0, tc // 2, finish, 0, unroll=2)

    @pl.when(j == pl.num_programs(1) - 1)
    def _():
        sfin_ref[...] = s_scr[...]


def _wkv(pairs, s0, gain_l, bias_l, rk_l, *, nseq, t, tc, seq_major):
    nsl = LANES // HA
    groups = nseq // nsl
    if seq_major:
        shape4, oshape4 = (nseq, t, HA, LANES), (nseq, t // 2, HA, LANES)
        vec_spec = pl.BlockSpec((nsl, tc, HA, LANES), lambda g, j: (g, j, 0, 0))
        out_spec = pl.BlockSpec((nsl, tc // 2, HA, LANES), lambda g, j: (g, j, 0, 0))
    else:
        shape4, oshape4 = (t, nseq, HA, LANES), (t // 2, nseq, HA, LANES)
        vec_spec = pl.BlockSpec((tc, nsl, HA, LANES), lambda g, j: (j, g, 0, 0))
        out_spec = pl.BlockSpec((tc // 2, nsl, HA, LANES), lambda g, j: (j, g, 0, 0))
    st_spec = pl.BlockSpec((KA, KA, LANES), lambda g, j: (0, 0, g))
    par_spec = pl.BlockSpec((KA, LANES), lambda g, j: (0, g))
    o4, s_fin = pl.pallas_call(
        functools.partial(_wkv_kernel, tc=tc, seq_major=seq_major),
        grid=(groups, t // tc),
        in_specs=[vec_spec] * 3 + [st_spec, par_spec, par_spec, par_spec],
        out_specs=[out_spec, st_spec],
        out_shape=[jax.ShapeDtypeStruct(oshape4, F32),
                   jax.ShapeDtypeStruct((KA, KA, nseq * HA), F32)],
        scratch_shapes=[pltpu.VMEM((KA, KA, LANES), F32), pltpu.VMEM((tc, 6, KA, LANES), F32),
                        pltpu.VMEM((tc, SUBLANES, LANES), F32), pltpu.VMEM((tc, KA, LANES), F32),
                        pltpu.VMEM((6, KA, LANES), F32)],
        compiler_params=pltpu.CompilerParams(dimension_semantics=("arbitrary", "arbitrary"),
                                             vmem_limit_bytes=VMEM_LIMIT),
        name="wkv",
    )(*[z.reshape(shape4) for z in pairs], s0, gain_l, bias_l, rk_l)
    return o4.reshape(nseq * t // 2, HA, LANES), s_fin


def _log_gamma():
    return [math.log1p(-2.0 ** (-5.0 - h)) for h in range(HB)]


def _rope_tables(pos):
    half = DK // 2
    theta = 1.0 / (ROPE_BASE ** jnp.linspace(0.0, 1.0, half, dtype=F32))
    ang = pos[:, None] * theta[None, :]
    cos, sin = jnp.cos(ang), jnp.sin(ang)
    return jnp.concatenate([cos, cos], axis=1), jnp.concatenate([-sin, sin], axis=1)


def _decay_tables(chunk):
    lg = jnp.asarray(_log_gamma(), F32)
    idx = jnp.arange(chunk, dtype=F32)
    diff = idx[:, None] - idx[None, :]
    dmat = jnp.exp(jnp.maximum(diff, 0.0)[None] * lg[:, None, None]) * (diff >= 0)[None]
    cross = jnp.exp((idx[None, :] + 1.0) * lg[:, None])[:, :, None]
    kdec = jnp.exp((chunk - 1.0 - idx)[None, :] * lg[:, None])[:, :, None]
    sdec = jnp.exp(chunk * lg)
    return dmat.astype(F32), cross, kdec, sdec


def _rotary(x, cos2, sin2):
    return x * cos2 + pltpu.roll(x, DK // 2, axis=1) * sin2


def _ln_rows(y, eps):
    m = jnp.mean(y, axis=-1, keepdims=True)
    yc = y - m
    var = jnp.mean(yc * yc, axis=-1, keepdims=True)
    return yc * lax.rsqrt(var + eps)


def _ret_prompt_kernel(qk_ref, vb_ref, cos_ref, sin_ref, dmat_ref, cross_ref, kdec_ref, sdec_ref,
                       o_ref, sfin_ref, s_scr):
    j = pl.program_id(1)

    @pl.when(j == 0)
    def _():
        s_scr[...] = jnp.zeros_like(s_scr)

    cos2, sin2 = cos_ref[...], sin_ref[...]
    for h in range(HB):
        q = _rotary(qk_ref[:, h * DK:(h + 1) * DK], cos2, sin2)
        k = _rotary(qk_ref[:, HB * DK + h * DK:HB * DK + (h + 1) * DK], cos2, sin2) * (DK ** -0.5)
        v = vb_ref[:, h * DV:(h + 1) * DV].astype(BF16)
        qb = q.astype(BF16)
        scores = lax.dot_general(qb, k.astype(BF16), (((1,), (1,)), ((), ())),
                                 preferred_element_type=F32) * dmat_ref[h]
        s = s_scr[h]
        y = (jnp.dot(scores.astype(BF16), v, preferred_element_type=F32)
             + jnp.dot(qb, s.astype(BF16), preferred_element_type=F32) * cross_ref[h])
        kd = (k * kdec_ref[h]).astype(BF16)
        s_scr[h] = s * sdec_ref[h] + lax.dot_general(kd, v, (((0,), (0,)), ((), ())),
                                                     preferred_element_type=F32)
        o_ref[:, h * DV:(h + 1) * DV] = _ln_rows(y, LN_EPS)

    @pl.when(j == pl.num_programs(1) - 1)
    def _():
        sfin_ref[...] = s_scr[...]


def _ret_prompt(qk, vb, nseq, t):
    chunks = t // RET_CHUNK
    cos2, sin2 = _rope_tables(jnp.arange(t, dtype=F32) + 0)
    dmat, cross, kdec, sdec = _decay_tables(RET_CHUNK)
    row_spec = pl.BlockSpec((RET_CHUNK, D), lambda n, j: (n * chunks + j, 0))
    tab_spec = pl.BlockSpec((RET_CHUNK, DK), lambda n, j: (j, 0))
    return pl.pallas_call(
        _ret_prompt_kernel,
        grid=(nseq, chunks),
        in_specs=[row_spec, row_spec, tab_spec, tab_spec,
                  _const_spec((HB, RET_CHUNK, RET_CHUNK)), _const_spec((HB, RET_CHUNK, 1)),
                  _const_spec((HB, RET_CHUNK, 1)),
                  pl.BlockSpec(memory_space=pltpu.SMEM)],
        out_specs=[row_spec, pl.BlockSpec((None, HB, DK, DV), lambda n, j: (n, 0, 0, 0))],
        out_shape=[jax.ShapeDtypeStruct(qk.shape, F32),
                   jax.ShapeDtypeStruct((nseq, HB, DK, DV), F32)],
        scratch_shapes=[pltpu.VMEM((HB, DK, DV), F32)],
        compiler_params=pltpu.CompilerParams(dimension_semantics=("arbitrary", "arbitrary"),
                                             vmem_limit_bytes=VMEM_LIMIT),
        name="ret_prompt",
    )(qk, vb, cos2, sin2, dmat, cross, kdec, sdec)


def _ret_sample_kernel(qk_ref, vb_ref, cos_ref, sin_ref, s_ref, dmat_ref, cross_ref, kdec_ref, sdec_ref,
                       o_ref, snew_ref, *, nb):
    cos2, sin2 = cos_ref[...], sin_ref[...]
    for n in range(nb):
        for h in range(HB):
            q = _rotary(qk_ref[:, n, h * DK:(h + 1) * DK], cos2, sin2)
            k = _rotary(qk_ref[:, n, HB * DK + h * DK:HB * DK + (h + 1) * DK], cos2, sin2) * (DK ** -0.5)
            v = vb_ref[:, n, h * DV:(h + 1) * DV].astype(BF16)
            qb = q.astype(BF16)
            scores = lax.dot_general(qb, k.astype(BF16), (((1,), (1,)), ((), ())),
                                     preferred_element_type=F32) * dmat_ref[h]
            s = s_ref[n, h]
            y = (jnp.dot(scores.astype(BF16), v, preferred_element_type=F32)
                 + jnp.dot(qb, s.astype(BF16), preferred_element_type=F32) * cross_ref[h])
            kd = (k * kdec_ref[h]).astype(BF16)
            snew_ref[n, h] = s * sdec_ref[h] + lax.dot_general(kd, v, (((0,), (0,)), ((), ())),
                                                               preferred_element_type=F32)
            o_ref[:, n, h * DV:(h + 1) * DV] = _ln_rows(y, LN_EPS)


def _ret_sample(qk3, vb3, s0, pos0, *, nb):
    t, nseq, _ = qk3.shape
    cos2, sin2 = _rope_tables(jnp.arange(t, dtype=F32) + pos0)
    dmat, cross, kdec, sdec = _decay_tables(t)
    row_spec = pl.BlockSpec((t, nb, D), lambda i: (0, i, 0))
    st_spec = pl.BlockSpec((nb, HB, DK, DV), lambda i: (i, 0, 0, 0))
    return pl.pallas_call(
        functools.partial(_ret_sample_kernel, nb=nb),
        grid=(nseq // nb,),
        in_specs=[row_spec, row_spec, _const_spec((t, DK)), _const_spec((t, DK)), st_spec,
                  _const_spec((HB, t, t)), _const_spec((HB, t, 1)), _const_spec((HB, t, 1)),
                  pl.BlockSpec(memory_space=pltpu.SMEM)],
        out_specs=[row_spec, st_spec],
        out_shape=[jax.ShapeDtypeStruct(qk3.shape, F32), jax.ShapeDtypeStruct(s0.shape, F32)],
        compiler_params=pltpu.CompilerParams(dimension_semantics=("arbitrary",),
                                             vmem_limit_bytes=VMEM_LIMIT),
        name="ret_sample",
    )(qk3, vb3, cos2, sin2, s0, dmat, cross, kdec, sdec)


def _post_kernel(ga_ref, ya_ref, gb_ref, ob_ref, x_ref, gate_ref, wout_ref, lng_ref, lnb_ref, y_ref,
                 ya_scr, *, gs, alpha, pair_rows):
    half = ya_ref.shape[0]
    groups = half // pair_rows
    low = lax.broadcasted_iota(jnp.int32, (half, LANES), 1) < KA
    for m in range(D // LANES):
        ha = ya_ref[:, 2 * m, :]
        hb = ya_ref[:, 2 * m + 1, :]
        lo = jnp.where(low, ha, pltpu.roll(hb, KA, axis=1))
        hi = jnp.where(low, pltpu.roll(ha, KA, axis=1), hb)
        ya_scr[:, 0, :, m * LANES:(m + 1) * LANES] = lo.reshape(groups, pair_rows, LANES)
        ya_scr[:, 1, :, m * LANES:(m + 1) * LANES] = hi.reshape(groups, pair_rows, LANES)
    ya = ya_scr[...].reshape(2 * half, D)
    merged = ga_ref[...] * ya + gb_ref[...] * ob_ref[...]
    sub = jnp.dot(merged.astype(BF16), wout_ref[...], preferred_element_type=F32)
    x = x_ref[...]
    gate = gate_ref[...]
    if gs == 1:
        z = alpha * x + gate * sub
    else:
        tt = x.shape[0]
        z = (alpha * x.reshape(tt // gs, gs, D) + gate[None] * sub.reshape(tt // gs, gs, D)).reshape(tt, D)
    y_ref[...] = _ln_rows(z, LN_EPS) * lng_ref[...] + lnb_ref[...]


def _post(ga, ya, gb, ob, x2, gate, wout, ln_gain, ln_bias, *, nseq_grid, tiles, tt, gs, alpha, pair_rows):
    row_spec = pl.BlockSpec((tt, D), lambda n, j: (n * tiles + j, 0))
    pair_spec = pl.BlockSpec((tt // 2, HA, LANES), lambda n, j: (n * tiles + j, 0, 0))
    return pl.pallas_call(
        functools.partial(_post_kernel, gs=gs, alpha=alpha, pair_rows=pair_rows),
        grid=(nseq_grid, tiles),
        in_specs=[row_spec, pair_spec, row_spec, row_spec, row_spec,
                  pl.BlockSpec((None, gs, D), lambda n, j: (n, 0, 0)),
                  _const_spec((D, D)), _const_spec((1, D)), _const_spec((1, D))],
        out_specs=row_spec,
        out_shape=jax.ShapeDtypeStruct(x2.shape, F32),
        scratch_shapes=[pltpu.VMEM((tt // 2 // pair_rows, 2, pair_rows, D), F32)],
        compiler_params=pltpu.CompilerParams(dimension_semantics=("arbitrary", "arbitrary"),
                                             vmem_limit_bytes=VMEM_LIMIT),
        name="post",
    )(ga, ya, gb, ob, x2, gate, wout, ln_gain, ln_bias)


def _head_param_lanes(p, nseq):
    return jnp.tile(p.reshape(HA, KA).T, (1, nseq))


def _layer(x2, ada, first_prev, s_wkv, s_ret, pos0, weights, *, nseq, t, seq_major, tt, tc, depth):
    (wsh, wrest, pre_params, gn_gain, gn_bias, r_k, wout, ln_gain, ln_bias) = weights
    if seq_major:
        nseq_grid, gs, tiles = nseq, 1, t // tt
    else:
        nseq_grid, gs, tiles = 1, nseq, (t * nseq) // tt
    shift_c = ada[:, :D].reshape(nseq_grid, gs, D)
    scale_c = ada[:, D:2 * D].reshape(nseq_grid, gs, D)
    gate_c = ada[:, 2 * D:].reshape(nseq_grid, gs, D)
    cfg = dict(nseq_grid=nseq_grid, tiles=tiles, tt=tt, gs=gs)

    p_rk, p_wk, p_va, new_shift = _pre_a(
        x2, scale_c, shift_c, first_prev.reshape(nseq_grid, gs, SHIFT_W), wsh, pre_params, **cfg)
    ga, gb, qk, vb = _pre_b(x2, scale_c, shift_c, wrest, **cfg)

    s0 = s_wkv.transpose(3, 2, 0, 1).reshape(KA, KA, nseq * HA)
    ya, s_fin = _wkv((p_rk, p_wk, p_va), s0, _head_param_lanes(gn_gain, nseq),
                     _head_param_lanes(gn_bias, nseq), _head_param_lanes(r_k.reshape(-1), nseq),
                     nseq=nseq, t=t, tc=tc, seq_major=seq_major)
    pair_rows = (tc // 2) * gs
    new_wkv = s_fin.reshape(KA, KA, nseq, HA).transpose(2, 3, 1, 0)

    if seq_major:
        ob, new_ret = _ret_prompt(qk, vb, nseq, t)
    else:
        ob3, new_ret = _ret_sample(qk.reshape(t, nseq, D), vb.reshape(t, nseq, D), s_ret, pos0, nb=8)
        ob = ob3.reshape(t * nseq, D)

    alpha = (2.0 * depth) ** 0.25
    y = _post(ga, ya, gb, ob, x2, gate_c, wout, ln_gain, ln_bias, alpha=alpha, pair_rows=pair_rows, **cfg)
    return y, new_shift.reshape(nseq, SHIFT_W), new_wkv, new_ret


def kernel(x_prompt, x_sample, c_prompt, c_sample, state_shift, state_wkv, state_ret, w_ada, b_ada, w_in,
           mu_shift, w0, w_decay_up, a0, w_icl_up, k_k, k_a, r_k, gn_a_gain, gn_a_bias, w_out, ln_gain, ln_bias):
    depth = w_in.shape[0]
    nb, t_p, _ = x_prompt.shape
    ns, t_s, _ = x_sample.shape
    past_len = 16384
    h_p = x_prompt.reshape(nb * t_p, D)
    h_s = x_sample.transpose(1, 0, 2).reshape(t_s * ns, D)
    outs = [[] for _ in range(6)]
    zero_lora = jnp.zeros((LORA, D), F32)
    for l in range(depth):
        row = lambda p: p[l].reshape(1, -1)
        wsh = w_in[l][:, :SHIFT_W].astype(BF16)
        wrest = w_in[l][:, SHIFT_W:].astype(BF16)
        wd_pad = jnp.concatenate([w_decay_up[l], zero_lora], axis=0).astype(BF16)
        wa_pad = jnp.concatenate([zero_lora, w_icl_up[l]], axis=0).astype(BF16)
        pre_params = (row(mu_shift), row(w0), row(a0), row(k_k), row(k_a), wd_pad, wa_pad)
        weights = (wsh, wrest, pre_params, gn_a_gain[l], gn_a_bias[l], r_k[l], w_out[l].astype(BF16),
                   row(ln_gain), row(ln_bias))
        ada = _ada(jnp.concatenate([c_prompt, c_sample], axis=0), w_ada[l], b_ada[l])
        h_p, s1, s2, s3 = _layer(
            h_p, ada[:nb], jnp.zeros((nb, SHIFT_W), F32), jnp.zeros((nb, HA, KA, KA), F32), None, 0,
            weights, nseq=nb, t=t_p, seq_major=True, tt=256, tc=16, depth=depth)
        h_s, t1, t2, t3 = _layer(
            h_s, ada[nb:], state_shift[l], state_wkv[l], state_ret[l], past_len,
            weights, nseq=ns, t=t_s, seq_major=False, tt=t_s * ns, tc=t_s, depth=depth)
        for lst, val in zip(outs, (s1, s2, s3, t1, t2, t3)):
            lst.append(val)
    y_p = h_p.reshape(nb, t_p, D)
    y_s = h_s.reshape(t_s, ns, D).transpose(1, 0, 2)
    return (y_p, y_s) + tuple(jnp.stack(o) for o in outs)
```

```python
import functools
import math

import numpy as np
import jax
import jax.numpy as jnp
from jax import lax
from jax.experimental import pallas as pl
from jax.experimental.pallas import tpu as pltpu

F32 = jnp.float32
BF16 = jnp.bfloat16

D = 1024
HA = 16
KA = 64
LORA = 64
GN_EPS = 64e-5
HB = 4
DK = 128
DV = 256
RET_CHUNK = 128
ROPE_BASE = 10000.0
LN_EPS = 1e-5
SHIFT_W = 3 * D + 2 * LORA
REST_W = 6 * D
LANES = 128
SUBLANES = 8
VMEM_LIMIT = 56 * 1024 * 1024


def _sigmoid(x):
    return 1.0 / (1.0 + jnp.exp(-x))


def _bdot(a, b):
    return jnp.dot(a.astype(BF16), b.astype(BF16), preferred_element_type=F32)


def _bdot_nt(a, b):
    return lax.dot_general(a.astype(BF16), b.astype(BF16), (((1,), (1,)), ((), ())), preferred_element_type=F32)


def _bdot_tn(a, b):
    return lax.dot_general(a.astype(BF16), b.astype(BF16), (((0,), (0,)), ((), ())), preferred_element_type=F32)


def _split_bf16(x, parts):
    out = []
    for _ in range(parts):
        term = x.astype(BF16)
        out.append(term)
        x = x - term.astype(F32)
    return out


def _const_spec(shape):
    nd = len(shape)
    return pl.BlockSpec(shape, lambda *_: (0,) * nd, pipeline_mode=pl.Buffered(1))


def _ada_kernel(c_ref, w_ref, b_ref, o_ref):
    c = c_ref[...]
    s = c * _sigmoid(c)
    o_ref[...] = jnp.dot(s, w_ref[...], preferred_element_type=F32,
                         precision=lax.Precision.HIGHEST) + b_ref[...]


def _ada(c_all, w_ada, b_ada):
    n = c_all.shape[0]
    return pl.pallas_call(
        _ada_kernel,
        grid=(3,),
        in_specs=[pl.BlockSpec((n, D), lambda j: (0, 0)),
                  pl.BlockSpec((D, D), lambda j: (0, j)),
                  pl.BlockSpec((1, D), lambda j: (0, j))],
        out_specs=pl.BlockSpec((n, D), lambda j: (0, j)),
        out_shape=jax.ShapeDtypeStruct((n, 3 * D), F32),
        compiler_params=pltpu.CompilerParams(dimension_semantics=("arbitrary",)),
        name="ada",
    )(c_all, w_ada, b_ada.reshape(1, 3 * D))


def _modulate(x, scale, shift, gs):
    if gs == 1:
        return x * (1.0 + scale) + shift
    tt = x.shape[0]
    x3 = x.reshape(tt // gs, gs, D)
    return (x3 * (1.0 + scale)[None] + shift[None]).reshape(tt, D)


def _put_head_pairs(out_ref, scr, x, y):
    tt = x.shape[0]
    low = lax.broadcasted_iota(jnp.int32, (tt, LANES), 1) < KA
    for m in range(D // LANES):
        xc = x[:, m * LANES:(m + 1) * LANES]
        yc = y[:, m * LANES:(m + 1) * LANES]
        even = jnp.where(low, xc, pltpu.roll(yc, KA, axis=1))
        odd = jnp.where(low, pltpu.roll(xc, KA, axis=1), yc)
        for r in range(tt // SUBLANES):
            rows = slice(r * SUBLANES, (r + 1) * SUBLANES)
            scr[pl.ds((r * HA + 2 * m) * SUBLANES, SUBLANES), :] = even[rows]
            scr[pl.ds((r * HA + 2 * m + 1) * SUBLANES, SUBLANES), :] = odd[rows]

    def row_block(r, carry):
        for s in range(SUBLANES):
            out_ref[r * SUBLANES + s] = scr[pl.ds(r * (HA * SUBLANES) + s, HA, stride=SUBLANES), :]
        return carry

    lax.fori_loop(0, tt // SUBLANES, row_block, 0)


def _pre_a_kernel(x_ref, scale_ref, shift_ref, prev_ref, wsh_ref, mu_ref, w0_ref, a0_ref,
                  kk_ref, ka_ref, wd_ref, wa_ref, *rest, gs, tt, pad, pair_out):
    if pair_out:
        rk_out, wk_out, va_out, ns_out, p_scr, pair_scr = rest
    else:
        r_out, lw_out, kmod_out, v_out, kkraw_out, a_out, ns_out, p_scr = rest
    j = pl.program_id(1)
    nj = pl.num_programs(1)
    lo = pad - gs

    @pl.when(j == 0)
    def _():
        p_scr[lo:pad, :] = prev_ref[...]

    u = _modulate(x_ref[...], scale_ref[...], shift_ref[...], gs).astype(BF16)

    def mixed(c0, c1):
        p_scr[pad:pad + tt, c0:c1] = jnp.dot(u, wsh_ref[:, c0:c1], preferred_element_type=F32)
        p = p_scr[pad:pad + tt, c0:c1]
        prev = p_scr[lo:lo + tt, c0:c1]
        return p + (prev - p) * mu_ref[:, c0:c1]

    lora_in = mixed(3 * D, SHIFT_W)
    lora_w = _bdot(jnp.tanh(lora_in), wd_ref[...])
    lora_a = _bdot(lora_in, wa_ref[...])
    zw = -(w0_ref[...] + lora_w)
    softplus = jnp.maximum(zw, 0.0) + jnp.log(1.0 + jnp.exp(-jnp.abs(zw)))
    w_log = -softplus - 0.5
    log_decay = -jnp.exp(w_log)
    a = _sigmoid(a0_ref[...] + lora_a)
    k = mixed(D, 2 * D)
    if pair_out:
        _put_head_pairs(rk_out, pair_scr.at[0], mixed(0, D), k * (1.0 + (a - 1.0) * ka_ref[...]))
        _put_head_pairs(wk_out, pair_scr.at[1], jnp.exp(log_decay), k * kk_ref[...])
        _put_head_pairs(va_out, pair_scr.at[2], mixed(2 * D, 3 * D), a)
    else:
        lw_out[...] = log_decay
        a_out[...] = a
        kkraw_out[...] = k * kk_ref[...]
        kmod_out[...] = k * (1.0 + (a - 1.0) * ka_ref[...])
        r_out[...] = mixed(0, D)
        v_out[...] = mixed(2 * D, 3 * D)

    @pl.when(j == nj - 1)
    def _():
        ns_out[...] = p_scr[pad + tt - gs:pad + tt, :]

    p_scr[lo:pad, :] = p_scr[pad + tt - gs:pad + tt, :]


def _pre_a(x2, scale, shift, first_prev, wsh, params, *, nseq_grid, tiles, tt, gs, pair_out):
    rows = x2.shape[0]
    pad = max(SUBLANES, gs)
    mu, w0, a0, k_k, k_a, wd_pad, wa_pad = params
    row_spec = pl.BlockSpec((tt, D), lambda n, j: (n * tiles + j, 0))
    seq_spec = pl.BlockSpec((None, gs, D), lambda n, j: (n, 0, 0))
    scratch = [pltpu.VMEM((pad + tt, SHIFT_W), F32)]
    if pair_out:
        vec_specs = [pl.BlockSpec((tt, HA, LANES), lambda n, j: (n * tiles + j, 0, 0))] * 3
        vec_sds = [jax.ShapeDtypeStruct((rows, HA, LANES), F32)] * 3
        scratch.append(pltpu.VMEM((3, tt * HA, LANES), F32))
    else:
        vec_specs = [row_spec] * 6
        vec_sds = [jax.ShapeDtypeStruct((rows, D), F32)] * 6
    kern = functools.partial(_pre_a_kernel, gs=gs, tt=tt, pad=pad, pair_out=pair_out)
    return pl.pallas_call(
        kern,
        grid=(nseq_grid, tiles),
        in_specs=[row_spec, seq_spec, seq_spec,
                  pl.BlockSpec((None, gs, SHIFT_W), lambda n, j: (n, 0, 0)),
                  _const_spec((D, SHIFT_W)), _const_spec((1, SHIFT_W)),
                  _const_spec((1, D)), _const_spec((1, D)), _const_spec((1, D)), _const_spec((1, D)),
                  _const_spec((LANES, D)), _const_spec((LANES, D))],
        out_specs=vec_specs + [pl.BlockSpec((None, gs, SHIFT_W), lambda n, j: (n, 0, 0))],
        out_shape=vec_sds + [jax.ShapeDtypeStruct((nseq_grid, gs, SHIFT_W), F32)],
        scratch_shapes=scratch,
        compiler_params=pltpu.CompilerParams(dimension_semantics=("arbitrary", "arbitrary"),
                                             vmem_limit_bytes=VMEM_LIMIT),
        name="pre_a",
    )(x2, scale, shift, first_prev, wsh, mu, w0, a0, k_k, k_a, wd_pad, wa_pad)


def _pre_b_kernel(x_ref, scale_ref, shift_ref, w_ref, ga_out, gb_out, qk_out, vb_out, *, gs):
    u = _modulate(x_ref[...], scale_ref[...], shift_ref[...], gs).astype(BF16)

    def proj(c0, c1):
        return jnp.dot(u, w_ref[:, c0:c1], preferred_element_type=F32)

    z_a = proj(0, D)
    g_a = proj(4 * D, 5 * D)
    ga_out[...] = _sigmoid(g_a) * (z_a * _sigmoid(z_a))
    z_b = proj(3 * D, 4 * D)
    g_b = proj(5 * D, 6 * D)
    gb_out[...] = _sigmoid(g_b) * (z_b * _sigmoid(z_b))
    qk_out[...] = proj(D, 2 * D)
    vb_out[...] = proj(2 * D, 3 * D)


def _pre_b(x2, scale, shift, wrest, *, nseq_grid, tiles, tt, gs):
    rows = x2.shape[0]
    row_spec = pl.BlockSpec((tt, D), lambda n, j: (n * tiles + j, 0))
    seq_spec = pl.BlockSpec((None, gs, D), lambda n, j: (n, 0, 0))
    out_sds = jax.ShapeDtypeStruct((rows, D), F32)
    return pl.pallas_call(
        functools.partial(_pre_b_kernel, gs=gs),
        grid=(nseq_grid, tiles),
        in_specs=[row_spec, seq_spec, seq_spec, _const_spec((D, REST_W))],
        out_specs=[row_spec] * 4,
        out_shape=[out_sds] * 4,
        compiler_params=pltpu.CompilerParams(dimension_semantics=("arbitrary", "arbitrary"),
                                             vmem_limit_bytes=VMEM_LIMIT),
        name="pre_b",
    )(x2, scale, shift, wrest)


V_KKN, V_B, V_W, V_K, V_R, V_V = range(6)


def _wkv_kernel(rk_ref, wk_ref, va_ref, s0_ref, gain_ref, bias_ref, rkp_ref,
                o_ref, sfin_ref, s_scr, vec_scr, row_scr, y_scr, cur_scr, *, tc, seq_major):
    j = pl.program_id(1)

    @pl.when(j == 0)
    def _():
        s_scr[...] = s0_ref[...]

    def load_pair(ref, t):
        x = ref[:, t] if seq_major else ref[t]
        xt = x.reshape(LANES, LANES).T
        return xt[:KA], xt[KA:]

    def prep(t, carry):
        dec, kkraw = load_pair(wk_ref, t)
        ss = jnp.sum(kkraw * kkraw, axis=0, keepdims=True)
        kkn = kkraw * (1.0 / jnp.maximum(jnp.sqrt(ss), 1e-12))
        v, a = load_pair(va_ref, t)
        r, kmod = load_pair(rk_ref, t)
        vec_scr[t, V_KKN] = kkn
        vec_scr[t, V_B] = kkn * a
        vec_scr[t, V_W] = dec
        vec_scr[t, V_K] = kmod
        vec_scr[t, V_R] = r
        vec_scr[t, V_V] = v
        rk = jnp.sum(r * kmod * rkp_ref[...], axis=0, keepdims=True)
        row_scr[t] = jnp.broadcast_to(rk, (SUBLANES, LANES))
        return carry

    def state_pass(t, carry):
        cur_scr[...] = vec_scr[t]
        skk = jnp.zeros((KA, LANES), F32)
        for k in range(KA):
            skk = skk + s_scr[k] * cur_scr[V_KKN, k:k + 1, :]
        vv = cur_scr[V_V]
        y = jnp.zeros((KA, LANES), F32)
        for k in range(KA):
            s_new = (s_scr[k] * cur_scr[V_W, k:k + 1, :] - skk * cur_scr[V_B, k:k + 1, :]
                     + vv * cur_scr[V_K, k:k + 1, :])
            s_scr[k] = s_new
            y = y + s_new * cur_scr[V_R, k:k + 1, :]
        y_scr[t] = y
        return carry

    def out_token(t):
        y = y_scr[t]
        m = jnp.mean(y, axis=0, keepdims=True)
        yc = y - m
        var = jnp.mean(yc * yc, axis=0, keepdims=True)
        yn = yc * lax.rsqrt(var + GN_EPS) * gain_ref[...] + bias_ref[...]
        return yn + row_scr[t, 0:1, :] * vec_scr[t, V_V]

    def finish(i, carry):
        o2 = jnp.concatenate([out_token(i), out_token(i + tc // 2)], axis=0)
        o = o2.T.reshape(LANES // HA, HA, LANES)
        if seq_major:
            o_ref[:, i] = o
        else:
            o_ref[i] = o
        return carry

    lax.fori_loop(0, tc, prep, 0, unroll=2)
    lax.fori_loop(0, tc, state_pass, 0)
    lax.fori_loop(0, tc // 2, finish, 0, unroll=2)

    @pl.when(j == pl.num_programs(1) - 1)
    def _():
        sfin_ref[...] = s_scr[...]


def _wkv(pairs, s0, gain_l, bias_l, rk_l, *, nseq, t, tc, seq_major):
    nsl = LANES // HA
    groups = nseq // nsl
    if seq_major:
        shape4, oshape4 = (nseq, t, HA, LANES), (nseq, t // 2, HA, LANES)
        vec_spec = pl.BlockSpec((nsl, tc, HA, LANES), lambda g, j: (g, j, 0, 0))
        out_spec = pl.BlockSpec((nsl, tc // 2, HA, LANES), lambda g, j: (g, j, 0, 0))
    else:
        shape4, oshape4 = (t, nseq, HA, LANES), (t // 2, nseq, HA, LANES)
        vec_spec = pl.BlockSpec((tc, nsl, HA, LANES), lambda g, j: (j, g, 0, 0))
        out_spec = pl.BlockSpec((tc // 2, nsl, HA, LANES), lambda g, j: (j, g, 0, 0))
    st_spec = pl.BlockSpec((KA, KA, LANES), lambda g, j: (0, 0, g))
    par_spec = pl.BlockSpec((KA, LANES), lambda g, j: (0, g))
    o4, s_fin = pl.pallas_call(
        functools.partial(_wkv_kernel, tc=tc, seq_major=seq_major),
        grid=(groups, t // tc),
        in_specs=[vec_spec] * 3 + [st_spec, par_spec, par_spec, par_spec],
        out_specs=[out_spec, st_spec],
        out_shape=[jax.ShapeDtypeStruct(oshape4, F32),
                   jax.ShapeDtypeStruct((KA, KA, nseq * HA), F32)],
        scratch_shapes=[pltpu.VMEM((KA, KA, LANES), F32), pltpu.VMEM((tc, 6, KA, LANES), F32),
                        pltpu.VMEM((tc, SUBLANES, LANES), F32), pltpu.VMEM((tc, KA, LANES), F32),
                        pltpu.VMEM((6, KA, LANES), F32)],
        compiler_params=pltpu.CompilerParams(dimension_semantics=("arbitrary", "arbitrary"),
                                             vmem_limit_bytes=VMEM_LIMIT),
        name="wkv",
    )(*[z.reshape(shape4) for z in pairs], s0, gain_l, bias_l, rk_l)
    return o4.reshape(nseq * t // 2, HA, LANES), s_fin


CH = 64
NPAIR = D // LANES


def _wkv_chunk_kernel(r_ref, lw_ref, k_ref, v_ref, kk_ref, a_ref, gain_ref, bias_ref, rk_ref, tri_ref, ones_ref,
                      o_ref, tfin_ref, t_scr, *, ct):
    j = pl.program_id(1)

    @pl.when(j == 0)
    def _():
        t_scr[...] = jnp.zeros_like(t_scr)

    lane = lax.broadcasted_iota(jnp.int32, (CH, LANES), 1)
    row = lax.broadcasted_iota(jnp.int32, (CH, LANES), 0)
    low = lane < KA
    col = jnp.bitwise_and(lane, KA - 1)
    strict = row > col
    incl = row >= col
    eye = (row == col).astype(F32)
    low2 = lax.broadcasted_iota(jnp.int32, (LANES, LANES), 1) < KA
    diag_blocks = (lax.broadcasted_iota(jnp.int32, (LANES, LANES), 0) < KA) == low2
    tri = tri_ref[...]
    ones_bd = ones_ref[...]

    def seg_sum(x):
        hi, lo = _split_bf16(x, 2)
        return (jnp.dot(hi, ones_bd, preferred_element_type=F32)
                + jnp.dot(lo, ones_bd, preferred_element_type=F32))

    def bd(z):
        return jnp.concatenate([jnp.where(low, z, 0.0), jnp.where(low, 0.0, z)], axis=0)

    def abd(z):
        return jnp.concatenate([jnp.where(low, 0.0, z), jnp.where(low, z, 0.0)], axis=0)

    def sub_chunk(c, carry):
        rows = pl.ds(pl.multiple_of(c * CH, CH), CH)
        P = range(NPAIR)
        ln = [slice(m * LANES, (m + 1) * LANES) for m in P]
        lw = [lw_ref[rows, ln[m]] for m in P]
        sp = [_split_bf16(lw[m], 3) for m in P]
        cs = [jnp.dot(tri, sp[m][0], preferred_element_type=F32) + jnp.dot(tri, sp[m][1], preferred_element_type=F32)
              + jnp.dot(tri, sp[m][2], preferred_element_type=F32) for m in P]
        kkraw = [kk_ref[rows, ln[m]] for m in P]
        ss = [seg_sum(kkraw[m] * kkraw[m]) for m in P]
        kkn = [kkraw[m] / jnp.maximum(jnp.sqrt(ss[m]), 1e-12) for m in P]
        b = [kkn[m] * a_ref[rows, ln[m]] for m in P]
        r = [r_ref[rows, ln[m]] for m in P]
        k = [k_ref[rows, ln[m]] for m in P]
        v = [v_ref[rows, ln[m]] for m in P]
        cs_last = [cs[m][CH - 1:CH, :] for m in P]
        e_neg = [jnp.exp(-cs[m]) for m in P]
        a_t = [-kkn[m] * jnp.exp(cs[m] - lw[m]) for m in P]
        r_t = [r[m] * jnp.exp(cs[m]) for m in P]
        b_t = [b[m] * e_neg[m] for m in P]
        k_t = [k[m] * e_neg[m] for m in P]
        lhs = [jnp.concatenate([a_t[m], r_t[m]], axis=0) for m in P]
        out0 = [_bdot_nt(jnp.where(low2, lhs[m], 0.0), jnp.concatenate([b_t[m], k_t[m]], axis=0)) for m in P]
        out1 = [_bdot_nt(jnp.where(low2, 0.0, lhs[m]), jnp.concatenate([k_t[m], b_t[m]], axis=0)) for m in P]
        l_cat = [jnp.where(strict, jnp.where(low, out0[m][:CH], out1[m][:CH]), 0.0) for m in P]
        ak_cat = [jnp.where(strict, jnp.where(low, out1[m][:CH], out0[m][:CH]), 0.0) for m in P]
        rb_cat = [jnp.where(incl, jnp.where(low, out0[m][CH:], out1[m][CH:]), 0.0) for m in P]
        rk_cat = [jnp.where(incl, jnp.where(low, out1[m][CH:], out0[m][CH:]), 0.0) for m in P]
        x = [eye + l_cat[m] for m in P]
        lp = l_cat
        for _ in range(5):
            lp = [_bdot(lp[m], bd(lp[m])) for m in P]
            x = [x[m] + _bdot(x[m], bd(lp[m])) for m in P]
        t0 = [t_scr[m] for m in P]
        v_abd = [abd(v[m]) for m in P]
        g = [_bdot(jnp.concatenate([a_t[m], ak_cat[m]], axis=1), jnp.concatenate([t0[m], v_abd[m]], axis=0))
             for m in P]
        u = [_bdot(x[m], bd(g[m])) for m in P]
        y = [_bdot(jnp.concatenate([r_t[m], rb_cat[m], rk_cat[m]], axis=1),
                   jnp.concatenate([t0[m], bd(u[m]), v_abd[m]], axis=0)) for m in P]
        e_end = [jnp.exp(cs_last[m] - cs[m]) for m in P]
        upd = [_bdot_tn(jnp.concatenate([b[m] * e_end[m], k[m] * e_end[m]], axis=0),
                        jnp.concatenate([u[m], v[m]], axis=0)) for m in P]
        for m in P:
            p_col = jnp.broadcast_to(jnp.exp(cs_last[m]), (LANES, LANES)).T
            t_scr[m] = t0[m] * p_col + jnp.where(diag_blocks, upd[m], 0.0)
        mean = [seg_sum(y[m]) * (1.0 / KA) for m in P]
        yc = [y[m] - mean[m] for m in P]
        var = [seg_sum(yc[m] * yc[m]) * (1.0 / KA) for m in P]
        rkv = [seg_sum(r[m] * k[m] * rk_ref[:, ln[m]]) for m in P]
        for m in P:
            yn = yc[m] * lax.rsqrt(var[m] + GN_EPS) * gain_ref[:, ln[m]] + bias_ref[:, ln[m]]
            o_ref[rows, ln[m]] = yn + rkv[m] * v[m]
        return carry

    lax.fori_loop(0, ct // CH, sub_chunk, 0)

    @pl.when(j == pl.num_programs(1) - 1)
    def _():
        tfin_ref[...] = t_scr[...]


def _wkv_chunked(vecs, gain, bias, rk, *, nseq, t, ct):
    tiles = t // ct
    row_spec = pl.BlockSpec((ct, D), lambda n, j: (n * tiles + j, 0))
    tri = jnp.asarray(np.tril(np.ones((CH, CH), np.float32)), BF16)
    head_of_lane = np.arange(LANES) // KA
    ones_bd = jnp.asarray((head_of_lane[:, None] == head_of_lane[None, :]).astype(np.float32), BF16)
    ya, t_fin = pl.pallas_call(
        functools.partial(_wkv_chunk_kernel, ct=ct),
        grid=(nseq, tiles),
        in_specs=[row_spec] * 6 + [_const_spec((1, D))] * 3 + [_const_spec((CH, CH)), _const_spec((LANES, LANES))],
        out_specs=[row_spec, pl.BlockSpec((None, NPAIR, LANES, LANES), lambda n, j: (n, 0, 0, 0))],
        out_shape=[jax.ShapeDtypeStruct((nseq * t, D), F32),
                   jax.ShapeDtypeStruct((nseq, NPAIR, LANES, LANES), F32)],
        scratch_shapes=[pltpu.VMEM((NPAIR, LANES, LANES), F32)],
        compiler_params=pltpu.CompilerParams(dimension_semantics=("arbitrary", "arbitrary"),
                                             vmem_limit_bytes=VMEM_LIMIT),
        name="wkv_chunk",
    )(*vecs, gain, bias, rk, tri, ones_bd)
    t6 = t_fin.reshape(nseq, NPAIR, 2, KA, 2, KA)
    diag = jnp.stack([t6[:, :, 0, :, 0, :], t6[:, :, 1, :, 1, :]], axis=2)
    return ya, diag.reshape(nseq, HA, KA, KA).transpose(0, 1, 3, 2)


def _log_gamma():
    return [math.log1p(-2.0 ** (-5.0 - h)) for h in range(HB)]


def _rope_tables(pos):
    half = DK // 2
    theta = 1.0 / (ROPE_BASE ** jnp.linspace(0.0, 1.0, half, dtype=F32))
    ang = pos[:, None] * theta[None, :]
    cos, sin = jnp.cos(ang), jnp.sin(ang)
    return jnp.concatenate([cos, cos], axis=1), jnp.concatenate([-sin, sin], axis=1)


def _decay_tables(chunk):
    lg = jnp.asarray(_log_gamma(), F32)
    idx = jnp.arange(chunk, dtype=F32)
    diff = idx[:, None] - idx[None, :]
    dmat = jnp.exp(jnp.maximum(diff, 0.0)[None] * lg[:, None, None]) * (diff >= 0)[None]
    cross = jnp.exp((idx[None, :] + 1.0) * lg[:, None])[:, :, None]
    kdec = jnp.exp((chunk - 1.0 - idx)[None, :] * lg[:, None])[:, :, None]
    sdec = jnp.exp(chunk * lg)
    return dmat.astype(F32), cross, kdec, sdec


def _rotary(x, cos2, sin2):
    return x * cos2 + pltpu.roll(x, DK // 2, axis=1) * sin2


def _ln_rows(y, eps):
    m = jnp.mean(y, axis=-1, keepdims=True)
    yc = y - m
    var = jnp.mean(yc * yc, axis=-1, keepdims=True)
    return yc * lax.rsqrt(var + eps)


def _ret_prompt_kernel(qk_ref, vb_ref, cos_ref, sin_ref, dmat_ref, cross_ref, kdec_ref, sdec_ref,
                       o_ref, sfin_ref, s_scr):
    j = pl.program_id(1)

    @pl.when(j == 0)
    def _():
        s_scr[...] = jnp.zeros_like(s_scr)

    cos2, sin2 = cos_ref[...], sin_ref[...]
    for h in range(HB):
        q = _rotary(qk_ref[:, h * DK:(h + 1) * DK], cos2, sin2)
        k = _rotary(qk_ref[:, HB * DK + h * DK:HB * DK + (h + 1) * DK], cos2, sin2) * (DK ** -0.5)
        v = vb_ref[:, h * DV:(h + 1) * DV].astype(BF16)
        qb = q.astype(BF16)
        scores = lax.dot_general(qb, k.astype(BF16), (((1,), (1,)), ((), ())),
                                 preferred_element_type=F32) * dmat_ref[h]
        s = s_scr[h]
        y = (jnp.dot(scores.astype(BF16), v, preferred_element_type=F32)
             + jnp.dot(qb, s.astype(BF16), preferred_element_type=F32) * cross_ref[h])
        kd = (k * kdec_ref[h]).astype(BF16)
        s_scr[h] = s * sdec_ref[h] + lax.dot_general(kd, v, (((0,), (0,)), ((), ())),
                                                     preferred_element_type=F32)
        o_ref[:, h * DV:(h + 1) * DV] = _ln_rows(y, LN_EPS)

    @pl.when(j == pl.num_programs(1) - 1)
    def _():
        sfin_ref[...] = s_scr[...]


def _ret_prompt(qk, vb, nseq, t):
    chunks = t // RET_CHUNK
    cos2, sin2 = _rope_tables(jnp.arange(t, dtype=F32) + 0)
    dmat, cross, kdec, sdec = _decay_tables(RET_CHUNK)
    row_spec = pl.BlockSpec((RET_CHUNK, D), lambda n, j: (n * chunks + j, 0))
    tab_spec = pl.BlockSpec((RET_CHUNK, DK), lambda n, j: (j, 0))
    return pl.pallas_call(
        _ret_prompt_kernel,
        grid=(nseq, chunks),
        in_specs=[row_spec, row_spec, tab_spec, tab_spec,
                  _const_spec((HB, RET_CHUNK, RET_CHUNK)), _const_spec((HB, RET_CHUNK, 1)),
                  _const_spec((HB, RET_CHUNK, 1)),
                  pl.BlockSpec(memory_space=pltpu.SMEM)],
        out_specs=[row_spec, pl.BlockSpec((None, HB, DK, DV), lambda n, j: (n, 0, 0, 0))],
        out_shape=[jax.ShapeDtypeStruct(qk.shape, F32),
                   jax.ShapeDtypeStruct((nseq, HB, DK, DV), F32)],
        scratch_shapes=[pltpu.VMEM((HB, DK, DV), F32)],
        compiler_params=pltpu.CompilerParams(dimension_semantics=("arbitrary", "arbitrary"),
                                             vmem_limit_bytes=VMEM_LIMIT),
        name="ret_prompt",
    )(qk, vb, cos2, sin2, dmat, cross, kdec, sdec)


def _ret_sample_kernel(qk_ref, vb_ref, cos_ref, sin_ref, s_ref, dmat_ref, cross_ref, kdec_ref, sdec_ref,
                       o_ref, snew_ref, *, nb):
    cos2, sin2 = cos_ref[...], sin_ref[...]
    for n in range(nb):
        for h in range(HB):
            q = _rotary(qk_ref[:, n, h * DK:(h + 1) * DK], cos2, sin2)
            k = _rotary(qk_ref[:, n, HB * DK + h * DK:HB * DK + (h + 1) * DK], cos2, sin2) * (DK ** -0.5)
            v = vb_ref[:, n, h * DV:(h + 1) * DV].astype(BF16)
            qb = q.astype(BF16)
            scores = lax.dot_general(qb, k.astype(BF16), (((1,), (1,)), ((), ())),
                                     preferred_element_type=F32) * dmat_ref[h]
            s = s_ref[n, h]
            y = (jnp.dot(scores.astype(BF16), v, preferred_element_type=F32)
                 + jnp.dot(qb, s.astype(BF16), preferred_element_type=F32) * cross_ref[h])
            kd = (k * kdec_ref[h]).astype(BF16)
            snew_ref[n, h] = s * sdec_ref[h] + lax.dot_general(kd, v, (((0,), (0,)), ((), ())),
                                                               preferred_element_type=F32)
            o_ref[:, n, h * DV:(h + 1) * DV] = _ln_rows(y, LN_EPS)


def _ret_sample(qk3, vb3, s0, pos0, *, nb):
    t, nseq, _ = qk3.shape
    cos2, sin2 = _rope_tables(jnp.arange(t, dtype=F32) + pos0)
    dmat, cross, kdec, sdec = _decay_tables(t)
    row_spec = pl.BlockSpec((t, nb, D), lambda i: (0, i, 0))
    st_spec = pl.BlockSpec((nb, HB, DK, DV), lambda i: (i, 0, 0, 0))
    return pl.pallas_call(
        functools.partial(_ret_sample_kernel, nb=nb),
        grid=(nseq // nb,),
        in_specs=[row_spec, row_spec, _const_spec((t, DK)), _const_spec((t, DK)), st_spec,
                  _const_spec((HB, t, t)), _const_spec((HB, t, 1)), _const_spec((HB, t, 1)),
                  pl.BlockSpec(memory_space=pltpu.SMEM)],
        out_specs=[row_spec, st_spec],
        out_shape=[jax.ShapeDtypeStruct(qk3.shape, F32), jax.ShapeDtypeStruct(s0.shape, F32)],
        compiler_params=pltpu.CompilerParams(dimension_semantics=("arbitrary",),
                                             vmem_limit_bytes=VMEM_LIMIT),
        name="ret_sample",
    )(qk3, vb3, cos2, sin2, s0, dmat, cross, kdec, sdec)


def _unpair_tokens(ya_ref, ya_scr, pair_rows):
    half = ya_ref.shape[0]
    groups = half // pair_rows
    low = lax.broadcasted_iota(jnp.int32, (half, LANES), 1) < KA
    for m in range(D // LANES):
        ha = ya_ref[:, 2 * m, :]
        hb = ya_ref[:, 2 * m + 1, :]
        lo = jnp.where(low, ha, pltpu.roll(hb, KA, axis=1))
        hi = jnp.where(low, pltpu.roll(ha, KA, axis=1), hb)
        ya_scr[:, 0, :, m * LANES:(m + 1) * LANES] = lo.reshape(groups, pair_rows, LANES)
        ya_scr[:, 1, :, m * LANES:(m + 1) * LANES] = hi.reshape(groups, pair_rows, LANES)
    return ya_scr[...].reshape(2 * half, D)


def _post_kernel(ga_ref, ya_ref, gb_ref, ob_ref, x_ref, gate_ref, wout_ref, lng_ref, lnb_ref, y_ref,
                 *scratch, gs, alpha, pair_rows):
    ya = _unpair_tokens(ya_ref, scratch[0], pair_rows) if pair_rows else ya_ref[...]
    merged = ga_ref[...] * ya + gb_ref[...] * ob_ref[...]
    sub = jnp.dot(merged.astype(BF16), wout_ref[...], preferred_element_type=F32)
    x = x_ref[...]
    gate = gate_ref[...]
    if gs == 1:
        z = alpha * x + gate * sub
    else:
        tt = x.shape[0]
        z = (alpha * x.reshape(tt // gs, gs, D) + gate[None] * sub.reshape(tt // gs, gs, D)).reshape(tt, D)
    y_ref[...] = _ln_rows(z, LN_EPS) * lng_ref[...] + lnb_ref[...]


def _post(ga, ya, gb, ob, x2, gate, wout, ln_gain, ln_bias, *, nseq_grid, tiles, tt, gs, alpha, pair_rows):
    row_spec = pl.BlockSpec((tt, D), lambda n, j: (n * tiles + j, 0))
    if pair_rows:
        ya_spec = pl.BlockSpec((tt // 2, HA, LANES), lambda n, j: (n * tiles + j, 0, 0))
        scratch = [pltpu.VMEM((tt // 2 // pair_rows, 2, pair_rows, D), F32)]
    else:
        ya_spec, scratch = row_spec, []
    return pl.pallas_call(
        functools.partial(_post_kernel, gs=gs, alpha=alpha, pair_rows=pair_rows),
        grid=(nseq_grid, tiles),
        in_specs=[row_spec, ya_spec, row_spec, row_spec, row_spec,
                  pl.BlockSpec((None, gs, D), lambda n, j: (n, 0, 0)),
                  _const_spec((D, D)), _const_spec((1, D)), _const_spec((1, D))],
        out_specs=row_spec,
        out_shape=jax.ShapeDtypeStruct(x2.shape, F32),
        scratch_shapes=scratch,
        compiler_params=pltpu.CompilerParams(dimension_semantics=("arbitrary", "arbitrary"),
                                             vmem_limit_bytes=VMEM_LIMIT),
        name="post",
    )(ga, ya, gb, ob, x2, gate, wout, ln_gain, ln_bias)


def _head_param_lanes(p, nseq):
    return jnp.tile(p.reshape(HA, KA).T, (1, nseq))


def _layer(x2, ada, first_prev, s_wkv, s_ret, pos0, weights, *, nseq, t, seq_major, tt, tc, depth):
    (wsh, wrest, pre_params, gn_gain, gn_bias, r_k, wout, ln_gain, ln_bias) = weights
    if seq_major:
        nseq_grid, gs, tiles = nseq, 1, t // tt
    else:
        nseq_grid, gs, tiles = 1, nseq, (t * nseq) // tt
    shift_c = ada[:, :D].reshape(nseq_grid, gs, D)
    scale_c = ada[:, D:2 * D].reshape(nseq_grid, gs, D)
    gate_c = ada[:, 2 * D:].reshape(nseq_grid, gs, D)
    cfg = dict(nseq_grid=nseq_grid, tiles=tiles, tt=tt, gs=gs)

    *vecs, new_shift = _pre_a(x2, scale_c, shift_c, first_prev.reshape(nseq_grid, gs, SHIFT_W), wsh, pre_params,
                              pair_out=not seq_major, **cfg)
    ga, gb, qk, vb = _pre_b(x2, scale_c, shift_c, wrest, **cfg)

    if seq_major:
        row = lambda p: p.reshape(1, D)
        ya, new_wkv = _wkv_chunked(vecs, row(gn_gain), row(gn_bias), row(r_k), nseq=nseq, t=t, ct=tc)
        pair_rows = None
    else:
        s0 = s_wkv.transpose(3, 2, 0, 1).reshape(KA, KA, nseq * HA)
        ya, s_fin = _wkv(vecs, s0, _head_param_lanes(gn_gain, nseq),
                         _head_param_lanes(gn_bias, nseq), _head_param_lanes(r_k.reshape(-1), nseq),
                         nseq=nseq, t=t, tc=tc, seq_major=False)
        pair_rows = (tc // 2) * gs
        new_wkv = s_fin.reshape(KA, KA, nseq, HA).transpose(2, 3, 1, 0)

    if seq_major:
        ob, new_ret = _ret_prompt(qk, vb, nseq, t)
    else:
        ob3, new_ret = _ret_sample(qk.reshape(t, nseq, D), vb.reshape(t, nseq, D), s_ret, pos0, nb=8)
        ob = ob3.reshape(t * nseq, D)

    alpha = (2.0 * depth) ** 0.25
    y = _post(ga, ya, gb, ob, x2, gate_c, wout, ln_gain, ln_bias, alpha=alpha, pair_rows=pair_rows, **cfg)
    return y, new_shift.reshape(nseq, SHIFT_W), new_wkv, new_ret


def kernel(x_prompt, x_sample, c_prompt, c_sample, state_shift, state_wkv, state_ret, w_ada, b_ada, w_in,
           mu_shift, w0, w_decay_up, a0, w_icl_up, k_k, k_a, r_k, gn_a_gain, gn_a_bias, w_out, ln_gain, ln_bias):
    depth = w_in.shape[0]
    nb, t_p, _ = x_prompt.shape
    ns, t_s, _ = x_sample.shape
    past_len = 16384
    h_p = x_prompt.reshape(nb * t_p, D)
    h_s = x_sample.transpose(1, 0, 2).reshape(t_s * ns, D)
    outs = [[] for _ in range(6)]
    zero_lora = jnp.zeros((LORA, D), F32)
    for l in range(depth):
        row = lambda p: p[l].reshape(1, -1)
        wsh = w_in[l][:, :SHIFT_W].astype(BF16)
        wrest = w_in[l][:, SHIFT_W:].astype(BF16)
        wd_pad = jnp.concatenate([w_decay_up[l], zero_lora], axis=0).astype(BF16)
        wa_pad = jnp.concatenate([zero_lora, w_icl_up[l]], axis=0).astype(BF16)
        pre_params = (row(mu_shift), row(w0), row(a0), row(k_k), row(k_a), wd_pad, wa_pad)
        weights = (wsh, wrest, pre_params, gn_a_gain[l], gn_a_bias[l], r_k[l], w_out[l].astype(BF16),
                   row(ln_gain), row(ln_bias))
        ada = _ada(jnp.concatenate([c_prompt, c_sample], axis=0), w_ada[l], b_ada[l])
        h_p, s1, s2, s3 = _layer(
            h_p, ada[:nb], jnp.zeros((nb, SHIFT_W), F32), None, None, 0,
            weights, nseq=nb, t=t_p, seq_major=True, tt=256, tc=128, depth=depth)
        h_s, t1, t2, t3 = _layer(
            h_s, ada[nb:], state_shift[l], state_wkv[l], state_ret[l], past_len,
            weights, nseq=ns, t=t_s, seq_major=False, tt=t_s * ns, tc=t_s, depth=depth)
        for lst, val in zip(outs, (s1, s2, s3, t1, t2, t3)):
            lst.append(val)
    y_p = h_p.reshape(nb, t_p, D)
    y_s = h_s.reshape(t_s, ns, D).transpose(1, 0, 2)
    return (y_p, y_s) + tuple(jnp.stack(o) for o in outs)
```

```python
import functools
import math

import numpy as np
import jax
import jax.numpy as jnp
from jax import lax
from jax.experimental import pallas as pl
from jax.experimental.pallas import tpu as pltpu

F32 = jnp.float32
BF16 = jnp.bfloat16

D = 1024
HA = 16
KA = 64
LORA = 64
GN_EPS = 64e-5
HB = 4
DK = 128
DV = 256
RET_CHUNK = 128
ROPE_BASE = 10000.0
LN_EPS = 1e-5
SHIFT_W = 3 * D + 2 * LORA
REST_W = 6 * D
LANES = 128
SUBLANES = 8
VMEM_LIMIT = 56 * 1024 * 1024


def _sigmoid(x):
    return 1.0 / (1.0 + jnp.exp(-x))


def _bdot(a, b):
    return jnp.dot(a.astype(BF16), b.astype(BF16), preferred_element_type=F32)


def _bdot_nt(a, b):
    return lax.dot_general(a.astype(BF16), b.astype(BF16), (((1,), (1,)), ((), ())), preferred_element_type=F32)


def _bdot_tn(a, b):
    return lax.dot_general(a.astype(BF16), b.astype(BF16), (((0,), (0,)), ((), ())), preferred_element_type=F32)


def _split_bf16(x, parts):
    out = []
    for _ in range(parts):
        term = x.astype(BF16)
        out.append(term)
        x = x - term.astype(F32)
    return out


def _const_spec(shape):
    nd = len(shape)
    return pl.BlockSpec(shape, lambda *_: (0,) * nd, pipeline_mode=pl.Buffered(1))


def _ada_kernel(c_ref, w_ref, b_ref, o_ref):
    c = c_ref[...]
    s = c * _sigmoid(c)
    o_ref[...] = jnp.dot(s, w_ref[...], preferred_element_type=F32,
                         precision=lax.Precision.HIGHEST) + b_ref[...]


def _ada(c_all, w_ada, b_ada):
    n = c_all.shape[0]
    return pl.pallas_call(
        _ada_kernel,
        grid=(3,),
        in_specs=[pl.BlockSpec((n, D), lambda j: (0, 0)),
                  pl.BlockSpec((D, D), lambda j: (0, j)),
                  pl.BlockSpec((1, D), lambda j: (0, j))],
        out_specs=pl.BlockSpec((n, D), lambda j: (0, j)),
        out_shape=jax.ShapeDtypeStruct((n, 3 * D), F32),
        compiler_params=pltpu.CompilerParams(dimension_semantics=("arbitrary",)),
        name="ada",
    )(c_all, w_ada, b_ada.reshape(1, 3 * D))


def _modulate(x, scale, shift, gs):
    if gs == 1:
        return x * (1.0 + scale) + shift
    tt = x.shape[0]
    x3 = x.reshape(tt // gs, gs, D)
    return (x3 * (1.0 + scale)[None] + shift[None]).reshape(tt, D)


def _put_head_pairs(out_ref, scr, x, y):
    tt = x.shape[0]
    low = lax.broadcasted_iota(jnp.int32, (tt, LANES), 1) < KA
    for m in range(D // LANES):
        xc = x[:, m * LANES:(m + 1) * LANES]
        yc = y[:, m * LANES:(m + 1) * LANES]
        even = jnp.where(low, xc, pltpu.roll(yc, KA, axis=1))
        odd = jnp.where(low, pltpu.roll(xc, KA, axis=1), yc)
        for r in range(tt // SUBLANES):
            rows = slice(r * SUBLANES, (r + 1) * SUBLANES)
            scr[pl.ds((r * HA + 2 * m) * SUBLANES, SUBLANES), :] = even[rows]
            scr[pl.ds((r * HA + 2 * m + 1) * SUBLANES, SUBLANES), :] = odd[rows]

    def row_block(r, carry):
        for s in range(SUBLANES):
            out_ref[r * SUBLANES + s] = scr[pl.ds(r * (HA * SUBLANES) + s, HA, stride=SUBLANES), :]
        return carry

    lax.fori_loop(0, tt // SUBLANES, row_block, 0)


def _pre_a_kernel(x_ref, scale_ref, shift_ref, prev_ref, wsh_ref, mu_ref, w0_ref, a0_ref,
                  kk_ref, ka_ref, wd_ref, wa_ref, *rest, gs, tt, pad, pair_out):
    if pair_out:
        rk_out, wk_out, va_out, ns_out, p_scr, pair_scr = rest
    else:
        r_out, lw_out, kmod_out, v_out, kkraw_out, a_out, ns_out, p_scr = rest
    j = pl.program_id(1)
    nj = pl.num_programs(1)
    lo = pad - gs

    @pl.when(j == 0)
    def _():
        p_scr[lo:pad, :] = prev_ref[...]

    u = _modulate(x_ref[...], scale_ref[...], shift_ref[...], gs).astype(BF16)

    def mixed(c0, c1):
        p_scr[pad:pad + tt, c0:c1] = jnp.dot(u, wsh_ref[:, c0:c1], preferred_element_type=F32)
        p = p_scr[pad:pad + tt, c0:c1]
        prev = p_scr[lo:lo + tt, c0:c1]
        return p + (prev - p) * mu_ref[:, c0:c1]

    lora_in = mixed(3 * D, SHIFT_W)
    lora_w = _bdot(jnp.tanh(lora_in), wd_ref[...])
    lora_a = _bdot(lora_in, wa_ref[...])
    zw = -(w0_ref[...] + lora_w)
    softplus = jnp.maximum(zw, 0.0) + jnp.log(1.0 + jnp.exp(-jnp.abs(zw)))
    w_log = -softplus - 0.5
    log_decay = -jnp.exp(w_log)
    a = _sigmoid(a0_ref[...] + lora_a)
    k = mixed(D, 2 * D)
    if pair_out:
        _put_head_pairs(rk_out, pair_scr.at[0], mixed(0, D), k * (1.0 + (a - 1.0) * ka_ref[...]))
        _put_head_pairs(wk_out, pair_scr.at[1], jnp.exp(log_decay), k * kk_ref[...])
        _put_head_pairs(va_out, pair_scr.at[2], mixed(2 * D, 3 * D), a)
    else:
        lw_out[...] = log_decay
        a_out[...] = a
        kkraw_out[...] = k * kk_ref[...]
        kmod_out[...] = k * (1.0 + (a - 1.0) * ka_ref[...])
        r_out[...] = mixed(0, D)
        v_out[...] = mixed(2 * D, 3 * D)

    @pl.when(j == nj - 1)
    def _():
        ns_out[...] = p_scr[pad + tt - gs:pad + tt, :]

    p_scr[lo:pad, :] = p_scr[pad + tt - gs:pad + tt, :]


def _pre_a(x2, scale, shift, first_prev, wsh, params, *, nseq_grid, tiles, tt, gs, pair_out):
    rows = x2.shape[0]
    pad = max(SUBLANES, gs)
    mu, w0, a0, k_k, k_a, wd_pad, wa_pad = params
    row_spec = pl.BlockSpec((tt, D), lambda n, j: (n * tiles + j, 0))
    seq_spec = pl.BlockSpec((None, gs, D), lambda n, j: (n, 0, 0))
    scratch = [pltpu.VMEM((pad + tt, SHIFT_W), F32)]
    if pair_out:
        vec_specs = [pl.BlockSpec((tt, HA, LANES), lambda n, j: (n * tiles + j, 0, 0))] * 3
        vec_sds = [jax.ShapeDtypeStruct((rows, HA, LANES), F32)] * 3
        scratch.append(pltpu.VMEM((3, tt * HA, LANES), F32))
    else:
        vec_specs = [row_spec] * 6
        vec_sds = [jax.ShapeDtypeStruct((rows, D), F32)] * 6
    kern = functools.partial(_pre_a_kernel, gs=gs, tt=tt, pad=pad, pair_out=pair_out)
    return pl.pallas_call(
        kern,
        grid=(nseq_grid, tiles),
        in_specs=[row_spec, seq_spec, seq_spec,
                  pl.BlockSpec((None, gs, SHIFT_W), lambda n, j: (n, 0, 0)),
                  _const_spec((D, SHIFT_W)), _const_spec((1, SHIFT_W)),
                  _const_spec((1, D)), _const_spec((1, D)), _const_spec((1, D)), _const_spec((1, D)),
                  _const_spec((LANES, D)), _const_spec((LANES, D))],
        out_specs=vec_specs + [pl.BlockSpec((None, gs, SHIFT_W), lambda n, j: (n, 0, 0))],
        out_shape=vec_sds + [jax.ShapeDtypeStruct((nseq_grid, gs, SHIFT_W), F32)],
        scratch_shapes=scratch,
        compiler_params=pltpu.CompilerParams(dimension_semantics=("arbitrary", "arbitrary"),
                                             vmem_limit_bytes=VMEM_LIMIT),
        name="pre_a",
    )(x2, scale, shift, first_prev, wsh, mu, w0, a0, k_k, k_a, wd_pad, wa_pad)


def _pre_b_kernel(x_ref, scale_ref, shift_ref, w_ref, ga_out, gb_out, qk_out, vb_out, *, gs):
    u = _modulate(x_ref[...], scale_ref[...], shift_ref[...], gs).astype(BF16)

    def proj(c0, c1):
        return jnp.dot(u, w_ref[:, c0:c1], preferred_element_type=F32)

    z_a = proj(0, D)
    g_a = proj(4 * D, 5 * D)
    ga_out[...] = _sigmoid(g_a) * (z_a * _sigmoid(z_a))
    z_b = proj(3 * D, 4 * D)
    g_b = proj(5 * D, 6 * D)
    gb_out[...] = _sigmoid(g_b) * (z_b * _sigmoid(z_b))
    qk_out[...] = proj(D, 2 * D)
    vb_out[...] = proj(2 * D, 3 * D)


def _pre_b(x2, scale, shift, wrest, *, nseq_grid, tiles, tt, gs):
    rows = x2.shape[0]
    row_spec = pl.BlockSpec((tt, D), lambda n, j: (n * tiles + j, 0))
    seq_spec = pl.BlockSpec((None, gs, D), lambda n, j: (n, 0, 0))
    out_sds = jax.ShapeDtypeStruct((rows, D), F32)
    return pl.pallas_call(
        functools.partial(_pre_b_kernel, gs=gs),
        grid=(nseq_grid, tiles),
        in_specs=[row_spec, seq_spec, seq_spec, _const_spec((D, REST_W))],
        out_specs=[row_spec] * 4,
        out_shape=[out_sds] * 4,
        compiler_params=pltpu.CompilerParams(dimension_semantics=("arbitrary", "arbitrary"),
                                             vmem_limit_bytes=VMEM_LIMIT),
        name="pre_b",
    )(x2, scale, shift, wrest)


V_KKN, V_B, V_W, V_K, V_R, V_V = range(6)


def _wkv_kernel(rk_ref, wk_ref, va_ref, s0_ref, gain_ref, bias_ref, rkp_ref,
                o_ref, sfin_ref, s_scr, vec_scr, row_scr, y_scr, cur_scr, *, tc, seq_major):
    j = pl.program_id(1)

    @pl.when(j == 0)
    def _():
        s_scr[...] = s0_ref[...]

    def load_pair(ref, t):
        x = ref[:, t] if seq_major else ref[t]
        xt = x.reshape(LANES, LANES).T
        return xt[:KA], xt[KA:]

    def prep(t, carry):
        dec, kkraw = load_pair(wk_ref, t)
        ss = jnp.sum(kkraw * kkraw, axis=0, keepdims=True)
        kkn = kkraw * (1.0 / jnp.maximum(jnp.sqrt(ss), 1e-12))
        v, a = load_pair(va_ref, t)
        r, kmod = load_pair(rk_ref, t)
        vec_scr[t, V_KKN] = kkn
        vec_scr[t, V_B] = kkn * a
        vec_scr[t, V_W] = dec
        vec_scr[t, V_K] = kmod
        vec_scr[t, V_R] = r
        vec_scr[t, V_V] = v
        rk = jnp.sum(r * kmod * rkp_ref[...], axis=0, keepdims=True)
        row_scr[t] = jnp.broadcast_to(rk, (SUBLANES, LANES))
        return carry

    def state_pass(t, carry):
        cur_scr[...] = vec_scr[t]
        skk = jnp.zeros((KA, LANES), F32)
        for k in range(KA):
            skk = skk + s_scr[k] * cur_scr[V_KKN, k:k + 1, :]
        vv = cur_scr[V_V]
        y = jnp.zeros((KA, LANES), F32)
        for k in range(KA):
            s_new = (s_scr[k] * cur_scr[V_W, k:k + 1, :] - skk * cur_scr[V_B, k:k + 1, :]
                     + vv * cur_scr[V_K, k:k + 1, :])
            s_scr[k] = s_new
            y = y + s_new * cur_scr[V_R, k:k + 1, :]
        y_scr[t] = y
        return carry

    def out_token(t):
        y = y_scr[t]
        m = jnp.mean(y, axis=0, keepdims=True)
        yc = y - m
        var = jnp.mean(yc * yc, axis=0, keepdims=True)
        yn = yc * lax.rsqrt(var + GN_EPS) * gain_ref[...] + bias_ref[...]
        return yn + row_scr[t, 0:1, :] * vec_scr[t, V_V]

    def finish(i, carry):
        o2 = jnp.concatenate([out_token(i), out_token(i + tc // 2)], axis=0)
        o = o2.T.reshape(LANES // HA, HA, LANES)
        if seq_major:
            o_ref[:, i] = o
        else:
            o_ref[i] = o
        return carry

    lax.fori_loop(0, tc, prep, 0, unroll=2)
    lax.fori_loop(0, tc, state_pass, 0)
    lax.fori_loop(0, tc // 2, finish, 0, unroll=2)

    @pl.when(j == pl.num_programs(1) - 1)
    def _():
        sfin_ref[...] = s_scr[...]


def _wkv(pairs, s0, gain_l, bias_l, rk_l, *, nseq, t, tc, seq_major):
    nsl = LANES // HA
    groups = nseq // nsl
    if seq_major:
        shape4, oshape4 = (nseq, t, HA, LANES), (nseq, t // 2, HA, LANES)
        vec_spec = pl.BlockSpec((nsl, tc, HA, LANES), lambda g, j: (g, j, 0, 0))
        out_spec = pl.BlockSpec((nsl, tc // 2, HA, LANES), lambda g, j: (g, j, 0, 0))
    else:
        shape4, oshape4 = (t, nseq, HA, LANES), (t // 2, nseq, HA, LANES)
        vec_spec = pl.BlockSpec((tc, nsl, HA, LANES), lambda g, j: (j, g, 0, 0))
        out_spec = pl.BlockSpec((tc // 2, nsl, HA, LANES), lambda g, j: (j, g, 0, 0))
    st_spec = pl.BlockSpec((KA, KA, LANES), lambda g, j: (0, 0, g))
    par_spec = pl.BlockSpec((KA, LANES), lambda g, j: (0, g))
    o4, s_fin = pl.pallas_call(
        functools.partial(_wkv_kernel, tc=tc, seq_major=seq_major),
        grid=(groups, t // tc),
        in_specs=[vec_spec] * 3 + [st_spec, par_spec, par_spec, par_spec],
        out_specs=[out_spec, st_spec],
        out_shape=[jax.ShapeDtypeStruct(oshape4, F32),
                   jax.ShapeDtypeStruct((KA, KA, nseq * HA), F32)],
        scratch_shapes=[pltpu.VMEM((KA, KA, LANES), F32), pltpu.VMEM((tc, 6, KA, LANES), F32),
                        pltpu.VMEM((tc, SUBLANES, LANES), F32), pltpu.VMEM((tc, KA, LANES), F32),
                        pltpu.VMEM((6, KA, LANES), F32)],
        compiler_params=pltpu.CompilerParams(dimension_semantics=("arbitrary", "arbitrary"),
                                             vmem_limit_bytes=VMEM_LIMIT),
        name="wkv",
    )(*[z.reshape(shape4) for z in pairs], s0, gain_l, bias_l, rk_l)
    return o4.reshape(nseq * t // 2, HA, LANES), s_fin


CH = 64
NPAIR = D // LANES


def _wkv_chunk_kernel(r_ref, lw_ref, k_ref, v_ref, kk_ref, a_ref, gain_ref, bias_ref, rk_ref, tri_ref, ones_ref,
                      o_ref, tfin_ref, t_scr, *, ct):
    j = pl.program_id(1)

    @pl.when(j == 0)
    def _():
        t_scr[...] = jnp.zeros_like(t_scr)

    lane = lax.broadcasted_iota(jnp.int32, (CH, LANES), 1)
    row = lax.broadcasted_iota(jnp.int32, (CH, LANES), 0)
    low = lane < KA
    col = jnp.bitwise_and(lane, KA - 1)
    strict = row > col
    incl = row >= col
    eye = (row == col).astype(F32)
    low2 = lax.broadcasted_iota(jnp.int32, (LANES, LANES), 1) < KA
    diag_blocks = (lax.broadcasted_iota(jnp.int32, (LANES, LANES), 0) < KA) == low2
    tri = tri_ref[...]
    ones_bd = ones_ref[...]

    def seg_sum(x, parts=2):
        return sum(jnp.dot(term, ones_bd, preferred_element_type=F32) for term in _split_bf16(x, parts))

    def bd(z):
        return jnp.concatenate([jnp.where(low, z, 0.0), jnp.where(low, 0.0, z)], axis=0)

    def abd(z):
        return jnp.concatenate([jnp.where(low, 0.0, z), jnp.where(low, z, 0.0)], axis=0)

    def seg_sum_pairs(xs, parts):
        tot = seg_sum(jnp.concatenate(xs, axis=0), parts)
        return [tot[m * CH:(m + 1) * CH] for m in range(NPAIR)]

    P = range(NPAIR)
    ln = [slice(m * LANES, (m + 1) * LANES) for m in P]

    def chunk_free(c0, ctx):
        rows = slice(c0, c0 + CH)
        lw_all = lw_ref[rows, :]
        cs_all = sum(jnp.dot(tri, term, preferred_element_type=F32) for term in _split_bf16(lw_all, 2))
        yield
        lw = [lw_all[:, ln[m]] for m in P]
        cs = [cs_all[:, ln[m]] for m in P]
        kkraw = [kk_ref[rows, ln[m]] for m in P]
        ss = seg_sum_pairs([kkraw[m] * kkraw[m] for m in P], 2)
        yield
        kkn = [kkraw[m] / jnp.maximum(jnp.sqrt(ss[m]), 1e-12) for m in P]
        b = [kkn[m] * a_ref[rows, ln[m]] for m in P]
        r = [r_ref[rows, ln[m]] for m in P]
        k = [k_ref[rows, ln[m]] for m in P]
        v = [v_ref[rows, ln[m]] for m in P]
        cs_last = [cs[m][CH - 1:CH, :] for m in P]
        e_neg = [jnp.exp(-cs[m]) for m in P]
        a_t = [-kkn[m] * jnp.exp(cs[m] - lw[m]) for m in P]
        r_t = [r[m] * jnp.exp(cs[m]) for m in P]
        b_t = [b[m] * e_neg[m] for m in P]
        k_t = [k[m] * e_neg[m] for m in P]
        lhs = [jnp.concatenate([a_t[m], r_t[m]], axis=0) for m in P]
        out0 = [_bdot_nt(jnp.where(low2, lhs[m], 0.0), jnp.concatenate([b_t[m], k_t[m]], axis=0)) for m in P]
        yield
        out1 = [_bdot_nt(jnp.where(low2, 0.0, lhs[m]), jnp.concatenate([k_t[m], b_t[m]], axis=0)) for m in P]
        yield
        l_cat = [jnp.where(strict, jnp.where(low, out0[m][:CH], out1[m][:CH]), 0.0) for m in P]
        ak_cat = [jnp.where(strict, jnp.where(low, out1[m][:CH], out0[m][:CH]), 0.0) for m in P]
        rb_cat = [jnp.where(incl, jnp.where(low, out0[m][CH:], out1[m][CH:]), 0.0) for m in P]
        rk_cat = [jnp.where(incl, jnp.where(low, out1[m][CH:], out0[m][CH:]), 0.0) for m in P]
        rkv = seg_sum_pairs([r[m] * k[m] * rk_ref[:, ln[m]] for m in P], 1)
        x = [eye + l_cat[m] for m in P]
        lp = [_bdot(l_cat[m], bd(l_cat[m])) for m in P]
        yield
        for _ in range(4):
            both = [_bdot(jnp.concatenate([lp[m], x[m]], axis=0), bd(lp[m])) for m in P]
            yield
            x = [x[m] + both[m][CH:] for m in P]
            lp = [both[m][:CH] for m in P]
        x = [x[m] + _bdot(x[m], bd(lp[m])) for m in P]
        e_end = [jnp.exp(cs_last[m] - cs[m]) for m in P]
        ctx.update(
            rows=rows, x=x, v=v, rkv=rkv,
            g_lhs=[jnp.concatenate([a_t[m], ak_cat[m]], axis=1) for m in P],
            y_lhs=[jnp.concatenate([r_t[m], rb_cat[m], rk_cat[m]], axis=1) for m in P],
            upd_lhs=[jnp.concatenate([b[m] * e_end[m], k[m] * e_end[m]], axis=0) for m in P],
            p_end=[jnp.exp(cs_last[m]) for m in P])
        yield

    def chunk_state(ctx):
        x, v = ctx["x"], ctx["v"]
        t0 = [t_scr[m] for m in P]
        v_abd = [abd(v[m]) for m in P]
        g = [_bdot(ctx["g_lhs"][m], jnp.concatenate([t0[m], v_abd[m]], axis=0)) for m in P]
        yield
        u = [_bdot(x[m], bd(g[m])) for m in P]
        yield
        upd = [_bdot_tn(ctx["upd_lhs"][m], jnp.concatenate([u[m], v[m]], axis=0)) for m in P]
        yield
        for m in P:
            p_col = jnp.broadcast_to(ctx["p_end"][m], (LANES, LANES)).T
            t_scr[m] = t0[m] * p_col + jnp.where(diag_blocks, upd[m], 0.0)
        y = [_bdot(ctx["y_lhs"][m], jnp.concatenate([t0[m], bd(u[m]), v_abd[m]], axis=0)) for m in P]
        yield
        mean = seg_sum_pairs(y, 1)
        yield
        yc = [y[m] - mean[m] * (1.0 / KA) for m in P]
        var = seg_sum_pairs([yc[m] * yc[m] for m in P], 1)
        yield
        for m in P:
            yn = yc[m] * lax.rsqrt(var[m] * (1.0 / KA) + GN_EPS) * gain_ref[:, ln[m]] + bias_ref[:, ln[m]]
            o_ref[ctx["rows"], ln[m]] = yn + ctx["rkv"][m] * v[m]
        yield

    def interleave(*gens):
        live = list(gens)
        while live:
            for gen in list(live):
                try:
                    next(gen)
                except StopIteration:
                    live.remove(gen)

    ctxs = [dict() for _ in range(ct // CH)]
    interleave(chunk_free(0, ctxs[0]))
    for c in range(ct // CH):
        nxt = [chunk_free((c + 1) * CH, ctxs[c + 1])] if c + 1 < ct // CH else []
        interleave(chunk_state(ctxs[c]), *nxt)

    @pl.when(j == pl.num_programs(1) - 1)
    def _():
        tfin_ref[...] = t_scr[...]


def _wkv_chunked(vecs, gain, bias, rk, *, nseq, t, ct):
    tiles = t // ct
    row_spec = pl.BlockSpec((ct, D), lambda n, j: (n * tiles + j, 0))
    tri = jnp.asarray(np.tril(np.ones((CH, CH), np.float32)), BF16)
    head_of_lane = np.arange(LANES) // KA
    ones_bd = jnp.asarray((head_of_lane[:, None] == head_of_lane[None, :]).astype(np.float32), BF16)
    ya, t_fin = pl.pallas_call(
        functools.partial(_wkv_chunk_kernel, ct=ct),
        grid=(nseq, tiles),
        in_specs=[row_spec] * 6 + [_const_spec((1, D))] * 3 + [_const_spec((CH, CH)), _const_spec((LANES, LANES))],
        out_specs=[row_spec, pl.BlockSpec((None, NPAIR, LANES, LANES), lambda n, j: (n, 0, 0, 0))],
        out_shape=[jax.ShapeDtypeStruct((nseq * t, D), F32),
                   jax.ShapeDtypeStruct((nseq, NPAIR, LANES, LANES), F32)],
        scratch_shapes=[pltpu.VMEM((NPAIR, LANES, LANES), F32)],
        compiler_params=pltpu.CompilerParams(dimension_semantics=("arbitrary", "arbitrary"),
                                             vmem_limit_bytes=VMEM_LIMIT),
        name="wkv_chunk",
    )(*vecs, gain, bias, rk, tri, ones_bd)
    t6 = t_fin.reshape(nseq, NPAIR, 2, KA, 2, KA)
    diag = jnp.stack([t6[:, :, 0, :, 0, :], t6[:, :, 1, :, 1, :]], axis=2)
    return ya, diag.reshape(nseq, HA, KA, KA).transpose(0, 1, 3, 2)


def _log_gamma():
    return [math.log1p(-2.0 ** (-5.0 - h)) for h in range(HB)]


def _rope_tables(pos):
    half = DK // 2
    theta = 1.0 / (ROPE_BASE ** jnp.linspace(0.0, 1.0, half, dtype=F32))
    ang = pos[:, None] * theta[None, :]
    cos, sin = jnp.cos(ang), jnp.sin(ang)
    return jnp.concatenate([cos, cos], axis=1), jnp.concatenate([-sin, sin], axis=1)


def _decay_tables(chunk):
    lg = jnp.asarray(_log_gamma(), F32)
    idx = jnp.arange(chunk, dtype=F32)
    diff = idx[:, None] - idx[None, :]
    dmat = jnp.exp(jnp.maximum(diff, 0.0)[None] * lg[:, None, None]) * (diff >= 0)[None]
    cross = jnp.exp((idx[None, :] + 1.0) * lg[:, None])[:, :, None]
    kdec = jnp.exp((chunk - 1.0 - idx)[None, :] * lg[:, None])[:, :, None]
    sdec = jnp.exp(chunk * lg)
    return dmat.astype(F32), cross, kdec, sdec


def _rotary(x, cos2, sin2):
    return x * cos2 + pltpu.roll(x, DK // 2, axis=1) * sin2


def _ln_rows(y, eps):
    m = jnp.mean(y, axis=-1, keepdims=True)
    yc = y - m
    var = jnp.mean(yc * yc, axis=-1, keepdims=True)
    return yc * lax.rsqrt(var + eps)


def _ret_prompt_kernel(qk_ref, vb_ref, cos_ref, sin_ref, dmat_ref, cross_ref, kdec_ref, sdec_ref,
                       o_ref, sfin_ref, s_scr):
    j = pl.program_id(1)

    @pl.when(j == 0)
    def _():
        s_scr[...] = jnp.zeros_like(s_scr)

    cos2, sin2 = cos_ref[...], sin_ref[...]
    H = range(HB)
    qb = [_rotary(qk_ref[:, h * DK:(h + 1) * DK], cos2, sin2).astype(BF16) for h in H]
    k = [_rotary(qk_ref[:, HB * DK + h * DK:HB * DK + (h + 1) * DK], cos2, sin2) * (DK ** -0.5) for h in H]
    v = [vb_ref[:, h * DV:(h + 1) * DV].astype(BF16) for h in H]
    scores = [_bdot_nt(qb[h], k[h]) * dmat_ref[h] for h in H]
    s = [s_scr[h] for h in H]
    cross = [jnp.dot(qb[h], s[h].astype(BF16), preferred_element_type=F32) * cross_ref[h] for h in H]
    upd = [_bdot_tn(k[h] * kdec_ref[h], v[h]) for h in H]
    inner = [jnp.dot(scores[h].astype(BF16), v[h], preferred_element_type=F32) for h in H]
    for h in H:
        s_scr[h] = s[h] * sdec_ref[h] + upd[h]
        o_ref[:, h * DV:(h + 1) * DV] = _ln_rows(inner[h] + cross[h], LN_EPS)

    @pl.when(j == pl.num_programs(1) - 1)
    def _():
        sfin_ref[...] = s_scr[...]


def _ret_prompt(qk, vb, nseq, t):
    chunks = t // RET_CHUNK
    cos2, sin2 = _rope_tables(jnp.arange(t, dtype=F32) + 0)
    dmat, cross, kdec, sdec = _decay_tables(RET_CHUNK)
    row_spec = pl.BlockSpec((RET_CHUNK, D), lambda n, j: (n * chunks + j, 0))
    tab_spec = pl.BlockSpec((RET_CHUNK, DK), lambda n, j: (j, 0))
    return pl.pallas_call(
        _ret_prompt_kernel,
        grid=(nseq, chunks),
        in_specs=[row_spec, row_spec, tab_spec, tab_spec,
                  _const_spec((HB, RET_CHUNK, RET_CHUNK)), _const_spec((HB, RET_CHUNK, 1)),
                  _const_spec((HB, RET_CHUNK, 1)),
                  pl.BlockSpec(memory_space=pltpu.SMEM)],
        out_specs=[row_spec, pl.BlockSpec((None, HB, DK, DV), lambda n, j: (n, 0, 0, 0))],
        out_shape=[jax.ShapeDtypeStruct(qk.shape, F32),
                   jax.ShapeDtypeStruct((nseq, HB, DK, DV), F32)],
        scratch_shapes=[pltpu.VMEM((HB, DK, DV), F32)],
        compiler_params=pltpu.CompilerParams(dimension_semantics=("arbitrary", "arbitrary"),
                                             vmem_limit_bytes=VMEM_LIMIT),
        name="ret_prompt",
    )(qk, vb, cos2, sin2, dmat, cross, kdec, sdec)


def _ret_sample_kernel(qk_ref, vb_ref, cos_ref, sin_ref, s_ref, dmat_ref, cross_ref, kdec_ref, sdec_ref,
                       o_ref, snew_ref, *, nb):
    cos2, sin2 = cos_ref[...], sin_ref[...]
    for n in range(nb):
        for h in range(HB):
            q = _rotary(qk_ref[:, n, h * DK:(h + 1) * DK], cos2, sin2)
            k = _rotary(qk_ref[:, n, HB * DK + h * DK:HB * DK + (h + 1) * DK], cos2, sin2) * (DK ** -0.5)
            v = vb_ref[:, n, h * DV:(h + 1) * DV].astype(BF16)
            qb = q.astype(BF16)
            scores = lax.dot_general(qb, k.astype(BF16), (((1,), (1,)), ((), ())),
                                     preferred_element_type=F32) * dmat_ref[h]
            s = s_ref[n, h]
            y = (jnp.dot(scores.astype(BF16), v, preferred_element_type=F32)
                 + jnp.dot(qb, s.astype(BF16), preferred_element_type=F32) * cross_ref[h])
            kd = (k * kdec_ref[h]).astype(BF16)
            snew_ref[n, h] = s * sdec_ref[h] + lax.dot_general(kd, v, (((0,), (0,)), ((), ())),
                                                               preferred_element_type=F32)
            o_ref[:, n, h * DV:(h + 1) * DV] = _ln_rows(y, LN_EPS)


def _ret_sample(qk3, vb3, s0, pos0, *, nb):
    t, nseq, _ = qk3.shape
    cos2, sin2 = _rope_tables(jnp.arange(t, dtype=F32) + pos0)
    dmat, cross, kdec, sdec = _decay_tables(t)
    row_spec = pl.BlockSpec((t, nb, D), lambda i: (0, i, 0))
    st_spec = pl.BlockSpec((nb, HB, DK, DV), lambda i: (i, 0, 0, 0))
    return pl.pallas_call(
        functools.partial(_ret_sample_kernel, nb=nb),
        grid=(nseq // nb,),
        in_specs=[row_spec, row_spec, _const_spec((t, DK)), _const_spec((t, DK)), st_spec,
                  _const_spec((HB, t, t)), _const_spec((HB, t, 1)), _const_spec((HB, t, 1)),
                  pl.BlockSpec(memory_space=pltpu.SMEM)],
        out_specs=[row_spec, st_spec],
        out_shape=[jax.ShapeDtypeStruct(qk3.shape, F32), jax.ShapeDtypeStruct(s0.shape, F32)],
        compiler_params=pltpu.CompilerParams(dimension_semantics=("arbitrary",),
                                             vmem_limit_bytes=VMEM_LIMIT),
        name="ret_sample",
    )(qk3, vb3, cos2, sin2, s0, dmat, cross, kdec, sdec)


def _unpair_tokens(ya_ref, ya_scr, pair_rows):
    half = ya_ref.shape[0]
    groups = half // pair_rows
    low = lax.broadcasted_iota(jnp.int32, (half, LANES), 1) < KA
    for m in range(D // LANES):
        ha = ya_ref[:, 2 * m, :]
        hb = ya_ref[:, 2 * m + 1, :]
        lo = jnp.where(low, ha, pltpu.roll(hb, KA, axis=1))
        hi = jnp.where(low, pltpu.roll(ha, KA, axis=1), hb)
        ya_scr[:, 0, :, m * LANES:(m + 1) * LANES] = lo.reshape(groups, pair_rows, LANES)
        ya_scr[:, 1, :, m * LANES:(m + 1) * LANES] = hi.reshape(groups, pair_rows, LANES)
    return ya_scr[...].reshape(2 * half, D)


def _post_kernel(ga_ref, ya_ref, gb_ref, ob_ref, x_ref, gate_ref, wout_ref, lng_ref, lnb_ref, y_ref,
                 *scratch, gs, alpha, pair_rows):
    ya = _unpair_tokens(ya_ref, scratch[0], pair_rows) if pair_rows else ya_ref[...]
    merged = ga_ref[...] * ya + gb_ref[...] * ob_ref[...]
    sub = jnp.dot(merged.astype(BF16), wout_ref[...], preferred_element_type=F32)
    x = x_ref[...]
    gate = gate_ref[...]
    if gs == 1:
        z = alpha * x + gate * sub
    else:
        tt = x.shape[0]
        z = (alpha * x.reshape(tt // gs, gs, D) + gate[None] * sub.reshape(tt // gs, gs, D)).reshape(tt, D)
    y_ref[...] = _ln_rows(z, LN_EPS) * lng_ref[...] + lnb_ref[...]


def _post(ga, ya, gb, ob, x2, gate, wout, ln_gain, ln_bias, *, nseq_grid, tiles, tt, gs, alpha, pair_rows):
    row_spec = pl.BlockSpec((tt, D), lambda n, j: (n * tiles + j, 0))
    if pair_rows:
        ya_spec = pl.BlockSpec((tt // 2, HA, LANES), lambda n, j: (n * tiles + j, 0, 0))
        scratch = [pltpu.VMEM((tt // 2 // pair_rows, 2, pair_rows, D), F32)]
    else:
        ya_spec, scratch = row_spec, []
    return pl.pallas_call(
        functools.partial(_post_kernel, gs=gs, alpha=alpha, pair_rows=pair_rows),
        grid=(nseq_grid, tiles),
        in_specs=[row_spec, ya_spec, row_spec, row_spec, row_spec,
                  pl.BlockSpec((None, gs, D), lambda n, j: (n, 0, 0)),
                  _const_spec((D, D)), _const_spec((1, D)), _const_spec((1, D))],
        out_specs=row_spec,
        out_shape=jax.ShapeDtypeStruct(x2.shape, F32),
        scratch_shapes=scratch,
        compiler_params=pltpu.CompilerParams(dimension_semantics=("arbitrary", "arbitrary"),
                                             vmem_limit_bytes=VMEM_LIMIT),
        name="post",
    )(ga, ya, gb, ob, x2, gate, wout, ln_gain, ln_bias)


def _head_param_lanes(p, nseq):
    return jnp.tile(p.reshape(HA, KA).T, (1, nseq))


def _layer(x2, ada, first_prev, s_wkv, s_ret, pos0, weights, *, nseq, t, seq_major, tt, tc, depth):
    (wsh, wrest, pre_params, gn_gain, gn_bias, r_k, wout, ln_gain, ln_bias) = weights
    if seq_major:
        nseq_grid, gs, tiles = nseq, 1, t // tt
    else:
        nseq_grid, gs, tiles = 1, nseq, (t * nseq) // tt
    shift_c = ada[:, :D].reshape(nseq_grid, gs, D)
    scale_c = ada[:, D:2 * D].reshape(nseq_grid, gs, D)
    gate_c = ada[:, 2 * D:].reshape(nseq_grid, gs, D)
    cfg = dict(nseq_grid=nseq_grid, tiles=tiles, tt=tt, gs=gs)

    *vecs, new_shift = _pre_a(x2, scale_c, shift_c, first_prev.reshape(nseq_grid, gs, SHIFT_W), wsh, pre_params,
                              pair_out=not seq_major, **cfg)
    ga, gb, qk, vb = _pre_b(x2, scale_c, shift_c, wrest, **cfg)

    if seq_major:
        row = lambda p: p.reshape(1, D)
        ya, new_wkv = _wkv_chunked(vecs, row(gn_gain), row(gn_bias), row(r_k), nseq=nseq, t=t, ct=tc)
        pair_rows = None
    else:
        s0 = s_wkv.transpose(3, 2, 0, 1).reshape(KA, KA, nseq * HA)
        ya, s_fin = _wkv(vecs, s0, _head_param_lanes(gn_gain, nseq),
                         _head_param_lanes(gn_bias, nseq), _head_param_lanes(r_k.reshape(-1), nseq),
                         nseq=nseq, t=t, tc=tc, seq_major=False)
        pair_rows = (tc // 2) * gs
        new_wkv = s_fin.reshape(KA, KA, nseq, HA).transpose(2, 3, 1, 0)

    if seq_major:
        ob, new_ret = _ret_prompt(qk, vb, nseq, t)
    else:
        ob3, new_ret = _ret_sample(qk.reshape(t, nseq, D), vb.reshape(t, nseq, D), s_ret, pos0, nb=8)
        ob = ob3.reshape(t * nseq, D)

    alpha = (2.0 * depth) ** 0.25
    y = _post(ga, ya, gb, ob, x2, gate_c, wout, ln_gain, ln_bias, alpha=alpha, pair_rows=pair_rows, **cfg)
    return y, new_shift.reshape(nseq, SHIFT_W), new_wkv, new_ret


def kernel(x_prompt, x_sample, c_prompt, c_sample, state_shift, state_wkv, state_ret, w_ada, b_ada, w_in,
           mu_shift, w0, w_decay_up, a0, w_icl_up, k_k, k_a, r_k, gn_a_gain, gn_a_bias, w_out, ln_gain, ln_bias):
    depth = w_in.shape[0]
    nb, t_p, _ = x_prompt.shape
    ns, t_s, _ = x_sample.shape
    past_len = 16384
    h_p = x_prompt.reshape(nb * t_p, D)
    h_s = x_sample.transpose(1, 0, 2).reshape(t_s * ns, D)
    outs = [[] for _ in range(6)]
    zero_lora = jnp.zeros((LORA, D), F32)
    for l in range(depth):
        row = lambda p: p[l].reshape(1, -1)
        wsh = w_in[l][:, :SHIFT_W].astype(BF16)
        wrest = w_in[l][:, SHIFT_W:].astype(BF16)
        wd_pad = jnp.concatenate([w_decay_up[l], zero_lora], axis=0).astype(BF16)
        wa_pad = jnp.concatenate([zero_lora, w_icl_up[l]], axis=0).astype(BF16)
        pre_params = (row(mu_shift), row(w0), row(a0), row(k_k), row(k_a), wd_pad, wa_pad)
        weights = (wsh, wrest, pre_params, gn_a_gain[l], gn_a_bias[l], r_k[l], w_out[l].astype(BF16),
                   row(ln_gain), row(ln_bias))
        ada = _ada(jnp.concatenate([c_prompt, c_sample], axis=0), w_ada[l], b_ada[l])
        h_p, s1, s2, s3 = _layer(
            h_p, ada[:nb], jnp.zeros((nb, SHIFT_W), F32), None, None, 0,
            weights, nseq=nb, t=t_p, seq_major=True, tt=256, tc=256, depth=depth)
        h_s, t1, t2, t3 = _layer(
            h_s, ada[nb:], state_shift[l], state_wkv[l], state_ret[l], past_len,
            weights, nseq=ns, t=t_s, seq_major=False, tt=t_s * ns, tc=t_s, depth=depth)
        for lst, val in zip(outs, (s1, s2, s3, t1, t2, t3)):
            lst.append(val)
    y_p = h_p.reshape(nb, t_p, D)
    y_s = h_s.reshape(t_s, ns, D).transpose(1, 0, 2)
    return (y_p, y_s) + tuple(jnp.stack(o) for o in outs)
```

```python
import functools
import math

import numpy as np
import jax
import jax.numpy as jnp
from jax import lax
from jax.experimental import pallas as pl
from jax.experimental.pallas import tpu as pltpu

F32 = jnp.float32
BF16 = jnp.bfloat16

D = 1024
HA = 16
KA = 64
LORA = 64
GN_EPS = 64e-5
HB = 4
DK = 128
DV = 256
RET_CHUNK = 128
ROPE_BASE = 10000.0
LN_EPS = 1e-5
SHIFT_W = 3 * D + 2 * LORA
REST_W = 6 * D
LANES = 128
SUBLANES = 8
VMEM_LIMIT = 56 * 1024 * 1024


def _sigmoid(x):
    return 1.0 / (1.0 + jnp.exp(-x))


def _log_decay(w_pre):
    zw = -w_pre
    softplus = jnp.maximum(zw, 0.0) + jnp.log(1.0 + jnp.exp(-jnp.abs(zw)))
    return -jnp.exp(-softplus - 0.5)


def _bdot(a, b):
    return jnp.dot(a.astype(BF16), b.astype(BF16), preferred_element_type=F32)


def _bdot_nt(a, b):
    return lax.dot_general(a.astype(BF16), b.astype(BF16), (((1,), (1,)), ((), ())), preferred_element_type=F32)


def _bdot_tn(a, b):
    return lax.dot_general(a.astype(BF16), b.astype(BF16), (((0,), (0,)), ((), ())), preferred_element_type=F32)


def _split_bf16(x, parts):
    out = []
    for _ in range(parts):
        term = x.astype(BF16)
        out.append(term)
        x = x - term.astype(F32)
    return out


def _const_spec(shape):
    nd = len(shape)
    return pl.BlockSpec(shape, lambda *_: (0,) * nd, pipeline_mode=pl.Buffered(1))


def _cast_w_in_kernel(w_ref, sh_ref, rest_ref):
    sh_ref[...] = w_ref[:, :SHIFT_W].astype(BF16)
    rest_ref[...] = w_ref[:, SHIFT_W:].astype(BF16)


def _cast_w_in(w_in, layer):
    rb = 128
    return pl.pallas_call(
        _cast_w_in_kernel,
        grid=(D // rb,),
        in_specs=[pl.BlockSpec((None, rb, SHIFT_W + REST_W), lambda i: (layer, i, 0))],
        out_specs=[pl.BlockSpec((rb, SHIFT_W), lambda i: (i, 0)), pl.BlockSpec((rb, REST_W), lambda i: (i, 0))],
        out_shape=[jax.ShapeDtypeStruct((D, SHIFT_W), BF16), jax.ShapeDtypeStruct((D, REST_W), BF16)],
        compiler_params=pltpu.CompilerParams(dimension_semantics=("arbitrary",)),
        name="cast_w_in",
    )(w_in)


def _ada_kernel(c_ref, w_ref, b_ref, o_ref):
    c = c_ref[...]
    s = c * _sigmoid(c)
    o_ref[...] = jnp.dot(s, w_ref[...], preferred_element_type=F32,
                         precision=lax.Precision.HIGHEST) + b_ref[...]


def _ada(c_all, w_ada, b_ada):
    n = c_all.shape[0]
    return pl.pallas_call(
        _ada_kernel,
        grid=(3,),
        in_specs=[pl.BlockSpec((n, D), lambda j: (0, 0)),
                  pl.BlockSpec((D, D), lambda j: (0, j)),
                  pl.BlockSpec((1, D), lambda j: (0, j))],
        out_specs=pl.BlockSpec((n, D), lambda j: (0, j)),
        out_shape=jax.ShapeDtypeStruct((n, 3 * D), F32),
        compiler_params=pltpu.CompilerParams(dimension_semantics=("arbitrary",)),
        name="ada",
    )(c_all, w_ada, b_ada.reshape(1, 3 * D))


def _modulate(x, scale, shift, gs):
    if gs == 1:
        return x * (1.0 + scale) + shift
    tt = x.shape[0]
    x3 = x.reshape(tt // gs, gs, D)
    return (x3 * (1.0 + scale)[None] + shift[None]).reshape(tt, D)


def _put_head_pairs(out_ref, scr, x, y):
    tt = x.shape[0]
    low = lax.broadcasted_iota(jnp.int32, (tt, LANES), 1) < KA
    for m in range(D // LANES):
        xc = x[:, m * LANES:(m + 1) * LANES]
        yc = y[:, m * LANES:(m + 1) * LANES]
        even = jnp.where(low, xc, pltpu.roll(yc, KA, axis=1))
        odd = jnp.where(low, pltpu.roll(xc, KA, axis=1), yc)
        for r in range(tt // SUBLANES):
            rows = slice(r * SUBLANES, (r + 1) * SUBLANES)
            scr[pl.ds((r * HA + 2 * m) * SUBLANES, SUBLANES), :] = even[rows]
            scr[pl.ds((r * HA + 2 * m + 1) * SUBLANES, SUBLANES), :] = odd[rows]

    def row_block(r, carry):
        for s in range(SUBLANES):
            out_ref[r * SUBLANES + s] = scr[pl.ds(r * (HA * SUBLANES) + s, HA, stride=SUBLANES), :]
        return carry

    lax.fori_loop(0, tt // SUBLANES, row_block, 0)


def _pre_a_kernel(x_ref, scale_ref, shift_ref, prev_ref, wsh_ref, mu_ref, w0_ref, a0_ref,
                  kk_ref, ka_ref, wd_ref, wa_ref, *rest, gs, tt, pad, pair_out):
    if pair_out:
        rk_out, wk_out, va_out, ns_out, p_scr, pair_scr = rest
    else:
        r_out, k_out, v_out, wpre_out, apre_out, ns_out, p_scr = rest
    j = pl.program_id(1)
    nj = pl.num_programs(1)
    lo = pad - gs

    @pl.when(j == 0)
    def _():
        p_scr[lo:pad, :] = prev_ref[...]

    u = _modulate(x_ref[...], scale_ref[...], shift_ref[...], gs).astype(BF16)

    def mixed(c0, c1):
        p_scr[pad:pad + tt, c0:c1] = jnp.dot(u, wsh_ref[:, c0:c1], preferred_element_type=F32)
        p = p_scr[pad:pad + tt, c0:c1]
        prev = p_scr[lo:lo + tt, c0:c1]
        return p + (prev - p) * mu_ref[:, c0:c1]

    lora_in = mixed(3 * D, SHIFT_W)
    lora_w = _bdot(jnp.tanh(lora_in), wd_ref[...])
    lora_a = _bdot(lora_in, wa_ref[...])
    w_pre = w0_ref[...] + lora_w
    a_pre = a0_ref[...] + lora_a
    k = mixed(D, 2 * D)
    if pair_out:
        a = _sigmoid(a_pre)
        _put_head_pairs(rk_out, pair_scr.at[0], mixed(0, D), k * (1.0 + (a - 1.0) * ka_ref[...]))
        _put_head_pairs(wk_out, pair_scr.at[1], jnp.exp(_log_decay(w_pre)), k * kk_ref[...])
        _put_head_pairs(va_out, pair_scr.at[2], mixed(2 * D, 3 * D), a)
    else:
        wpre_out[...] = w_pre
        apre_out[...] = a_pre
        k_out[...] = k
        r_out[...] = mixed(0, D)
        v_out[...] = mixed(2 * D, 3 * D)

    @pl.when(j == nj - 1)
    def _():
        ns_out[...] = p_scr[pad + tt - gs:pad + tt, :]

    p_scr[lo:pad, :] = p_scr[pad + tt - gs:pad + tt, :]


def _pre_a(x2, scale, shift, first_prev, wsh, params, *, nseq_grid, tiles, tt, gs, pair_out):
    rows = x2.shape[0]
    pad = max(SUBLANES, gs)
    mu, w0, a0, k_k, k_a, wd_pad, wa_pad = params
    row_spec = pl.BlockSpec((tt, D), lambda n, j: (n * tiles + j, 0))
    seq_spec = pl.BlockSpec((None, gs, D), lambda n, j: (n, 0, 0))
    scratch = [pltpu.VMEM((pad + tt, SHIFT_W), F32)]
    if pair_out:
        vec_specs = [pl.BlockSpec((tt, HA, LANES), lambda n, j: (n * tiles + j, 0, 0))] * 3
        vec_sds = [jax.ShapeDtypeStruct((rows, HA, LANES), F32)] * 3
        scratch.append(pltpu.VMEM((3, tt * HA, LANES), F32))
    else:
        vec_specs = [row_spec] * 5
        vec_sds = [jax.ShapeDtypeStruct((rows, D), F32)] * 5
    kern = functools.partial(_pre_a_kernel, gs=gs, tt=tt, pad=pad, pair_out=pair_out)
    return pl.pallas_call(
        kern,
        grid=(nseq_grid, tiles),
        in_specs=[row_spec, seq_spec, seq_spec,
                  pl.BlockSpec((None, gs, SHIFT_W), lambda n, j: (n, 0, 0)),
                  _const_spec((D, SHIFT_W)), _const_spec((1, SHIFT_W)),
                  _const_spec((1, D)), _const_spec((1, D)), _const_spec((1, D)), _const_spec((1, D)),
                  _const_spec((LANES, D)), _const_spec((LANES, D))],
        out_specs=vec_specs + [pl.BlockSpec((None, gs, SHIFT_W), lambda n, j: (n, 0, 0))],
        out_shape=vec_sds + [jax.ShapeDtypeStruct((nseq_grid, gs, SHIFT_W), F32)],
        scratch_shapes=scratch,
        compiler_params=pltpu.CompilerParams(dimension_semantics=("arbitrary", "arbitrary"),
                                             vmem_limit_bytes=VMEM_LIMIT),
        name="pre_a",
    )(x2, scale, shift, first_prev, wsh, mu, w0, a0, k_k, k_a, wd_pad, wa_pad)


def _pre_b_kernel(x_ref, scale_ref, shift_ref, w_ref, ga_out, gb_out, qk_out, vb_out, *, gs):
    u = _modulate(x_ref[...], scale_ref[...], shift_ref[...], gs).astype(BF16)

    def proj(c0, c1):
        return jnp.dot(u, w_ref[:, c0:c1], preferred_element_type=F32)

    z_a = proj(0, D)
    g_a = proj(4 * D, 5 * D)
    ga_out[...] = _sigmoid(g_a) * (z_a * _sigmoid(z_a))
    z_b = proj(3 * D, 4 * D)
    g_b = proj(5 * D, 6 * D)
    gb_out[...] = _sigmoid(g_b) * (z_b * _sigmoid(z_b))
    qk_out[...] = proj(D, 2 * D)
    vb_out[...] = proj(2 * D, 3 * D)


def _pre_b(x2, scale, shift, wrest, *, nseq_grid, tiles, tt, gs):
    rows = x2.shape[0]
    row_spec = pl.BlockSpec((tt, D), lambda n, j: (n * tiles + j, 0))
    seq_spec = pl.BlockSpec((None, gs, D), lambda n, j: (n, 0, 0))
    out_sds = jax.ShapeDtypeStruct((rows, D), F32)
    return pl.pallas_call(
        functools.partial(_pre_b_kernel, gs=gs),
        grid=(nseq_grid, tiles),
        in_specs=[row_spec, seq_spec, seq_spec, _const_spec((D, REST_W))],
        out_specs=[row_spec] * 4,
        out_shape=[out_sds] * 4,
        compiler_params=pltpu.CompilerParams(dimension_semantics=("arbitrary", "arbitrary"),
                                             vmem_limit_bytes=VMEM_LIMIT),
        name="pre_b",
    )(x2, scale, shift, wrest)


V_KKN, V_B, V_W, V_K, V_R, V_V = range(6)


def _wkv_kernel(rk_ref, wk_ref, va_ref, s0_ref, gain_ref, bias_ref, rkp_ref,
                o_ref, sfin_ref, s_scr, vec_scr, row_scr, y_scr, cur_scr, *, tc, seq_major):
    j = pl.program_id(1)

    @pl.when(j == 0)
    def _():
        s_scr[...] = s0_ref[...]

    def load_pair(ref, t):
        x = ref[:, t] if seq_major else ref[t]
        xt = x.reshape(LANES, LANES).T
        return xt[:KA], xt[KA:]

    def prep(t, carry):
        dec, kkraw = load_pair(wk_ref, t)
        ss = jnp.sum(kkraw * kkraw, axis=0, keepdims=True)
        kkn = kkraw * (1.0 / jnp.maximum(jnp.sqrt(ss), 1e-12))
        v, a = load_pair(va_ref, t)
        r, kmod = load_pair(rk_ref, t)
        vec_scr[t, V_KKN] = kkn
        vec_scr[t, V_B] = kkn * a
        vec_scr[t, V_W] = dec
        vec_scr[t, V_K] = kmod
        vec_scr[t, V_R] = r
        vec_scr[t, V_V] = v
        rk = jnp.sum(r * kmod * rkp_ref[...], axis=0, keepdims=True)
        row_scr[t] = jnp.broadcast_to(rk, (SUBLANES, LANES))
        return carry

    def state_pass(t, carry):
        cur_scr[...] = vec_scr[t]
        skk = jnp.zeros((KA, LANES), F32)
        for k in range(KA):
            skk = skk + s_scr[k] * cur_scr[V_KKN, k:k + 1, :]
        vv = cur_scr[V_V]
        y = jnp.zeros((KA, LANES), F32)
        for k in range(KA):
            s_new = (s_scr[k] * cur_scr[V_W, k:k + 1, :] - skk * cur_scr[V_B, k:k + 1, :]
                     + vv * cur_scr[V_K, k:k + 1, :])
            s_scr[k] = s_new
            y = y + s_new * cur_scr[V_R, k:k + 1, :]
        y_scr[t] = y
        return carry

    def out_token(t):
        y = y_scr[t]
        m = jnp.mean(y, axis=0, keepdims=True)
        yc = y - m
        var = jnp.mean(yc * yc, axis=0, keepdims=True)
        yn = yc * lax.rsqrt(var + GN_EPS) * gain_ref[...] + bias_ref[...]
        return yn + row_scr[t, 0:1, :] * vec_scr[t, V_V]

    def finish(i, carry):
        o2 = jnp.concatenate([out_token(i), out_token(i + tc // 2)], axis=0)
        o = o2.T.reshape(LANES // HA, HA, LANES)
        if seq_major:
            o_ref[:, i] = o
        else:
            o_ref[i] = o
        return carry

    lax.fori_loop(0, tc, prep, 0, unroll=2)
    lax.fori_loop(0, tc, state_pass, 0)
    lax.fori_loop(0, tc // 2, finish, 0, unroll=2)

    @pl.when(j == pl.num_programs(1) - 1)
    def _():
        sfin_ref[...] = s_scr[...]


def _wkv(pairs, s0, gain_l, bias_l, rk_l, *, nseq, t, tc, seq_major):
    nsl = LANES // HA
    groups = nseq // nsl
    if seq_major:
        shape4, oshape4 = (nseq, t, HA, LANES), (nseq, t // 2, HA, LANES)
        vec_spec = pl.BlockSpec((nsl, tc, HA, LANES), lambda g, j: (g, j, 0, 0))
        out_spec = pl.BlockSpec((nsl, tc // 2, HA, LANES), lambda g, j: (g, j, 0, 0))
    else:
        shape4, oshape4 = (t, nseq, HA, LANES), (t // 2, nseq, HA, LANES)
        vec_spec = pl.BlockSpec((tc, nsl, HA, LANES), lambda g, j: (j, g, 0, 0))
        out_spec = pl.BlockSpec((tc // 2, nsl, HA, LANES), lambda g, j: (j, g, 0, 0))
    st_spec = pl.BlockSpec((KA, KA, LANES), lambda g, j: (0, 0, g))
    par_spec = pl.BlockSpec((KA, LANES), lambda g, j: (0, g))
    o4, s_fin = pl.pallas_call(
        functools.partial(_wkv_kernel, tc=tc, seq_major=seq_major),
        grid=(groups, t // tc),
        in_specs=[vec_spec] * 3 + [st_spec, par_spec, par_spec, par_spec],
        out_specs=[out_spec, st_spec],
        out_shape=[jax.ShapeDtypeStruct(oshape4, F32),
                   jax.ShapeDtypeStruct((KA, KA, nseq * HA), F32)],
        scratch_shapes=[pltpu.VMEM((KA, KA, LANES), F32), pltpu.VMEM((tc, 6, KA, LANES), F32),
                        pltpu.VMEM((tc, SUBLANES, LANES), F32), pltpu.VMEM((tc, KA, LANES), F32),
                        pltpu.VMEM((6, KA, LANES), F32)],
        compiler_params=pltpu.CompilerParams(dimension_semantics=("arbitrary", "arbitrary"),
                                             vmem_limit_bytes=VMEM_LIMIT),
        name="wkv",
    )(*[z.reshape(shape4) for z in pairs], s0, gain_l, bias_l, rk_l)
    return o4.reshape(nseq * t // 2, HA, LANES), s_fin


CH = 64
NPAIR = D // LANES


def _wkv_chunk_kernel(r_ref, k_ref, v_ref, wpre_ref, apre_ref, kk_ref, ka_ref, gain_ref, bias_ref, rk_ref,
                      tri_ref, ones_ref, o_ref, tfin_ref, t_scr, *, ct):
    j = pl.program_id(1)

    @pl.when(j == 0)
    def _():
        t_scr[...] = jnp.zeros_like(t_scr)

    lane = lax.broadcasted_iota(jnp.int32, (CH, LANES), 1)
    row = lax.broadcasted_iota(jnp.int32, (CH, LANES), 0)
    low = lane < KA
    col = jnp.bitwise_and(lane, KA - 1)
    strict = row > col
    incl = row >= col
    eye = (row == col).astype(F32)
    low2 = lax.broadcasted_iota(jnp.int32, (LANES, LANES), 1) < KA
    diag_blocks = (lax.broadcasted_iota(jnp.int32, (LANES, LANES), 0) < KA) == low2
    tri = tri_ref[...]
    ones_bd = ones_ref[...]

    def seg_sum(x, parts=2):
        return sum(jnp.dot(term, ones_bd, preferred_element_type=F32) for term in _split_bf16(x, parts))

    def bd(z):
        return jnp.concatenate([jnp.where(low, z, 0.0), jnp.where(low, 0.0, z)], axis=0)

    def abd(z):
        return jnp.concatenate([jnp.where(low, 0.0, z), jnp.where(low, z, 0.0)], axis=0)

    def seg_sum_pairs(xs, parts):
        tot = seg_sum(jnp.concatenate(xs, axis=0), parts)
        return [tot[m * CH:(m + 1) * CH] for m in range(NPAIR)]

    P = range(NPAIR)
    ln = [slice(m * LANES, (m + 1) * LANES) for m in P]

    def chunk_free(c0, ctx):
        rows = slice(c0, c0 + CH)
        lw_all = _log_decay(wpre_ref[rows, :])
        cs_all = sum(jnp.dot(tri, term, preferred_element_type=F32) for term in _split_bf16(lw_all, 2))
        yield
        lw = [lw_all[:, ln[m]] for m in P]
        cs = [cs_all[:, ln[m]] for m in P]
        k_in = [k_ref[rows, ln[m]] for m in P]
        kkraw = [k_in[m] * kk_ref[:, ln[m]] for m in P]
        ss = seg_sum_pairs([kkraw[m] * kkraw[m] for m in P], 2)
        yield
        a = [_sigmoid(apre_ref[rows, ln[m]]) for m in P]
        kkn = [kkraw[m] / jnp.maximum(jnp.sqrt(ss[m]), 1e-12) for m in P]
        b = [kkn[m] * a[m] for m in P]
        r = [r_ref[rows, ln[m]] for m in P]
        k = [k_in[m] * (1.0 + (a[m] - 1.0) * ka_ref[:, ln[m]]) for m in P]
        v = [v_ref[rows, ln[m]] for m in P]
        cs_last = [cs[m][CH - 1:CH, :] for m in P]
        e_neg = [jnp.exp(-cs[m]) for m in P]
        a_t = [-kkn[m] * jnp.exp(cs[m] - lw[m]) for m in P]
        r_t = [r[m] * jnp.exp(cs[m]) for m in P]
        b_t = [b[m] * e_neg[m] for m in P]
        k_t = [k[m] * e_neg[m] for m in P]
        lhs = [jnp.concatenate([a_t[m], r_t[m]], axis=0) for m in P]
        out0 = [_bdot_nt(jnp.where(low2, lhs[m], 0.0), jnp.concatenate([b_t[m], k_t[m]], axis=0)) for m in P]
        yield
        out1 = [_bdot_nt(jnp.where(low2, 0.0, lhs[m]), jnp.concatenate([k_t[m], b_t[m]], axis=0)) for m in P]
        yield
        l_cat = [jnp.where(strict, jnp.where(low, out0[m][:CH], out1[m][:CH]), 0.0) for m in P]
        ak_cat = [jnp.where(strict, jnp.where(low, out1[m][:CH], out0[m][:CH]), 0.0) for m in P]
        rb_cat = [jnp.where(incl, jnp.where(low, out0[m][CH:], out1[m][CH:]), 0.0) for m in P]
        rk_cat = [jnp.where(incl, jnp.where(low, out1[m][CH:], out0[m][CH:]), 0.0) for m in P]
        rkv = seg_sum_pairs([r[m] * k[m] * rk_ref[:, ln[m]] for m in P], 1)
        x = [eye + l_cat[m] for m in P]
        lp = [_bdot(l_cat[m], bd(l_cat[m])) for m in P]
        yield
        for _ in range(4):
            both = [_bdot(jnp.concatenate([lp[m], x[m]], axis=0), bd(lp[m])) for m in P]
            yield
            x = [x[m] + both[m][CH:] for m in P]
            lp = [both[m][:CH] for m in P]
        x = [x[m] + _bdot(x[m], bd(lp[m])) for m in P]
        e_end = [jnp.exp(cs_last[m] - cs[m]) for m in P]
        ctx.update(
            rows=rows, x=x, v=v, rkv=rkv,
            g_lhs=[jnp.concatenate([a_t[m], ak_cat[m]], axis=1) for m in P],
            y_lhs=[jnp.concatenate([r_t[m], rb_cat[m], rk_cat[m]], axis=1) for m in P],
            upd_lhs=[jnp.concatenate([b[m] * e_end[m], k[m] * e_end[m]], axis=0) for m in P],
            p_end=[jnp.exp(cs_last[m]) for m in P])
        yield

    def chunk_state(ctx):
        x, v = ctx["x"], ctx["v"]
        t0 = [t_scr[m] for m in P]
        v_abd = [abd(v[m]) for m in P]
        g = [_bdot(ctx["g_lhs"][m], jnp.concatenate([t0[m], v_abd[m]], axis=0)) for m in P]
        yield
        u = [_bdot(x[m], bd(g[m])) for m in P]
        yield
        upd = [_bdot_tn(ctx["upd_lhs"][m], jnp.concatenate([u[m], v[m]], axis=0)) for m in P]
        yield
        for m in P:
            p_col = jnp.broadcast_to(ctx["p_end"][m], (LANES, LANES)).T
            t_scr[m] = t0[m] * p_col + jnp.where(diag_blocks, upd[m], 0.0)
        y = [_bdot(ctx["y_lhs"][m], jnp.concatenate([t0[m], bd(u[m]), v_abd[m]], axis=0)) for m in P]
        yield
        mean = seg_sum_pairs(y, 1)
        yield
        yc = [y[m] - mean[m] * (1.0 / KA) for m in P]
        var = seg_sum_pairs([yc[m] * yc[m] for m in P], 1)
        yield
        for m in P:
            yn = yc[m] * lax.rsqrt(var[m] * (1.0 / KA) + GN_EPS) * gain_ref[:, ln[m]] + bias_ref[:, ln[m]]
            o_ref[ctx["rows"], ln[m]] = yn + ctx["rkv"][m] * v[m]
        yield

    def interleave(*gens):
        live = list(gens)
        while live:
            for gen in list(live):
                try:
                    next(gen)
                except StopIteration:
                    live.remove(gen)

    ctxs = [dict() for _ in range(ct // CH)]
    interleave(chunk_free(0, ctxs[0]))
    for c in range(ct // CH):
        nxt = [chunk_free((c + 1) * CH, ctxs[c + 1])] if c + 1 < ct // CH else []
        interleave(chunk_state(ctxs[c]), *nxt)

    @pl.when(j == pl.num_programs(1) - 1)
    def _():
        tfin_ref[...] = t_scr[...]


def _wkv_chunked(vecs, k_k, k_a, gain, bias, rk, *, nseq, t, ct):
    tiles = t // ct
    row_spec = pl.BlockSpec((ct, D), lambda n, j: (n * tiles + j, 0))
    tri = jnp.asarray(np.tril(np.ones((CH, CH), np.float32)), BF16)
    head_of_lane = np.arange(LANES) // KA
    ones_bd = jnp.asarray((head_of_lane[:, None] == head_of_lane[None, :]).astype(np.float32), BF16)
    ya, t_fin = pl.pallas_call(
        functools.partial(_wkv_chunk_kernel, ct=ct),
        grid=(nseq, tiles),
        in_specs=[row_spec] * 5 + [_const_spec((1, D))] * 5 + [_const_spec((CH, CH)), _const_spec((LANES, LANES))],
        out_specs=[row_spec, pl.BlockSpec((None, NPAIR, LANES, LANES), lambda n, j: (n, 0, 0, 0))],
        out_shape=[jax.ShapeDtypeStruct((nseq * t, D), F32),
                   jax.ShapeDtypeStruct((nseq, NPAIR, LANES, LANES), F32)],
        scratch_shapes=[pltpu.VMEM((NPAIR, LANES, LANES), F32)],
        compiler_params=pltpu.CompilerParams(dimension_semantics=("arbitrary", "arbitrary"),
                                             vmem_limit_bytes=VMEM_LIMIT),
        name="wkv_chunk",
    )(*vecs, k_k, k_a, gain, bias, rk, tri, ones_bd)
    t6 = t_fin.reshape(nseq, NPAIR, 2, KA, 2, KA)
    diag = jnp.stack([t6[:, :, 0, :, 0, :], t6[:, :, 1, :, 1, :]], axis=2)
    return ya, diag.reshape(nseq, HA, KA, KA).transpose(0, 1, 3, 2)


def _log_gamma():
    return [math.log1p(-2.0 ** (-5.0 - h)) for h in range(HB)]


def _rope_tables(pos):
    half = DK // 2
    theta = 1.0 / (ROPE_BASE ** jnp.linspace(0.0, 1.0, half, dtype=F32))
    ang = pos[:, None] * theta[None, :]
    cos, sin = jnp.cos(ang), jnp.sin(ang)
    return jnp.concatenate([cos, cos], axis=1), jnp.concatenate([-sin, sin], axis=1)


def _decay_tables(chunk):
    lg = jnp.asarray(_log_gamma(), F32)
    idx = jnp.arange(chunk, dtype=F32)
    diff = idx[:, None] - idx[None, :]
    dmat = jnp.exp(jnp.maximum(diff, 0.0)[None] * lg[:, None, None]) * (diff >= 0)[None]
    cross = jnp.exp((idx[None, :] + 1.0) * lg[:, None])[:, :, None]
    kdec = jnp.exp((chunk - 1.0 - idx)[None, :] * lg[:, None])[:, :, None]
    sdec = jnp.exp(chunk * lg)
    return dmat.astype(F32), cross, kdec, sdec


def _rotary(x, cos2, sin2):
    return x * cos2 + pltpu.roll(x, DK // 2, axis=1) * sin2


def _ln_rows(y, eps):
    m = jnp.mean(y, axis=-1, keepdims=True)
    yc = y - m
    var = jnp.mean(yc * yc, axis=-1, keepdims=True)
    return yc * lax.rsqrt(var + eps)


def _ret_post_kernel(qk_ref, vb_ref, ga_ref, ya_ref, gb_ref, x_ref, gate_ref, cos_ref, sin_ref,
                     dmat_ref, cross_ref, kdec_ref, sdec_ref, wout_ref, lng_ref, lnb_ref,
                     y_ref, sfin_ref, s_scr, *, tt, alpha):
    j = pl.program_id(1)

    @pl.when(j == 0)
    def _():
        s_scr[...] = jnp.zeros_like(s_scr)

    H = range(HB)
    merged = []
    for c0 in range(0, tt, RET_CHUNK):
        rows = slice(c0, c0 + RET_CHUNK)
        cos2, sin2 = cos_ref[rows, :], sin_ref[rows, :]
        qb = [_rotary(qk_ref[rows, h * DK:(h + 1) * DK], cos2, sin2).astype(BF16) for h in H]
        k = [_rotary(qk_ref[rows, HB * DK + h * DK:HB * DK + (h + 1) * DK], cos2, sin2) * (DK ** -0.5)
             for h in H]
        v = [vb_ref[rows, h * DV:(h + 1) * DV].astype(BF16) for h in H]
        scores = [_bdot_nt(qb[h], k[h]) * dmat_ref[h] for h in H]
        s = [s_scr[h] for h in H]
        cross = [jnp.dot(qb[h], s[h].astype(BF16), preferred_element_type=F32) * cross_ref[h] for h in H]
        upd = [_bdot_tn(k[h] * kdec_ref[h], v[h]) for h in H]
        inner = [jnp.dot(scores[h].astype(BF16), v[h], preferred_element_type=F32) for h in H]
        ob = []
        for h in H:
            s_scr[h] = s[h] * sdec_ref[h] + upd[h]
            ob.append(_ln_rows(inner[h] + cross[h], LN_EPS))
        merged.append((ga_ref[rows, :] * ya_ref[rows, :]
                       + gb_ref[rows, :] * jnp.concatenate(ob, axis=1)).astype(BF16))
    sub = jnp.dot(jnp.concatenate(merged, axis=0), wout_ref[...], preferred_element_type=F32)
    z = alpha * x_ref[...] + gate_ref[...] * sub
    y_ref[...] = _ln_rows(z, LN_EPS) * lng_ref[...] + lnb_ref[...]

    @pl.when(j == pl.num_programs(1) - 1)
    def _():
        sfin_ref[...] = s_scr[...]


def _ret_post(qk, vb, ga, ya, gb, x2, gate, wout, ln_gain, ln_bias, *, nseq, t, tt, alpha):
    tiles = t // tt
    cos2, sin2 = _rope_tables(jnp.arange(t, dtype=F32) + 0)
    dmat, cross, kdec, sdec = _decay_tables(RET_CHUNK)
    row_spec = pl.BlockSpec((tt, D), lambda n, j: (n * tiles + j, 0))
    tab_spec = pl.BlockSpec((tt, DK), lambda n, j: (j, 0))
    return pl.pallas_call(
        functools.partial(_ret_post_kernel, tt=tt, alpha=alpha),
        grid=(nseq, tiles),
        in_specs=[row_spec] * 6 + [pl.BlockSpec((None, 1, D), lambda n, j: (n, 0, 0)), tab_spec, tab_spec,
                                   _const_spec((HB, RET_CHUNK, RET_CHUNK)), _const_spec((HB, RET_CHUNK, 1)),
                                   _const_spec((HB, RET_CHUNK, 1)), pl.BlockSpec(memory_space=pltpu.SMEM),
                                   _const_spec((D, D)), _const_spec((1, D)), _const_spec((1, D))],
        out_specs=[row_spec, pl.BlockSpec((None, HB, DK, DV), lambda n, j: (n, 0, 0, 0))],
        out_shape=[jax.ShapeDtypeStruct(x2.shape, F32),
                   jax.ShapeDtypeStruct((nseq, HB, DK, DV), F32)],
        scratch_shapes=[pltpu.VMEM((HB, DK, DV), F32)],
        compiler_params=pltpu.CompilerParams(dimension_semantics=("arbitrary", "arbitrary"),
                                             vmem_limit_bytes=VMEM_LIMIT),
        name="ret_post",
    )(qk, vb, ga, ya, gb, x2, gate, cos2, sin2, dmat, cross, kdec, sdec, wout, ln_gain, ln_bias)


def _ret_sample_kernel(qk_ref, vb_ref, cos_ref, sin_ref, s_ref, dmat_ref, cross_ref, kdec_ref, sdec_ref,
                       o_ref, snew_ref, *, nb):
    cos2, sin2 = cos_ref[...], sin_ref[...]
    for n in range(nb):
        for h in range(HB):
            q = _rotary(qk_ref[:, n, h * DK:(h + 1) * DK], cos2, sin2)
            k = _rotary(qk_ref[:, n, HB * DK + h * DK:HB * DK + (h + 1) * DK], cos2, sin2) * (DK ** -0.5)
            v = vb_ref[:, n, h * DV:(h + 1) * DV].astype(BF16)
            qb = q.astype(BF16)
            scores = lax.dot_general(qb, k.astype(BF16), (((1,), (1,)), ((), ())),
                                     preferred_element_type=F32) * dmat_ref[h]
            s = s_ref[n, h]
            y = (jnp.dot(scores.astype(BF16), v, preferred_element_type=F32)
                 + jnp.dot(qb, s.astype(BF16), preferred_element_type=F32) * cross_ref[h])
            kd = (k * kdec_ref[h]).astype(BF16)
            snew_ref[n, h] = s * sdec_ref[h] + lax.dot_general(kd, v, (((0,), (0,)), ((), ())),
                                                               preferred_element_type=F32)
            o_ref[:, n, h * DV:(h + 1) * DV] = _ln_rows(y, LN_EPS)


def _ret_sample(qk3, vb3, s0, pos0, *, nb):
    t, nseq, _ = qk3.shape
    cos2, sin2 = _rope_tables(jnp.arange(t, dtype=F32) + pos0)
    dmat, cross, kdec, sdec = _decay_tables(t)
    row_spec = pl.BlockSpec((t, nb, D), lambda i: (0, i, 0))
    st_spec = pl.BlockSpec((nb, HB, DK, DV), lambda i: (i, 0, 0, 0))
    return pl.pallas_call(
        functools.partial(_ret_sample_kernel, nb=nb),
        grid=(nseq // nb,),
        in_specs=[row_spec, row_spec, _const_spec((t, DK)), _const_spec((t, DK)), st_spec,
                  _const_spec((HB, t, t)), _const_spec((HB, t, 1)), _const_spec((HB, t, 1)),
                  pl.BlockSpec(memory_space=pltpu.SMEM)],
        out_specs=[row_spec, st_spec],
        out_shape=[jax.ShapeDtypeStruct(qk3.shape, F32), jax.ShapeDtypeStruct(s0.shape, F32)],
        compiler_params=pltpu.CompilerParams(dimension_semantics=("arbitrary",),
                                             vmem_limit_bytes=VMEM_LIMIT),
        name="ret_sample",
    )(qk3, vb3, cos2, sin2, s0, dmat, cross, kdec, sdec)


def _unpair_tokens(ya_ref, ya_scr, pair_rows):
    half = ya_ref.shape[0]
    groups = half // pair_rows
    low = lax.broadcasted_iota(jnp.int32, (half, LANES), 1) < KA
    for m in range(D // LANES):
        ha = ya_ref[:, 2 * m, :]
        hb = ya_ref[:, 2 * m + 1, :]
        lo = jnp.where(low, ha, pltpu.roll(hb, KA, axis=1))
        hi = jnp.where(low, pltpu.roll(ha, KA, axis=1), hb)
        ya_scr[:, 0, :, m * LANES:(m + 1) * LANES] = lo.reshape(groups, pair_rows, LANES)
        ya_scr[:, 1, :, m * LANES:(m + 1) * LANES] = hi.reshape(groups, pair_rows, LANES)
    return ya_scr[...].reshape(2 * half, D)


def _post_kernel(ga_ref, ya_ref, gb_ref, ob_ref, x_ref, gate_ref, wout_ref, lng_ref, lnb_ref, y_ref,
                 ya_scr, *, gs, alpha, pair_rows):
    merged = ga_ref[...] * _unpair_tokens(ya_ref, ya_scr, pair_rows) + gb_ref[...] * ob_ref[...]
    sub = jnp.dot(merged.astype(BF16), wout_ref[...], preferred_element_type=F32)
    x = x_ref[...]
    gate = gate_ref[...]
    if gs == 1:
        z = alpha * x + gate * sub
    else:
        tt = x.shape[0]
        z = (alpha * x.reshape(tt // gs, gs, D) + gate[None] * sub.reshape(tt // gs, gs, D)).reshape(tt, D)
    y_ref[...] = _ln_rows(z, LN_EPS) * lng_ref[...] + lnb_ref[...]


def _post(ga, ya, gb, ob, x2, gate, wout, ln_gain, ln_bias, *, nseq_grid, tiles, tt, gs, alpha, pair_rows):
    row_spec = pl.BlockSpec((tt, D), lambda n, j: (n * tiles + j, 0))
    ya_spec = pl.BlockSpec((tt // 2, HA, LANES), lambda n, j: (n * tiles + j, 0, 0))
    scratch = [pltpu.VMEM((tt // 2 // pair_rows, 2, pair_rows, D), F32)]
    return pl.pallas_call(
        functools.partial(_post_kernel, gs=gs, alpha=alpha, pair_rows=pair_rows),
        grid=(nseq_grid, tiles),
        in_specs=[row_spec, ya_spec, row_spec, row_spec, row_spec,
                  pl.BlockSpec((None, gs, D), lambda n, j: (n, 0, 0)),
                  _const_spec((D, D)), _const_spec((1, D)), _const_spec((1, D))],
        out_specs=row_spec,
        out_shape=jax.ShapeDtypeStruct(x2.shape, F32),
        scratch_shapes=scratch,
        compiler_params=pltpu.CompilerParams(dimension_semantics=("arbitrary", "arbitrary"),
                                             vmem_limit_bytes=VMEM_LIMIT),
        name="post",
    )(ga, ya, gb, ob, x2, gate, wout, ln_gain, ln_bias)


def _head_param_lanes(p, nseq):
    return jnp.tile(p.reshape(HA, KA).T, (1, nseq))


def _layer(x2, ada, first_prev, s_wkv, s_ret, pos0, weights, *, nseq, t, seq_major, tt, tc, depth):
    (wsh, wrest, pre_params, gn_gain, gn_bias, r_k, wout, ln_gain, ln_bias) = weights
    if seq_major:
        nseq_grid, gs, tiles = nseq, 1, t // tt
    else:
        nseq_grid, gs, tiles = 1, nseq, (t * nseq) // tt
    shift_c = ada[:, :D].reshape(nseq_grid, gs, D)
    scale_c = ada[:, D:2 * D].reshape(nseq_grid, gs, D)
    gate_c = ada[:, 2 * D:].reshape(nseq_grid, gs, D)
    cfg = dict(nseq_grid=nseq_grid, tiles=tiles, tt=tt, gs=gs)

    *vecs, new_shift = _pre_a(x2, scale_c, shift_c, first_prev.reshape(nseq_grid, gs, SHIFT_W), wsh, pre_params,
                              pair_out=not seq_major, **cfg)
    ga, gb, qk, vb = _pre_b(x2, scale_c, shift_c, wrest, **cfg)

    if seq_major:
        row = lambda p: p.reshape(1, D)
        ya, new_wkv = _wkv_chunked(vecs, pre_params[3], pre_params[4], row(gn_gain), row(gn_bias), row(r_k),
                                   nseq=nseq, t=t, ct=tc)
        pair_rows = None
    else:
        s0 = s_wkv.transpose(3, 2, 0, 1).reshape(KA, KA, nseq * HA)
        ya, s_fin = _wkv(vecs, s0, _head_param_lanes(gn_gain, nseq),
                         _head_param_lanes(gn_bias, nseq), _head_param_lanes(r_k.reshape(-1), nseq),
                         nseq=nseq, t=t, tc=tc, seq_major=False)
        pair_rows = (tc // 2) * gs
        new_wkv = s_fin.reshape(KA, KA, nseq, HA).transpose(2, 3, 1, 0)

    alpha = (2.0 * depth) ** 0.25
    if seq_major:
        y, new_ret = _ret_post(qk, vb, ga, ya, gb, x2, gate_c, wout, ln_gain, ln_bias,
                               nseq=nseq, t=t, tt=tt, alpha=alpha)
    else:
        ob3, new_ret = _ret_sample(qk.reshape(t, nseq, D), vb.reshape(t, nseq, D), s_ret, pos0, nb=8)
        y = _post(ga, ya, gb, ob3.reshape(t * nseq, D), x2, gate_c, wout, ln_gain, ln_bias, alpha=alpha,
                  pair_rows=pair_rows, **cfg)
    return y, new_shift.reshape(nseq, SHIFT_W), new_wkv, new_ret


def kernel(x_prompt, x_sample, c_prompt, c_sample, state_shift, state_wkv, state_ret, w_ada, b_ada, w_in,
           mu_shift, w0, w_decay_up, a0, w_icl_up, k_k, k_a, r_k, gn_a_gain, gn_a_bias, w_out, ln_gain, ln_bias):
    depth = w_in.shape[0]
    nb, t_p, _ = x_prompt.shape
    ns, t_s, _ = x_sample.shape
    past_len = 16384
    h_p = x_prompt.reshape(nb * t_p, D)
    h_s = x_sample.transpose(1, 0, 2).reshape(t_s * ns, D)
    outs = [[] for _ in range(6)]
    zero_lora = jnp.zeros((LORA, D), F32)
    for l in range(depth):
        row = lambda p: p[l].reshape(1, -1)
        wsh, wrest = _cast_w_in(w_in, l)
        wd_pad = jnp.concatenate([w_decay_up[l], zero_lora], axis=0).astype(BF16)
        wa_pad = jnp.concatenate([zero_lora, w_icl_up[l]], axis=0).astype(BF16)
        pre_params = (row(mu_shift), row(w0), row(a0), row(k_k), row(k_a), wd_pad, wa_pad)
        weights = (wsh, wrest, pre_params, gn_a_gain[l], gn_a_bias[l], r_k[l], w_out[l].astype(BF16),
                   row(ln_gain), row(ln_bias))
        ada = _ada(jnp.concatenate([c_prompt, c_sample], axis=0), w_ada[l], b_ada[l])
        h_p, s1, s2, s3 = _layer(
            h_p, ada[:nb], jnp.zeros((nb, SHIFT_W), F32), None, None, 0,
            weights, nseq=nb, t=t_p, seq_major=True, tt=256, tc=256, depth=depth)
        h_s, t1, t2, t3 = _layer(
            h_s, ada[nb:], state_shift[l], state_wkv[l], state_ret[l], past_len,
            weights, nseq=ns, t=t_s, seq_major=False, tt=t_s * ns, tc=t_s, depth=depth)
        for lst, val in zip(outs, (s1, s2, s3, t1, t2, t3)):
            lst.append(val)
    y_p = h_p.reshape(nb, t_p, D)
    y_s = h_s.reshape(t_s, ns, D).transpose(1, 0, 2)
    return (y_p, y_s) + tuple(jnp.stack(o) for o in outs)
```

```python
import functools
import math

import numpy as np
import jax
import jax.numpy as jnp
from jax import lax
from jax.experimental import pallas as pl
from jax.experimental.pallas import tpu as pltpu

F32 = jnp.float32
BF16 = jnp.bfloat16

D = 1024
HA = 16
KA = 64
LORA = 64
GN_EPS = 64e-5
HB = 4
DK = 128
DV = 256
RET_CHUNK = 128
ROPE_BASE = 10000.0
LN_EPS = 1e-5
SHIFT_W = 3 * D + 2 * LORA
REST_W = 6 * D
LANES = 128
SUBLANES = 8
VMEM_LIMIT = 56 * 1024 * 1024
PROMPT_TILES = dict(tt=256, tc=512, tp=512)


def _sigmoid(x):
    return 1.0 / (1.0 + jnp.exp(-x))


def _log_decay(w_pre):
    zw = -w_pre
    softplus = jnp.maximum(zw, 0.0) + jnp.log(1.0 + jnp.exp(-jnp.abs(zw)))
    return -jnp.exp(-softplus - 0.5)


def _bdot(a, b):
    return jnp.dot(a.astype(BF16), b.astype(BF16), preferred_element_type=F32)


def _bdot_nt(a, b):
    return lax.dot_general(a.astype(BF16), b.astype(BF16), (((1,), (1,)), ((), ())), preferred_element_type=F32)


def _bdot_tn(a, b):
    return lax.dot_general(a.astype(BF16), b.astype(BF16), (((0,), (0,)), ((), ())), preferred_element_type=F32)


def _split_bf16(x, parts):
    out = []
    for _ in range(parts):
        term = x.astype(BF16)
        out.append(term)
        x = x - term.astype(F32)
    return out


def _interleave(*gens):
    live = list(gens)
    while live:
        for gen in list(live):
            try:
                next(gen)
            except StopIteration:
                live.remove(gen)


def _const_spec(shape):
    nd = len(shape)
    return pl.BlockSpec(shape, lambda *_: (0,) * nd, pipeline_mode=pl.Buffered(1))


def _cast_w_in_kernel(w_ref, sh_ref, rest_ref):
    sh_ref[...] = w_ref[:, :SHIFT_W].astype(BF16)
    rest_ref[...] = w_ref[:, SHIFT_W:].astype(BF16)


def _cast_w_in(w_in, layer):
    rb = 128
    return pl.pallas_call(
        _cast_w_in_kernel,
        grid=(D // rb,),
        in_specs=[pl.BlockSpec((None, rb, SHIFT_W + REST_W), lambda i: (layer, i, 0))],
        out_specs=[pl.BlockSpec((rb, SHIFT_W), lambda i: (i, 0)), pl.BlockSpec((rb, REST_W), lambda i: (i, 0))],
        out_shape=[jax.ShapeDtypeStruct((D, SHIFT_W), BF16), jax.ShapeDtypeStruct((D, REST_W), BF16)],
        compiler_params=pltpu.CompilerParams(dimension_semantics=("arbitrary",)),
        name="cast_w_in",
    )(w_in)


def _ada_kernel(c_ref, w_ref, b_ref, o_ref):
    c = c_ref[...]
    s = c * _sigmoid(c)
    o_ref[...] = jnp.dot(s, w_ref[...], preferred_element_type=F32,
                         precision=lax.Precision.HIGHEST) + b_ref[...]


def _ada(c_all, w_ada, b_ada):
    n = c_all.shape[0]
    return pl.pallas_call(
        _ada_kernel,
        grid=(3,),
        in_specs=[pl.BlockSpec((n, D), lambda j: (0, 0)),
                  pl.BlockSpec((D, D), lambda j: (0, j)),
                  pl.BlockSpec((1, D), lambda j: (0, j))],
        out_specs=pl.BlockSpec((n, D), lambda j: (0, j)),
        out_shape=jax.ShapeDtypeStruct((n, 3 * D), F32),
        compiler_params=pltpu.CompilerParams(dimension_semantics=("arbitrary",)),
        name="ada",
    )(c_all, w_ada, b_ada.reshape(1, 3 * D))


def _modulate(x, scale, shift, gs):
    if gs == 1:
        return x * (1.0 + scale) + shift
    tt = x.shape[0]
    x3 = x.reshape(tt // gs, gs, D)
    return (x3 * (1.0 + scale)[None] + shift[None]).reshape(tt, D)


def _put_head_pairs(out_ref, scr, x, y):
    tt = x.shape[0]
    low = lax.broadcasted_iota(jnp.int32, (tt, LANES), 1) < KA
    for m in range(D // LANES):
        xc = x[:, m * LANES:(m + 1) * LANES]
        yc = y[:, m * LANES:(m + 1) * LANES]
        even = jnp.where(low, xc, pltpu.roll(yc, KA, axis=1))
        odd = jnp.where(low, pltpu.roll(xc, KA, axis=1), yc)
        for r in range(tt // SUBLANES):
            rows = slice(r * SUBLANES, (r + 1) * SUBLANES)
            scr[pl.ds((r * HA + 2 * m) * SUBLANES, SUBLANES), :] = even[rows]
            scr[pl.ds((r * HA + 2 * m + 1) * SUBLANES, SUBLANES), :] = odd[rows]

    def row_block(r, carry):
        for s in range(SUBLANES):
            out_ref[r * SUBLANES + s] = scr[pl.ds(r * (HA * SUBLANES) + s, HA, stride=SUBLANES), :]
        return carry

    lax.fori_loop(0, tt // SUBLANES, row_block, 0)


def _pre_a_kernel(x_ref, scale_ref, shift_ref, prev_ref, wsh_ref, mu_ref, w0_ref, a0_ref,
                  kk_ref, ka_ref, wd_ref, wa_ref, *rest, gs, tt, pad, pair_out):
    if pair_out:
        rk_out, wk_out, va_out, ns_out, p_scr, pair_scr = rest
    else:
        r_out, k_out, v_out, wpre_out, apre_out, ns_out, p_scr = rest
    j = pl.program_id(1)
    nj = pl.num_programs(1)
    lo = pad - gs

    @pl.when(j == 0)
    def _():
        p_scr[lo:pad, :] = prev_ref[...]

    u = _modulate(x_ref[...], scale_ref[...], shift_ref[...], gs).astype(BF16)

    def mixed(c0, c1):
        p_scr[pad:pad + tt, c0:c1] = jnp.dot(u, wsh_ref[:, c0:c1], preferred_element_type=F32)
        p = p_scr[pad:pad + tt, c0:c1]
        prev = p_scr[lo:lo + tt, c0:c1]
        return p + (prev - p) * mu_ref[:, c0:c1]

    lora_in = mixed(3 * D, SHIFT_W)
    lora_w = _bdot(jnp.tanh(lora_in), wd_ref[...])
    lora_a = _bdot(lora_in, wa_ref[...])
    w_pre = w0_ref[...] + lora_w
    a_pre = a0_ref[...] + lora_a
    k = mixed(D, 2 * D)
    if pair_out:
        a = _sigmoid(a_pre)
        _put_head_pairs(rk_out, pair_scr.at[0], mixed(0, D), k * (1.0 + (a - 1.0) * ka_ref[...]))
        _put_head_pairs(wk_out, pair_scr.at[1], jnp.exp(_log_decay(w_pre)), k * kk_ref[...])
        _put_head_pairs(va_out, pair_scr.at[2], mixed(2 * D, 3 * D), a)
    else:
        wpre_out[...] = w_pre
        apre_out[...] = a_pre
        k_out[...] = k
        r_out[...] = mixed(0, D)
        v_out[...] = mixed(2 * D, 3 * D)

    @pl.when(j == nj - 1)
    def _():
        ns_out[...] = p_scr[pad + tt - gs:pad + tt, :]

    p_scr[lo:pad, :] = p_scr[pad + tt - gs:pad + tt, :]


def _pre_a(x2, scale, shift, first_prev, wsh, params, *, nseq_grid, tiles, tt, gs, pair_out):
    rows = x2.shape[0]
    pad = max(SUBLANES, gs)
    mu, w0, a0, k_k, k_a, wd_pad, wa_pad = params
    row_spec = pl.BlockSpec((tt, D), lambda n, j: (n * tiles + j, 0))
    seq_spec = pl.BlockSpec((None, gs, D), lambda n, j: (n, 0, 0))
    scratch = [pltpu.VMEM((pad + tt, SHIFT_W), F32)]
    if pair_out:
        vec_specs = [pl.BlockSpec((tt, HA, LANES), lambda n, j: (n * tiles + j, 0, 0))] * 3
        vec_sds = [jax.ShapeDtypeStruct((rows, HA, LANES), F32)] * 3
        scratch.append(pltpu.VMEM((3, tt * HA, LANES), F32))
    else:
        vec_specs = [row_spec] * 5
        vec_sds = [jax.ShapeDtypeStruct((rows, D), F32)] * 5
    kern = functools.partial(_pre_a_kernel, gs=gs, tt=tt, pad=pad, pair_out=pair_out)
    return pl.pallas_call(
        kern,
        grid=(nseq_grid, tiles),
        in_specs=[row_spec, seq_spec, seq_spec,
                  pl.BlockSpec((None, gs, SHIFT_W), lambda n, j: (n, 0, 0)),
                  _const_spec((D, SHIFT_W)), _const_spec((1, SHIFT_W)),
                  _const_spec((1, D)), _const_spec((1, D)), _const_spec((1, D)), _const_spec((1, D)),
                  _const_spec((LANES, D)), _const_spec((LANES, D))],
        out_specs=vec_specs + [pl.BlockSpec((None, gs, SHIFT_W), lambda n, j: (n, 0, 0))],
        out_shape=vec_sds + [jax.ShapeDtypeStruct((nseq_grid, gs, SHIFT_W), F32)],
        scratch_shapes=scratch,
        compiler_params=pltpu.CompilerParams(dimension_semantics=("arbitrary", "arbitrary"),
                                             vmem_limit_bytes=VMEM_LIMIT),
        name="pre_a",
    )(x2, scale, shift, first_prev, wsh, mu, w0, a0, k_k, k_a, wd_pad, wa_pad)


def _pre_b_kernel(x_ref, scale_ref, shift_ref, w_ref, ga_out, gb_out, qk_out, vb_out, *, gs):
    u = _modulate(x_ref[...], scale_ref[...], shift_ref[...], gs).astype(BF16)

    def proj(c0, c1):
        return jnp.dot(u, w_ref[:, c0:c1], preferred_element_type=F32)

    z_a = proj(0, D)
    g_a = proj(4 * D, 5 * D)
    ga_out[...] = (_sigmoid(g_a) * (z_a * _sigmoid(z_a))).astype(ga_out.dtype)
    z_b = proj(3 * D, 4 * D)
    g_b = proj(5 * D, 6 * D)
    gb_out[...] = (_sigmoid(g_b) * (z_b * _sigmoid(z_b))).astype(gb_out.dtype)
    qk_out[...] = proj(D, 2 * D).astype(qk_out.dtype)
    vb_out[...] = proj(2 * D, 3 * D).astype(vb_out.dtype)


def _pre_b(x2, scale, shift, wrest, *, nseq_grid, tiles, tt, gs, out_dtype):
    rows = x2.shape[0]
    row_spec = pl.BlockSpec((tt, D), lambda n, j: (n * tiles + j, 0))
    seq_spec = pl.BlockSpec((None, gs, D), lambda n, j: (n, 0, 0))
    out_sds = jax.ShapeDtypeStruct((rows, D), out_dtype)
    return pl.pallas_call(
        functools.partial(_pre_b_kernel, gs=gs),
        grid=(nseq_grid, tiles),
        in_specs=[row_spec, seq_spec, seq_spec, _const_spec((D, REST_W))],
        out_specs=[row_spec] * 4,
        out_shape=[out_sds] * 4,
        compiler_params=pltpu.CompilerParams(dimension_semantics=("arbitrary", "arbitrary"),
                                             vmem_limit_bytes=VMEM_LIMIT),
        name="pre_b",
    )(x2, scale, shift, wrest)


V_KKN, V_B, V_W, V_K, V_R, V_V = range(6)


def _wkv_kernel(rk_ref, wk_ref, va_ref, s0_ref, gain_ref, bias_ref, rkp_ref,
                o_ref, sfin_ref, s_scr, vec_scr, row_scr, y_scr, cur_scr, *, tc, seq_major):
    j = pl.program_id(1)

    @pl.when(j == 0)
    def _():
        s_scr[...] = s0_ref[...]

    def load_pair(ref, t):
        x = ref[:, t] if seq_major else ref[t]
        xt = x.reshape(LANES, LANES).T
        return xt[:KA], xt[KA:]

    def prep(t, carry):
        dec, kkraw = load_pair(wk_ref, t)
        ss = jnp.sum(kkraw * kkraw, axis=0, keepdims=True)
        kkn = kkraw * (1.0 / jnp.maximum(jnp.sqrt(ss), 1e-12))
        v, a = load_pair(va_ref, t)
        r, kmod = load_pair(rk_ref, t)
        vec_scr[t, V_KKN] = kkn
        vec_scr[t, V_B] = kkn * a
        vec_scr[t, V_W] = dec
        vec_scr[t, V_K] = kmod
        vec_scr[t, V_R] = r
        vec_scr[t, V_V] = v
        rk = jnp.sum(r * kmod * rkp_ref[...], axis=0, keepdims=True)
        row_scr[t] = jnp.broadcast_to(rk, (SUBLANES, LANES))
        return carry

    def state_pass(t, carry):
        cur_scr[...] = vec_scr[t]
        skk = jnp.zeros((KA, LANES), F32)
        for k in range(KA):
            skk = skk + s_scr[k] * cur_scr[V_KKN, k:k + 1, :]
        vv = cur_scr[V_V]
        y = jnp.zeros((KA, LANES), F32)
        for k in range(KA):
            s_new = (s_scr[k] * cur_scr[V_W, k:k + 1, :] - skk * cur_scr[V_B, k:k + 1, :]
                     + vv * cur_scr[V_K, k:k + 1, :])
            s_scr[k] = s_new
            y = y + s_new * cur_scr[V_R, k:k + 1, :]
        y_scr[t] = y
        return carry

    def out_token(t):
        y = y_scr[t]
        m = jnp.mean(y, axis=0, keepdims=True)
        yc = y - m
        var = jnp.mean(yc * yc, axis=0, keepdims=True)
        yn = yc * lax.rsqrt(var + GN_EPS) * gain_ref[...] + bias_ref[...]
        return yn + row_scr[t, 0:1, :] * vec_scr[t, V_V]

    def finish(i, carry):
        o2 = jnp.concatenate([out_token(i), out_token(i + tc // 2)], axis=0)
        o = o2.T.reshape(LANES // HA, HA, LANES)
        if seq_major:
            o_ref[:, i] = o
        else:
            o_ref[i] = o
        return carry

    lax.fori_loop(0, tc, prep, 0, unroll=2)
    lax.fori_loop(0, tc, state_pass, 0)
    lax.fori_loop(0, tc // 2, finish, 0, unroll=2)

    @pl.when(j == pl.num_programs(1) - 1)
    def _():
        sfin_ref[...] = s_scr[...]


def _wkv(pairs, s0, gain_l, bias_l, rk_l, *, nseq, t, tc, seq_major):
    nsl = LANES // HA
    groups = nseq // nsl
    if seq_major:
        shape4, oshape4 = (nseq, t, HA, LANES), (nseq, t // 2, HA, LANES)
        vec_spec = pl.BlockSpec((nsl, tc, HA, LANES), lambda g, j: (g, j, 0, 0))
        out_spec = pl.BlockSpec((nsl, tc // 2, HA, LANES), lambda g, j: (g, j, 0, 0))
    else:
        shape4, oshape4 = (t, nseq, HA, LANES), (t // 2, nseq, HA, LANES)
        vec_spec = pl.BlockSpec((tc, nsl, HA, LANES), lambda g, j: (j, g, 0, 0))
        out_spec = pl.BlockSpec((tc // 2, nsl, HA, LANES), lambda g, j: (j, g, 0, 0))
    st_spec = pl.BlockSpec((KA, KA, LANES), lambda g, j: (0, 0, g))
    par_spec = pl.BlockSpec((KA, LANES), lambda g, j: (0, g))
    o4, s_fin = pl.pallas_call(
        functools.partial(_wkv_kernel, tc=tc, seq_major=seq_major),
        grid=(groups, t // tc),
        in_specs=[vec_spec] * 3 + [st_spec, par_spec, par_spec, par_spec],
        out_specs=[out_spec, st_spec],
        out_shape=[jax.ShapeDtypeStruct(oshape4, F32),
                   jax.ShapeDtypeStruct((KA, KA, nseq * HA), F32)],
        scratch_shapes=[pltpu.VMEM((KA, KA, LANES), F32), pltpu.VMEM((tc, 6, KA, LANES), F32),
                        pltpu.VMEM((tc, SUBLANES, LANES), F32), pltpu.VMEM((tc, KA, LANES), F32),
                        pltpu.VMEM((6, KA, LANES), F32)],
        compiler_params=pltpu.CompilerParams(dimension_semantics=("arbitrary", "arbitrary"),
                                             vmem_limit_bytes=VMEM_LIMIT),
        name="wkv",
    )(*[z.reshape(shape4) for z in pairs], s0, gain_l, bias_l, rk_l)
    return o4.reshape(nseq * t // 2, HA, LANES), s_fin


CH = 64
NPAIR = D // LANES


def _wkv_chunk_kernel(r_ref, k_ref, v_ref, wpre_ref, apre_ref, kk_ref, ka_ref, gain_ref, bias_ref, rk_ref,
                      tri_ref, ones_ref, o_ref, tfin_ref, t_scr, *, ct):
    j = pl.program_id(1)

    @pl.when(j == 0)
    def _():
        t_scr[...] = jnp.zeros_like(t_scr)

    lane = lax.broadcasted_iota(jnp.int32, (CH, LANES), 1)
    row = lax.broadcasted_iota(jnp.int32, (CH, LANES), 0)
    low = lane < KA
    col = jnp.bitwise_and(lane, KA - 1)
    strict = row > col
    incl = row >= col
    eye = (row == col).astype(F32)
    low2 = lax.broadcasted_iota(jnp.int32, (LANES, LANES), 1) < KA
    diag_blocks = (lax.broadcasted_iota(jnp.int32, (LANES, LANES), 0) < KA) == low2
    tri = tri_ref[...]
    ones_bd = ones_ref[...]

    def seg_sum(x, parts=2):
        return sum(jnp.dot(term, ones_bd, preferred_element_type=F32) for term in _split_bf16(x, parts))

    def bd(z):
        return jnp.concatenate([jnp.where(low, z, 0.0), jnp.where(low, 0.0, z)], axis=0)

    def abd(z):
        return jnp.concatenate([jnp.where(low, 0.0, z), jnp.where(low, z, 0.0)], axis=0)

    def seg_sum_pairs(xs, parts):
        tot = seg_sum(jnp.concatenate(xs, axis=0), parts)
        return [tot[m * CH:(m + 1) * CH] for m in range(NPAIR)]

    P = range(NPAIR)
    ln = [slice(m * LANES, (m + 1) * LANES) for m in P]

    def chunk_prep(c0, ctx):
        rows = slice(c0, c0 + CH)
        lw_all = _log_decay(wpre_ref[rows, :])
        cs_all = sum(jnp.dot(tri, term, preferred_element_type=F32) for term in _split_bf16(lw_all, 2))
        yield
        lw = [lw_all[:, ln[m]] for m in P]
        cs = [cs_all[:, ln[m]] for m in P]
        k_in = [k_ref[rows, ln[m]] for m in P]
        kkraw = [k_in[m] * kk_ref[:, ln[m]] for m in P]
        ss = seg_sum_pairs([kkraw[m] * kkraw[m] for m in P], 2)
        yield
        a = [_sigmoid(apre_ref[rows, ln[m]]) for m in P]
        kkn = [kkraw[m] / jnp.maximum(jnp.sqrt(ss[m]), 1e-12) for m in P]
        b = [kkn[m] * a[m] for m in P]
        r = [r_ref[rows, ln[m]] for m in P]
        k = [k_in[m] * (1.0 + (a[m] - 1.0) * ka_ref[:, ln[m]]) for m in P]
        v = [v_ref[rows, ln[m]] for m in P]
        cs_last = [cs[m][CH - 1:CH, :] for m in P]
        e_neg = [jnp.exp(-cs[m]) for m in P]
        a_t = [-kkn[m] * jnp.exp(cs[m] - lw[m]) for m in P]
        r_t = [r[m] * jnp.exp(cs[m]) for m in P]
        b_t = [b[m] * e_neg[m] for m in P]
        k_t = [k[m] * e_neg[m] for m in P]
        lhs = [jnp.concatenate([a_t[m], r_t[m]], axis=0) for m in P]
        out0 = [_bdot_nt(jnp.where(low2, lhs[m], 0.0), jnp.concatenate([b_t[m], k_t[m]], axis=0)) for m in P]
        yield
        out1 = [_bdot_nt(jnp.where(low2, 0.0, lhs[m]), jnp.concatenate([k_t[m], b_t[m]], axis=0)) for m in P]
        yield
        l_cat = [jnp.where(strict, jnp.where(low, out0[m][:CH], out1[m][:CH]), 0.0) for m in P]
        ak_cat = [jnp.where(strict, jnp.where(low, out1[m][:CH], out0[m][:CH]), 0.0) for m in P]
        rb_cat = [jnp.where(incl, jnp.where(low, out0[m][CH:], out1[m][CH:]), 0.0) for m in P]
        rk_cat = [jnp.where(incl, jnp.where(low, out1[m][CH:], out0[m][CH:]), 0.0) for m in P]
        rkv = seg_sum_pairs([r[m] * k[m] * rk_ref[:, ln[m]] for m in P], 1)
        e_end = [jnp.exp(cs_last[m] - cs[m]) for m in P]
        ctx.update(
            rows=rows, l_cat=l_cat, v=v, rkv=rkv,
            g_lhs=[jnp.concatenate([a_t[m], ak_cat[m]], axis=1) for m in P],
            y_lhs=[jnp.concatenate([r_t[m], rb_cat[m], rk_cat[m]], axis=1) for m in P],
            upd_lhs=[jnp.concatenate([b[m] * e_end[m], k[m] * e_end[m]], axis=0) for m in P],
            p_end=[jnp.exp(cs_last[m]) for m in P])
        yield

    def chunk_inverse(ctx):
        l_cat = ctx["l_cat"]
        x = [eye + l_cat[m] for m in P]
        lp = [_bdot(l_cat[m], bd(l_cat[m])) for m in P]
        yield
        for _ in range(4):
            both = [_bdot(jnp.concatenate([lp[m], x[m]], axis=0), bd(lp[m])) for m in P]
            yield
            x = [x[m] + both[m][CH:] for m in P]
            lp = [both[m][:CH] for m in P]
        ctx["x"] = [x[m] + _bdot(x[m], bd(lp[m])) for m in P]
        yield

    def chunk_state(ctx):
        x, v = ctx["x"], ctx["v"]
        t0 = [t_scr[m] for m in P]
        v_abd = [abd(v[m]) for m in P]
        g = [_bdot(ctx["g_lhs"][m], jnp.concatenate([t0[m], v_abd[m]], axis=0)) for m in P]
        yield
        u = [_bdot(x[m], bd(g[m])) for m in P]
        yield
        upd = [_bdot_tn(ctx["upd_lhs"][m], jnp.concatenate([u[m], v[m]], axis=0)) for m in P]
        yield
        for m in P:
            p_col = jnp.broadcast_to(ctx["p_end"][m], (LANES, LANES)).T
            t_scr[m] = t0[m] * p_col + jnp.where(diag_blocks, upd[m], 0.0)
        y = [_bdot(ctx["y_lhs"][m], jnp.concatenate([t0[m], bd(u[m]), v_abd[m]], axis=0)) for m in P]
        yield
        mean = seg_sum_pairs(y, 1)
        yield
        yc = [y[m] - mean[m] * (1.0 / KA) for m in P]
        var = seg_sum_pairs([yc[m] * yc[m] for m in P], 1)
        yield
        for m in P:
            yn = yc[m] * lax.rsqrt(var[m] * (1.0 / KA) + GN_EPS) * gain_ref[:, ln[m]] + bias_ref[:, ln[m]]
            o_ref[ctx["rows"], ln[m]] = yn + ctx["rkv"][m] * v[m]
        yield

    nchunk = ct // CH
    ctxs = [dict() for _ in range(nchunk)]
    for c in range(-2, nchunk):
        stages = []
        if c >= 0:
            stages.append(chunk_state(ctxs[c]))
        if 0 <= c + 1 < nchunk:
            stages.append(chunk_inverse(ctxs[c + 1]))
        if c + 2 < nchunk:
            stages.append(chunk_prep((c + 2) * CH, ctxs[c + 2]))
        _interleave(*stages)

    @pl.when(j == pl.num_programs(1) - 1)
    def _():
        tfin_ref[...] = t_scr[...]


def _wkv_chunked(vecs, k_k, k_a, gain, bias, rk, *, nseq, t, ct):
    tiles = t // ct
    row_spec = pl.BlockSpec((ct, D), lambda n, j: (n * tiles + j, 0))
    tri = jnp.asarray(np.tril(np.ones((CH, CH), np.float32)), BF16)
    head_of_lane = np.arange(LANES) // KA
    ones_bd = jnp.asarray((head_of_lane[:, None] == head_of_lane[None, :]).astype(np.float32), BF16)
    ya, t_fin = pl.pallas_call(
        functools.partial(_wkv_chunk_kernel, ct=ct),
        grid=(nseq, tiles),
        in_specs=[row_spec] * 5 + [_const_spec((1, D))] * 5 + [_const_spec((CH, CH)), _const_spec((LANES, LANES))],
        out_specs=[row_spec, pl.BlockSpec((None, NPAIR, LANES, LANES), lambda n, j: (n, 0, 0, 0))],
        out_shape=[jax.ShapeDtypeStruct((nseq * t, D), F32),
                   jax.ShapeDtypeStruct((nseq, NPAIR, LANES, LANES), F32)],
        scratch_shapes=[pltpu.VMEM((NPAIR, LANES, LANES), F32)],
        compiler_params=pltpu.CompilerParams(dimension_semantics=("arbitrary", "arbitrary"),
                                             vmem_limit_bytes=VMEM_LIMIT),
        name="wkv_chunk",
    )(*vecs, k_k, k_a, gain, bias, rk, tri, ones_bd)
    t6 = t_fin.reshape(nseq, NPAIR, 2, KA, 2, KA)
    diag = jnp.stack([t6[:, :, 0, :, 0, :], t6[:, :, 1, :, 1, :]], axis=2)
    return ya, diag.reshape(nseq, HA, KA, KA).transpose(0, 1, 3, 2)


def _log_gamma():
    return [math.log1p(-2.0 ** (-5.0 - h)) for h in range(HB)]


def _rope_tables(pos):
    half = DK // 2
    theta = 1.0 / (ROPE_BASE ** jnp.linspace(0.0, 1.0, half, dtype=F32))
    ang = pos[:, None] * theta[None, :]
    cos, sin = jnp.cos(ang), jnp.sin(ang)
    return jnp.concatenate([cos, cos], axis=1), jnp.concatenate([-sin, sin], axis=1)


def _decay_tables(chunk):
    lg = jnp.asarray(_log_gamma(), F32)
    idx = jnp.arange(chunk, dtype=F32)
    diff = idx[:, None] - idx[None, :]
    dmat = jnp.exp(jnp.maximum(diff, 0.0)[None] * lg[:, None, None]) * (diff >= 0)[None]
    cross = jnp.exp((idx[None, :] + 1.0) * lg[:, None])[:, :, None]
    kdec = jnp.exp((chunk - 1.0 - idx)[None, :] * lg[:, None])[:, :, None]
    sdec = jnp.exp(chunk * lg)
    return dmat.astype(F32), cross, kdec, sdec


def _rotary(x, cos2, sin2):
    return x * cos2 + pltpu.roll(x, DK // 2, axis=1) * sin2


def _ln_rows(y, eps):
    m = jnp.mean(y, axis=-1, keepdims=True)
    yc = y - m
    var = jnp.mean(yc * yc, axis=-1, keepdims=True)
    return yc * lax.rsqrt(var + eps)


def _ret_post_kernel(qk_ref, vb_ref, ga_ref, ya_ref, gb_ref, x_ref, gate_ref, cos_ref, sin_ref,
                     dmat_ref, cross_ref, kdec_ref, sdec_ref, wout_ref, lng_ref, lnb_ref,
                     y_ref, sfin_ref, s_scr, *, tt, alpha):
    j = pl.program_id(1)

    @pl.when(j == 0)
    def _():
        s_scr[...] = jnp.zeros_like(s_scr)

    H = range(HB)
    nchunk = tt // RET_CHUNK
    merged = [None] * nchunk

    def retention(c):
        rows = slice(c * RET_CHUNK, (c + 1) * RET_CHUNK)
        cos2, sin2 = cos_ref[rows, :], sin_ref[rows, :]
        qb = [_rotary(qk_ref[rows, h * DK:(h + 1) * DK].astype(F32), cos2, sin2).astype(BF16) for h in H]
        k = [_rotary(qk_ref[rows, HB * DK + h * DK:HB * DK + (h + 1) * DK].astype(F32), cos2, sin2) * (DK ** -0.5)
             for h in H]
        v = [vb_ref[rows, h * DV:(h + 1) * DV] for h in H]
        scores = [_bdot_nt(qb[h], k[h]) * dmat_ref[h] for h in H]
        yield
        s = [s_scr[h] for h in H]
        cross = [jnp.dot(qb[h], s[h].astype(BF16), preferred_element_type=F32) * cross_ref[h] for h in H]
        yield
        upd = [_bdot_tn(k[h] * kdec_ref[h], v[h]) for h in H]
        yield
        inner = [jnp.dot(scores[h].astype(BF16), v[h], preferred_element_type=F32) for h in H]
        ob = []
        for h in H:
            s_scr[h] = s[h] * sdec_ref[h] + upd[h]
            ob.append(_ln_rows(inner[h] + cross[h], LN_EPS))
        merged[c] = (ga_ref[rows, :].astype(F32) * ya_ref[rows, :]
                     + gb_ref[rows, :].astype(F32) * jnp.concatenate(ob, axis=1)).astype(BF16)
        yield

    def project(c):
        rows = slice(c * RET_CHUNK, (c + 1) * RET_CHUNK)
        cols = []
        for n0 in range(0, D, 2 * LANES):
            cols.append(jnp.dot(merged[c], wout_ref[:, n0:n0 + 2 * LANES], preferred_element_type=F32))
            yield
        z = alpha * x_ref[rows, :] + gate_ref[...] * jnp.concatenate(cols, axis=1)
        y_ref[rows, :] = _ln_rows(z, LN_EPS) * lng_ref[...] + lnb_ref[...]
        yield

    for c in range(-1, nchunk):
        _interleave(*([project(c)] if c >= 0 else []), *([retention(c + 1)] if c + 1 < nchunk else []))

    @pl.when(j == pl.num_programs(1) - 1)
    def _():
        sfin_ref[...] = s_scr[...]


def _ret_post(qk, vb, ga, ya, gb, x2, gate, wout, ln_gain, ln_bias, *, nseq, t, tt, alpha):
    tiles = t // tt
    cos2, sin2 = _rope_tables(jnp.arange(t, dtype=F32) + 0)
    dmat, cross, kdec, sdec = _decay_tables(RET_CHUNK)
    row_spec = pl.BlockSpec((tt, D), lambda n, j: (n * tiles + j, 0))
    tab_spec = pl.BlockSpec((tt, DK), lambda n, j: (j, 0))
    return pl.pallas_call(
        functools.partial(_ret_post_kernel, tt=tt, alpha=alpha),
        grid=(nseq, tiles),
        in_specs=[row_spec] * 6 + [pl.BlockSpec((None, 1, D), lambda n, j: (n, 0, 0)), tab_spec, tab_spec,
                                   _const_spec((HB, RET_CHUNK, RET_CHUNK)), _const_spec((HB, RET_CHUNK, 1)),
                                   _const_spec((HB, RET_CHUNK, 1)), pl.BlockSpec(memory_space=pltpu.SMEM),
                                   _const_spec((D, D)), _const_spec((1, D)), _const_spec((1, D))],
        out_specs=[row_spec, pl.BlockSpec((None, HB, DK, DV), lambda n, j: (n, 0, 0, 0))],
        out_shape=[jax.ShapeDtypeStruct(x2.shape, F32),
                   jax.ShapeDtypeStruct((nseq, HB, DK, DV), F32)],
        scratch_shapes=[pltpu.VMEM((HB, DK, DV), F32)],
        compiler_params=pltpu.CompilerParams(dimension_semantics=("arbitrary", "arbitrary"),
                                             vmem_limit_bytes=VMEM_LIMIT),
        name="ret_post",
    )(qk, vb, ga, ya, gb, x2, gate, cos2, sin2, dmat, cross, kdec, sdec, wout, ln_gain, ln_bias)


def _ret_sample_kernel(qk_ref, vb_ref, cos_ref, sin_ref, s_ref, dmat_ref, cross_ref, kdec_ref, sdec_ref,
                       o_ref, snew_ref, *, nb):
    cos2, sin2 = cos_ref[...], sin_ref[...]
    for n in range(nb):
        for h in range(HB):
            q = _rotary(qk_ref[:, n, h * DK:(h + 1) * DK], cos2, sin2)
            k = _rotary(qk_ref[:, n, HB * DK + h * DK:HB * DK + (h + 1) * DK], cos2, sin2) * (DK ** -0.5)
            v = vb_ref[:, n, h * DV:(h + 1) * DV].astype(BF16)
            qb = q.astype(BF16)
            scores = lax.dot_general(qb, k.astype(BF16), (((1,), (1,)), ((), ())),
                                     preferred_element_type=F32) * dmat_ref[h]
            s = s_ref[n, h]
            y = (jnp.dot(scores.astype(BF16), v, preferred_element_type=F32)
                 + jnp.dot(qb, s.astype(BF16), preferred_element_type=F32) * cross_ref[h])
            kd = (k * kdec_ref[h]).astype(BF16)
            snew_ref[n, h] = s * sdec_ref[h] + lax.dot_general(kd, v, (((0,), (0,)), ((), ())),
                                                               preferred_element_type=F32)
            o_ref[:, n, h * DV:(h + 1) * DV] = _ln_rows(y, LN_EPS)


def _ret_sample(qk3, vb3, s0, pos0, *, nb):
    t, nseq, _ = qk3.shape
    cos2, sin2 = _rope_tables(jnp.arange(t, dtype=F32) + pos0)
    dmat, cross, kdec, sdec = _decay_tables(t)
    row_spec = pl.BlockSpec((t, nb, D), lambda i: (0, i, 0))
    st_spec = pl.BlockSpec((nb, HB, DK, DV), lambda i: (i, 0, 0, 0))
    return pl.pallas_call(
        functools.partial(_ret_sample_kernel, nb=nb),
        grid=(nseq // nb,),
        in_specs=[row_spec, row_spec, _const_spec((t, DK)), _const_spec((t, DK)), st_spec,
                  _const_spec((HB, t, t)), _const_spec((HB, t, 1)), _const_spec((HB, t, 1)),
                  pl.BlockSpec(memory_space=pltpu.SMEM)],
        out_specs=[row_spec, st_spec],
        out_shape=[jax.ShapeDtypeStruct(qk3.shape, F32), jax.ShapeDtypeStruct(s0.shape, F32)],
        compiler_params=pltpu.CompilerParams(dimension_semantics=("arbitrary",),
                                             vmem_limit_bytes=VMEM_LIMIT),
        name="ret_sample",
    )(qk3, vb3, cos2, sin2, s0, dmat, cross, kdec, sdec)


def _unpair_tokens(ya_ref, ya_scr, pair_rows):
    half = ya_ref.shape[0]
    groups = half // pair_rows
    low = lax.broadcasted_iota(jnp.int32, (half, LANES), 1) < KA
    for m in range(D // LANES):
        ha = ya_ref[:, 2 * m, :]
        hb = ya_ref[:, 2 * m + 1, :]
        lo = jnp.where(low, ha, pltpu.roll(hb, KA, axis=1))
        hi = jnp.where(low, pltpu.roll(ha, KA, axis=1), hb)
        ya_scr[:, 0, :, m * LANES:(m + 1) * LANES] = lo.reshape(groups, pair_rows, LANES)
        ya_scr[:, 1, :, m * LANES:(m + 1) * LANES] = hi.reshape(groups, pair_rows, LANES)
    return ya_scr[...].reshape(2 * half, D)


def _post_kernel(ga_ref, ya_ref, gb_ref, ob_ref, x_ref, gate_ref, wout_ref, lng_ref, lnb_ref, y_ref,
                 ya_scr, *, gs, alpha, pair_rows):
    merged = ga_ref[...] * _unpair_tokens(ya_ref, ya_scr, pair_rows) + gb_ref[...] * ob_ref[...]
    sub = jnp.dot(merged.astype(BF16), wout_ref[...], preferred_element_type=F32)
    x = x_ref[...]
    gate = gate_ref[...]
    if gs == 1:
        z = alpha * x + gate * sub
    else:
        tt = x.shape[0]
        z = (alpha * x.reshape(tt // gs, gs, D) + gate[None] * sub.reshape(tt // gs, gs, D)).reshape(tt, D)
    y_ref[...] = _ln_rows(z, LN_EPS) * lng_ref[...] + lnb_ref[...]


def _post(ga, ya, gb, ob, x2, gate, wout, ln_gain, ln_bias, *, nseq_grid, tiles, tt, gs, alpha, pair_rows):
    row_spec = pl.BlockSpec((tt, D), lambda n, j: (n * tiles + j, 0))
    ya_spec = pl.BlockSpec((tt // 2, HA, LANES), lambda n, j: (n * tiles + j, 0, 0))
    scratch = [pltpu.VMEM((tt // 2 // pair_rows, 2, pair_rows, D), F32)]
    return pl.pallas_call(
        functools.partial(_post_kernel, gs=gs, alpha=alpha, pair_rows=pair_rows),
        grid=(nseq_grid, tiles),
        in_specs=[row_spec, ya_spec, row_spec, row_spec, row_spec,
                  pl.BlockSpec((None, gs, D), lambda n, j: (n, 0, 0)),
                  _const_spec((D, D)), _const_spec((1, D)), _const_spec((1, D))],
        out_specs=row_spec,
        out_shape=jax.ShapeDtypeStruct(x2.shape, F32),
        scratch_shapes=scratch,
        compiler_params=pltpu.CompilerParams(dimension_semantics=("arbitrary", "arbitrary"),
                                             vmem_limit_bytes=VMEM_LIMIT),
        name="post",
    )(ga, ya, gb, ob, x2, gate, wout, ln_gain, ln_bias)


def _head_param_lanes(p, nseq):
    return jnp.tile(p.reshape(HA, KA).T, (1, nseq))


def _layer(x2, ada, first_prev, s_wkv, s_ret, pos0, weights, *, nseq, t, seq_major, tt, tc, tp, depth):
    (wsh, wrest, pre_params, gn_gain, gn_bias, r_k, wout, ln_gain, ln_bias) = weights
    if seq_major:
        nseq_grid, gs, tiles = nseq, 1, t // tt
    else:
        nseq_grid, gs, tiles = 1, nseq, (t * nseq) // tt
    shift_c = ada[:, :D].reshape(nseq_grid, gs, D)
    scale_c = ada[:, D:2 * D].reshape(nseq_grid, gs, D)
    gate_c = ada[:, 2 * D:].reshape(nseq_grid, gs, D)
    cfg = dict(nseq_grid=nseq_grid, tiles=tiles, tt=tt, gs=gs)

    *vecs, new_shift = _pre_a(x2, scale_c, shift_c, first_prev.reshape(nseq_grid, gs, SHIFT_W), wsh, pre_params,
                              pair_out=not seq_major, **cfg)
    ga, gb, qk, vb = _pre_b(x2, scale_c, shift_c, wrest, out_dtype=BF16 if seq_major else F32, **cfg)

    if seq_major:
        row = lambda p: p.reshape(1, D)
        ya, new_wkv = _wkv_chunked(vecs, pre_params[3], pre_params[4], row(gn_gain), row(gn_bias), row(r_k),
                                   nseq=nseq, t=t, ct=tc)
        pair_rows = None
    else:
        s0 = s_wkv.transpose(3, 2, 0, 1).reshape(KA, KA, nseq * HA)
        ya, s_fin = _wkv(vecs, s0, _head_param_lanes(gn_gain, nseq),
                         _head_param_lanes(gn_bias, nseq), _head_param_lanes(r_k.reshape(-1), nseq),
                         nseq=nseq, t=t, tc=tc, seq_major=False)
        pair_rows = (tc // 2) * gs
        new_wkv = s_fin.reshape(KA, KA, nseq, HA).transpose(2, 3, 1, 0)

    alpha = (2.0 * depth) ** 0.25
    if seq_major:
        y, new_ret = _ret_post(qk, vb, ga, ya, gb, x2, gate_c, wout, ln_gain, ln_bias,
                               nseq=nseq, t=t, tt=tp, alpha=alpha)
    else:
        ob3, new_ret = _ret_sample(qk.reshape(t, nseq, D), vb.reshape(t, nseq, D), s_ret, pos0, nb=8)
        y = _post(ga, ya, gb, ob3.reshape(t * nseq, D), x2, gate_c, wout, ln_gain, ln_bias, alpha=alpha,
                  pair_rows=pair_rows, **cfg)
    return y, new_shift.reshape(nseq, SHIFT_W), new_wkv, new_ret


def kernel(x_prompt, x_sample, c_prompt, c_sample, state_shift, state_wkv, state_ret, w_ada, b_ada, w_in,
           mu_shift, w0, w_decay_up, a0, w_icl_up, k_k, k_a, r_k, gn_a_gain, gn_a_bias, w_out, ln_gain, ln_bias):
    depth = w_in.shape[0]
    nb, t_p, _ = x_prompt.shape
    ns, t_s, _ = x_sample.shape
    past_len = 16384
    h_p = x_prompt.reshape(nb * t_p, D)
    h_s = x_sample.transpose(1, 0, 2).reshape(t_s * ns, D)
    outs = [[] for _ in range(6)]
    zero_lora = jnp.zeros((LORA, D), F32)
    for l in range(depth):
        row = lambda p: p[l].reshape(1, -1)
        wsh, wrest = _cast_w_in(w_in, l)
        wd_pad = jnp.concatenate([w_decay_up[l], zero_lora], axis=0).astype(BF16)
        wa_pad = jnp.concatenate([zero_lora, w_icl_up[l]], axis=0).astype(BF16)
        pre_params = (row(mu_shift), row(w0), row(a0), row(k_k), row(k_a), wd_pad, wa_pad)
        weights = (wsh, wrest, pre_params, gn_a_gain[l], gn_a_bias[l], r_k[l], w_out[l].astype(BF16),
                   row(ln_gain), row(ln_bias))
        ada = _ada(jnp.concatenate([c_prompt, c_sample], axis=0), w_ada[l], b_ada[l])
        h_p, s1, s2, s3 = _layer(
            h_p, ada[:nb], jnp.zeros((nb, SHIFT_W), F32), None, None, 0,
            weights, nseq=nb, t=t_p, seq_major=True, depth=depth, **PROMPT_TILES)
        h_s, t1, t2, t3 = _layer(
            h_s, ada[nb:], state_shift[l], state_wkv[l], state_ret[l], past_len,
            weights, nseq=ns, t=t_s, seq_major=False, tt=t_s * ns, tc=t_s, tp=t_s * ns, depth=depth)
        for lst, val in zip(outs, (s1, s2, s3, t1, t2, t3)):
            lst.append(val)
    y_p = h_p.reshape(nb, t_p, D)
    y_s = h_s.reshape(t_s, ns, D).transpose(1, 0, 2)
    return (y_p, y_s) + tuple(jnp.stack(o) for o in outs)
```

```python
import functools
import math

import numpy as np
import jax
import jax.numpy as jnp
from jax import lax
from jax.experimental import pallas as pl
from jax.experimental.pallas import tpu as pltpu

F32 = jnp.float32
BF16 = jnp.bfloat16

D = 1024
HA = 16
KA = 64
LORA = 64
GN_EPS = 64e-5
HB = 4
DK = 128
DV = 256
RET_CHUNK = 128
ROPE_BASE = 10000.0
LN_EPS = 1e-5
SHIFT_W = 3 * D + 2 * LORA
REST_W = 6 * D
LANES = 128
SUBLANES = 8
VMEM_LIMIT = 56 * 1024 * 1024
PROMPT_TILES = dict(tt=256, tc=256, wkv_seqs=2, tp=512)


def _sigmoid(x):
    return 1.0 / (1.0 + jnp.exp(-x))


def _log_decay(w_pre):
    zw = -w_pre
    softplus = jnp.maximum(zw, 0.0) + jnp.log(1.0 + jnp.exp(-jnp.abs(zw)))
    return -jnp.exp(-softplus - 0.5)


def _bdot(a, b):
    return jnp.dot(a.astype(BF16), b.astype(BF16), preferred_element_type=F32)


def _bdot_nt(a, b):
    return lax.dot_general(a.astype(BF16), b.astype(BF16), (((1,), (1,)), ((), ())), preferred_element_type=F32)


def _bdot_tn(a, b):
    return lax.dot_general(a.astype(BF16), b.astype(BF16), (((0,), (0,)), ((), ())), preferred_element_type=F32)


def _split_bf16(x, parts):
    out = []
    for _ in range(parts):
        term = x.astype(BF16)
        out.append(term)
        x = x - term.astype(F32)
    return out


def _interleave(*gens):
    live = list(gens)
    while live:
        for gen in list(live):
            try:
                next(gen)
            except StopIteration:
                live.remove(gen)


def _const_spec(shape):
    nd = len(shape)
    return pl.BlockSpec(shape, lambda *_: (0,) * nd, pipeline_mode=pl.Buffered(1))


def _cast_w_in_kernel(w_ref, sh_ref, rest_ref):
    sh_ref[...] = w_ref[:, :SHIFT_W].astype(BF16)
    rest_ref[...] = w_ref[:, SHIFT_W:].astype(BF16)


def _cast_w_in(w_in, layer):
    rb = 256
    return pl.pallas_call(
        _cast_w_in_kernel,
        grid=(D // rb,),
        in_specs=[pl.BlockSpec((None, rb, SHIFT_W + REST_W), lambda i: (layer, i, 0))],
        out_specs=[pl.BlockSpec((rb, SHIFT_W), lambda i: (i, 0)), pl.BlockSpec((rb, REST_W), lambda i: (i, 0))],
        out_shape=[jax.ShapeDtypeStruct((D, SHIFT_W), BF16), jax.ShapeDtypeStruct((D, REST_W), BF16)],
        compiler_params=pltpu.CompilerParams(dimension_semantics=("arbitrary",)),
        name="cast_w_in",
    )(w_in)


def _ada_kernel(c_ref, w_ref, b_ref, o_ref):
    c = c_ref[...]
    s = c * _sigmoid(c)
    o_ref[...] = jnp.dot(s, w_ref[...], preferred_element_type=F32,
                         precision=lax.Precision.HIGHEST) + b_ref[...]


def _ada(c_all, w_ada, b_ada):
    n = c_all.shape[0]
    return pl.pallas_call(
        _ada_kernel,
        grid=(3,),
        in_specs=[pl.BlockSpec((n, D), lambda j: (0, 0)),
                  pl.BlockSpec((D, D), lambda j: (0, j)),
                  pl.BlockSpec((1, D), lambda j: (0, j))],
        out_specs=pl.BlockSpec((n, D), lambda j: (0, j)),
        out_shape=jax.ShapeDtypeStruct((n, 3 * D), F32),
        compiler_params=pltpu.CompilerParams(dimension_semantics=("arbitrary",)),
        name="ada",
    )(c_all, w_ada, b_ada.reshape(1, 3 * D))


def _modulate(x, scale, shift, gs):
    if gs == 1:
        return x * (1.0 + scale) + shift
    tt = x.shape[0]
    x3 = x.reshape(tt // gs, gs, D)
    return (x3 * (1.0 + scale)[None] + shift[None]).reshape(tt, D)


def _put_head_pairs(out_ref, scr, x, y):
    tt = x.shape[0]
    low = lax.broadcasted_iota(jnp.int32, (tt, LANES), 1) < KA
    for m in range(D // LANES):
        xc = x[:, m * LANES:(m + 1) * LANES]
        yc = y[:, m * LANES:(m + 1) * LANES]
        even = jnp.where(low, xc, pltpu.roll(yc, KA, axis=1))
        odd = jnp.where(low, pltpu.roll(xc, KA, axis=1), yc)
        for r in range(tt // SUBLANES):
            rows = slice(r * SUBLANES, (r + 1) * SUBLANES)
            scr[pl.ds((r * HA + 2 * m) * SUBLANES, SUBLANES), :] = even[rows]
            scr[pl.ds((r * HA + 2 * m + 1) * SUBLANES, SUBLANES), :] = odd[rows]

    def row_block(r, carry):
        for s in range(SUBLANES):
            out_ref[r * SUBLANES + s] = scr[pl.ds(r * (HA * SUBLANES) + s, HA, stride=SUBLANES), :]
        return carry

    lax.fori_loop(0, tt // SUBLANES, row_block, 0)


def _pre_kernel(x_ref, scale_ref, shift_ref, prev_ref, wsh_ref, wrest_ref, mu_ref, w0_ref, a0_ref,
                kk_ref, ka_ref, wd_ref, wa_ref, *rest, gs, tt, pad, pair_out):
    if pair_out:
        rk_out, wk_out, va_out, ga_out, gb_out, qk_out, vb_out, ns_out, p_scr, pair_scr = rest
    else:
        r_out, k_out, v_out, wpre_out, apre_out, ga_out, gb_out, qk_out, vb_out, ns_out, p_scr = rest
    j = pl.program_id(1)
    nj = pl.num_programs(1)
    lo = pad - gs

    @pl.when(j == 0)
    def _():
        p_scr[lo:pad, :] = prev_ref[...]

    u = _modulate(x_ref[...], scale_ref[...], shift_ref[...], gs).astype(BF16)

    def mixed(c0, c1):
        p_scr[pad:pad + tt, c0:c1] = jnp.dot(u, wsh_ref[:, c0:c1], preferred_element_type=F32)
        p = p_scr[pad:pad + tt, c0:c1]
        prev = p_scr[lo:lo + tt, c0:c1]
        return p + (prev - p) * mu_ref[:, c0:c1]

    lora_in = mixed(3 * D, SHIFT_W)
    lora_w = _bdot(jnp.tanh(lora_in), wd_ref[...])
    lora_a = _bdot(lora_in, wa_ref[...])
    w_pre = w0_ref[...] + lora_w
    a_pre = a0_ref[...] + lora_a
    k = mixed(D, 2 * D)
    if pair_out:
        a = _sigmoid(a_pre)
        _put_head_pairs(rk_out, pair_scr.at[0], mixed(0, D), k * (1.0 + (a - 1.0) * ka_ref[...]))
        _put_head_pairs(wk_out, pair_scr.at[1], jnp.exp(_log_decay(w_pre)), k * kk_ref[...])
        _put_head_pairs(va_out, pair_scr.at[2], mixed(2 * D, 3 * D), a)
    else:
        wpre_out[...] = w_pre
        apre_out[...] = a_pre
        k_out[...] = k
        r_out[...] = mixed(0, D)
        v_out[...] = mixed(2 * D, 3 * D)

    @pl.when(j == nj - 1)
    def _():
        ns_out[...] = p_scr[pad + tt - gs:pad + tt, :]

    p_scr[lo:pad, :] = p_scr[pad + tt - gs:pad + tt, :]

    def proj(c0, c1):
        return jnp.dot(u, wrest_ref[:, c0:c1], preferred_element_type=F32)

    z_a = proj(0, D)
    g_a = proj(4 * D, 5 * D)
    ga_out[...] = (_sigmoid(g_a) * (z_a * _sigmoid(z_a))).astype(ga_out.dtype)
    z_b = proj(3 * D, 4 * D)
    g_b = proj(5 * D, 6 * D)
    gb_out[...] = (_sigmoid(g_b) * (z_b * _sigmoid(z_b))).astype(gb_out.dtype)
    qk_out[...] = proj(D, 2 * D).astype(qk_out.dtype)
    vb_out[...] = proj(2 * D, 3 * D).astype(vb_out.dtype)


def _pre(x2, scale, shift, first_prev, wsh, wrest, params, *, nseq_grid, tiles, tt, gs, pair_out, gate_dtype):
    rows = x2.shape[0]
    pad = max(SUBLANES, gs)
    mu, w0, a0, k_k, k_a, wd_pad, wa_pad = params
    row_spec = pl.BlockSpec((tt, D), lambda n, j: (n * tiles + j, 0))
    seq_spec = pl.BlockSpec((None, gs, D), lambda n, j: (n, 0, 0))
    scratch = [pltpu.VMEM((pad + tt, SHIFT_W), F32)]
    if pair_out:
        vec_specs = [pl.BlockSpec((tt, HA, LANES), lambda n, j: (n * tiles + j, 0, 0))] * 3
        vec_sds = [jax.ShapeDtypeStruct((rows, HA, LANES), F32)] * 3
        scratch.append(pltpu.VMEM((3, tt * HA, LANES), F32))
    else:
        vec_specs = [row_spec] * 5
        vec_sds = [jax.ShapeDtypeStruct((rows, D), F32)] * 5
    gate_sds = [jax.ShapeDtypeStruct((rows, D), gate_dtype)] * 4
    kern = functools.partial(_pre_kernel, gs=gs, tt=tt, pad=pad, pair_out=pair_out)
    return pl.pallas_call(
        kern,
        grid=(nseq_grid, tiles),
        in_specs=[row_spec, seq_spec, seq_spec,
                  pl.BlockSpec((None, gs, SHIFT_W), lambda n, j: (n, 0, 0)),
                  _const_spec((D, SHIFT_W)), _const_spec((D, REST_W)), _const_spec((1, SHIFT_W)),
                  _const_spec((1, D)), _const_spec((1, D)), _const_spec((1, D)), _const_spec((1, D)),
                  _const_spec((LANES, D)), _const_spec((LANES, D))],
        out_specs=vec_specs + [row_spec] * 4 + [pl.BlockSpec((None, gs, SHIFT_W), lambda n, j: (n, 0, 0))],
        out_shape=vec_sds + gate_sds + [jax.ShapeDtypeStruct((nseq_grid, gs, SHIFT_W), F32)],
        scratch_shapes=scratch,
        compiler_params=pltpu.CompilerParams(dimension_semantics=("arbitrary", "arbitrary"),
                                             vmem_limit_bytes=VMEM_LIMIT),
        name="pre",
    )(x2, scale, shift, first_prev, wsh, wrest, mu, w0, a0, k_k, k_a, wd_pad, wa_pad)


V_KKN, V_B, V_W, V_K, V_R, V_V = range(6)


def _wkv_kernel(rk_ref, wk_ref, va_ref, s0_ref, gain_ref, bias_ref, rkp_ref,
                o_ref, sfin_ref, s_scr, vec_scr, row_scr, y_scr, cur_scr, *, tc, seq_major):
    j = pl.program_id(1)

    @pl.when(j == 0)
    def _():
        s_scr[...] = s0_ref[...]

    def load_pair(ref, t):
        x = ref[:, t] if seq_major else ref[t]
        xt = x.reshape(LANES, LANES).T
        return xt[:KA], xt[KA:]

    def prep(t, carry):
        dec, kkraw = load_pair(wk_ref, t)
        ss = jnp.sum(kkraw * kkraw, axis=0, keepdims=True)
        kkn = kkraw * (1.0 / jnp.maximum(jnp.sqrt(ss), 1e-12))
        v, a = load_pair(va_ref, t)
        r, kmod = load_pair(rk_ref, t)
        vec_scr[t, V_KKN] = kkn
        vec_scr[t, V_B] = kkn * a
        vec_scr[t, V_W] = dec
        vec_scr[t, V_K] = kmod
        vec_scr[t, V_R] = r
        vec_scr[t, V_V] = v
        rk = jnp.sum(r * kmod * rkp_ref[...], axis=0, keepdims=True)
        row_scr[t] = jnp.broadcast_to(rk, (SUBLANES, LANES))
        return carry

    def state_pass(t, carry):
        cur_scr[...] = vec_scr[t]
        skk = jnp.zeros((KA, LANES), F32)
        for k in range(KA):
            skk = skk + s_scr[k] * cur_scr[V_KKN, k:k + 1, :]
        vv = cur_scr[V_V]
        y = jnp.zeros((KA, LANES), F32)
        for k in range(KA):
            s_new = (s_scr[k] * cur_scr[V_W, k:k + 1, :] - skk * cur_scr[V_B, k:k + 1, :]
                     + vv * cur_scr[V_K, k:k + 1, :])
            s_scr[k] = s_new
            y = y + s_new * cur_scr[V_R, k:k + 1, :]
        y_scr[t] = y
        return carry

    def out_token(t):
        y = y_scr[t]
        m = jnp.mean(y, axis=0, keepdims=True)
        yc = y - m
        var = jnp.mean(yc * yc, axis=0, keepdims=True)
        yn = yc * lax.rsqrt(var + GN_EPS) * gain_ref[...] + bias_ref[...]
        return yn + row_scr[t, 0:1, :] * vec_scr[t, V_V]

    def finish(i, carry):
        o2 = jnp.concatenate([out_token(i), out_token(i + tc // 2)], axis=0)
        o = o2.T.reshape(LANES // HA, HA, LANES)
        if seq_major:
            o_ref[:, i] = o
        else:
            o_ref[i] = o
        return carry

    lax.fori_loop(0, tc, prep, 0, unroll=2)
    lax.fori_loop(0, tc, state_pass, 0)
    lax.fori_loop(0, tc // 2, finish, 0, unroll=2)

    @pl.when(j == pl.num_programs(1) - 1)
    def _():
        sfin_ref[...] = s_scr[...]


def _wkv(pairs, s0, gain_l, bias_l, rk_l, *, nseq, t, tc, seq_major):
    nsl = LANES // HA
    groups = nseq // nsl
    if seq_major:
        shape4, oshape4 = (nseq, t, HA, LANES), (nseq, t // 2, HA, LANES)
        vec_spec = pl.BlockSpec((nsl, tc, HA, LANES), lambda g, j: (g, j, 0, 0))
        out_spec = pl.BlockSpec((nsl, tc // 2, HA, LANES), lambda g, j: (g, j, 0, 0))
    else:
        shape4, oshape4 = (t, nseq, HA, LANES), (t // 2, nseq, HA, LANES)
        vec_spec = pl.BlockSpec((tc, nsl, HA, LANES), lambda g, j: (j, g, 0, 0))
        out_spec = pl.BlockSpec((tc // 2, nsl, HA, LANES), lambda g, j: (j, g, 0, 0))
    st_spec = pl.BlockSpec((KA, KA, LANES), lambda g, j: (0, 0, g))
    par_spec = pl.BlockSpec((KA, LANES), lambda g, j: (0, g))
    o4, s_fin = pl.pallas_call(
        functools.partial(_wkv_kernel, tc=tc, seq_major=seq_major),
        grid=(groups, t // tc),
        in_specs=[vec_spec] * 3 + [st_spec, par_spec, par_spec, par_spec],
        out_specs=[out_spec, st_spec],
        out_shape=[jax.ShapeDtypeStruct(oshape4, F32),
                   jax.ShapeDtypeStruct((KA, KA, nseq * HA), F32)],
        scratch_shapes=[pltpu.VMEM((KA, KA, LANES), F32), pltpu.VMEM((tc, 6, KA, LANES), F32),
                        pltpu.VMEM((tc, SUBLANES, LANES), F32), pltpu.VMEM((tc, KA, LANES), F32),
                        pltpu.VMEM((6, KA, LANES), F32)],
        compiler_params=pltpu.CompilerParams(dimension_semantics=("arbitrary", "arbitrary"),
                                             vmem_limit_bytes=VMEM_LIMIT),
        name="wkv",
    )(*[z.reshape(shape4) for z in pairs], s0, gain_l, bias_l, rk_l)
    return o4.reshape(nseq * t // 2, HA, LANES), s_fin


CH = 64
NPAIR = D // LANES


def _wkv_chunk_kernel(r_ref, k_ref, v_ref, wpre_ref, apre_ref, kk_ref, ka_ref, gain_ref, bias_ref, rk_ref,
                      tri_ref, ones_ref, o_ref, tfin_ref, t_scr, *, ct):
    j = pl.program_id(1)

    @pl.when(j == 0)
    def _():
        t_scr[...] = jnp.zeros_like(t_scr)

    lane = lax.broadcasted_iota(jnp.int32, (CH, LANES), 1)
    row = lax.broadcasted_iota(jnp.int32, (CH, LANES), 0)
    low = lane < KA
    col = jnp.bitwise_and(lane, KA - 1)
    strict = row > col
    incl = row >= col
    eye = (row == col).astype(F32)
    low2 = lax.broadcasted_iota(jnp.int32, (LANES, LANES), 1) < KA
    diag_blocks = (lax.broadcasted_iota(jnp.int32, (LANES, LANES), 0) < KA) == low2
    tri = tri_ref[...]
    ones_bd = ones_ref[...]

    def seg_sum(x, parts=2):
        return sum(jnp.dot(term, ones_bd, preferred_element_type=F32) for term in _split_bf16(x, parts))

    def bd(z):
        return jnp.concatenate([jnp.where(low, z, 0.0), jnp.where(low, 0.0, z)], axis=0)

    def abd(z):
        return jnp.concatenate([jnp.where(low, 0.0, z), jnp.where(low, z, 0.0)], axis=0)

    def seg_sum_pairs(xs, parts):
        tot = seg_sum(jnp.concatenate(xs, axis=0), parts)
        return [tot[i * CH:(i + 1) * CH] for i in range(len(xs))]

    nbatch = r_ref.shape[0]
    B = range(nbatch)
    P = range(nbatch * NPAIR)
    seq = [i // NPAIR for i in P]
    pair = [i % NPAIR for i in P]
    ln = [slice(pair[i] * LANES, (pair[i] + 1) * LANES) for i in P]

    def chunk_prep(c0, ctx):
        rows = slice(c0, c0 + CH)
        lw_all = [_log_decay(wpre_ref[b, rows, :]) for b in B]
        cs_all = [sum(jnp.dot(tri, term, preferred_element_type=F32) for term in _split_bf16(lw_all[b], 2))
                  for b in B]
        yield
        lw = [lw_all[seq[m]][:, ln[m]] for m in P]
        cs = [cs_all[seq[m]][:, ln[m]] for m in P]
        k_in = [k_ref[seq[m], rows, ln[m]] for m in P]
        kkraw = [k_in[m] * kk_ref[:, ln[m]] for m in P]
        ss = seg_sum_pairs([kkraw[m] * kkraw[m] for m in P], 2)
        yield
        a = [_sigmoid(apre_ref[seq[m], rows, ln[m]]) for m in P]
        kkn = [kkraw[m] / jnp.maximum(jnp.sqrt(ss[m]), 1e-12) for m in P]
        b = [kkn[m] * a[m] for m in P]
        r = [r_ref[seq[m], rows, ln[m]] for m in P]
        k = [k_in[m] * (1.0 + (a[m] - 1.0) * ka_ref[:, ln[m]]) for m in P]
        v = [v_ref[seq[m], rows, ln[m]] for m in P]
        cs_last = [cs[m][CH - 1:CH, :] for m in P]
        e_neg = [jnp.exp(-cs[m]) for m in P]
        a_t = [-kkn[m] * jnp.exp(cs[m] - lw[m]) for m in P]
        r_t = [r[m] * jnp.exp(cs[m]) for m in P]
        b_t = [b[m] * e_neg[m] for m in P]
        k_t = [k[m] * e_neg[m] for m in P]
        lhs = [jnp.concatenate([a_t[m], r_t[m]], axis=0) for m in P]
        out0 = [_bdot_nt(jnp.where(low2, lhs[m], 0.0), jnp.concatenate([b_t[m], k_t[m]], axis=0)) for m in P]
        yield
        out1 = [_bdot_nt(jnp.where(low2, 0.0, lhs[m]), jnp.concatenate([k_t[m], b_t[m]], axis=0)) for m in P]
        yield
        l_cat = [jnp.where(strict, jnp.where(low, out0[m][:CH], out1[m][:CH]), 0.0) for m in P]
        ak_cat = [jnp.where(strict, jnp.where(low, out1[m][:CH], out0[m][:CH]), 0.0) for m in P]
        rb_cat = [jnp.where(incl, jnp.where(low, out0[m][CH:], out1[m][CH:]), 0.0) for m in P]
        rk_cat = [jnp.where(incl, jnp.where(low, out1[m][CH:], out0[m][CH:]), 0.0) for m in P]
        rkv = seg_sum_pairs([r[m] * k[m] * rk_ref[:, ln[m]] for m in P], 1)
        e_end = [jnp.exp(cs_last[m] - cs[m]) for m in P]
        ctx.update(
            rows=rows, l_cat=l_cat, v=v, rkv=rkv,
            g_lhs=[jnp.concatenate([a_t[m], ak_cat[m]], axis=1) for m in P],
            y_lhs=[jnp.concatenate([r_t[m], rb_cat[m], rk_cat[m]], axis=1) for m in P],
            upd_lhs=[jnp.concatenate([b[m] * e_end[m], k[m] * e_end[m]], axis=0) for m in P],
            p_end=[jnp.exp(cs_last[m]) for m in P])
        yield

    def chunk_inverse(ctx):
        l_cat = ctx["l_cat"]
        x = [eye + l_cat[m] for m in P]
        lp = [_bdot(l_cat[m], bd(l_cat[m])) for m in P]
        yield
        for _ in range(4):
            both = [_bdot(jnp.concatenate([lp[m], x[m]], axis=0), bd(lp[m])) for m in P]
            yield
            x = [x[m] + both[m][CH:] for m in P]
            lp = [both[m][:CH] for m in P]
        ctx["x"] = [x[m] + _bdot(x[m], bd(lp[m])) for m in P]
        yield

    def chunk_state(ctx):
        x, v = ctx["x"], ctx["v"]
        t0 = [t_scr[seq[m], pair[m]] for m in P]
        v_abd = [abd(v[m]) for m in P]
        g = [_bdot(ctx["g_lhs"][m], jnp.concatenate([t0[m], v_abd[m]], axis=0)) for m in P]
        yield
        u = [_bdot(x[m], bd(g[m])) for m in P]
        yield
        upd = [_bdot_tn(ctx["upd_lhs"][m], jnp.concatenate([u[m], v[m]], axis=0)) for m in P]
        yield
        for m in P:
            p_col = jnp.broadcast_to(ctx["p_end"][m], (LANES, LANES)).T
            t_scr[seq[m], pair[m]] = t0[m] * p_col + jnp.where(diag_blocks, upd[m], 0.0)
        y = [_bdot(ctx["y_lhs"][m], jnp.concatenate([t0[m], bd(u[m]), v_abd[m]], axis=0)) for m in P]
        yield
        mean = seg_sum_pairs(y, 1)
        yield
        yc = [y[m] - mean[m] * (1.0 / KA) for m in P]
        var = seg_sum_pairs([yc[m] * yc[m] for m in P], 1)
        yield
        for m in P:
            yn = yc[m] * lax.rsqrt(var[m] * (1.0 / KA) + GN_EPS) * gain_ref[:, ln[m]] + bias_ref[:, ln[m]]
            o_ref[seq[m], ctx["rows"], ln[m]] = yn + ctx["rkv"][m] * v[m]
        yield

    nchunk = ct // CH
    ctxs = [dict() for _ in range(nchunk)]
    for c in range(-2, nchunk):
        stages = []
        if c >= 0:
            stages.append(chunk_state(ctxs[c]))
        if 0 <= c + 1 < nchunk:
            stages.append(chunk_inverse(ctxs[c + 1]))
        if c + 2 < nchunk:
            stages.append(chunk_prep((c + 2) * CH, ctxs[c + 2]))
        _interleave(*stages)

    @pl.when(j == pl.num_programs(1) - 1)
    def _():
        tfin_ref[...] = t_scr[...]


def _wkv_chunked(vecs, k_k, k_a, gain, bias, rk, *, nseq, t, ct, nbatch):
    row_spec = pl.BlockSpec((nbatch, ct, D), lambda n, j: (n, j, 0))
    tri = jnp.asarray(np.tril(np.ones((CH, CH), np.float32)), BF16)
    head_of_lane = np.arange(LANES) // KA
    ones_bd = jnp.asarray((head_of_lane[:, None] == head_of_lane[None, :]).astype(np.float32), BF16)
    ya, t_fin = pl.pallas_call(
        functools.partial(_wkv_chunk_kernel, ct=ct),
        grid=(nseq // nbatch, t // ct),
        in_specs=[row_spec] * 5 + [_const_spec((1, D))] * 5 + [_const_spec((CH, CH)), _const_spec((LANES, LANES))],
        out_specs=[row_spec, pl.BlockSpec((nbatch, NPAIR, LANES, LANES), lambda n, j: (n, 0, 0, 0))],
        out_shape=[jax.ShapeDtypeStruct((nseq, t, D), F32),
                   jax.ShapeDtypeStruct((nseq, NPAIR, LANES, LANES), F32)],
        scratch_shapes=[pltpu.VMEM((nbatch, NPAIR, LANES, LANES), F32)],
        compiler_params=pltpu.CompilerParams(dimension_semantics=("arbitrary", "arbitrary"),
                                             vmem_limit_bytes=VMEM_LIMIT),
        name="wkv_chunk",
    )(*[z.reshape(nseq, t, D) for z in vecs], k_k, k_a, gain, bias, rk, tri, ones_bd)
    t6 = t_fin.reshape(nseq, NPAIR, 2, KA, 2, KA)
    diag = jnp.stack([t6[:, :, 0, :, 0, :], t6[:, :, 1, :, 1, :]], axis=2)
    return ya.reshape(nseq * t, D), diag.reshape(nseq, HA, KA, KA).transpose(0, 1, 3, 2)


def _log_gamma():
    return [math.log1p(-2.0 ** (-5.0 - h)) for h in range(HB)]


def _rope_tables(pos):
    half = DK // 2
    theta = 1.0 / (ROPE_BASE ** jnp.linspace(0.0, 1.0, half, dtype=F32))
    ang = pos[:, None] * theta[None, :]
    cos, sin = jnp.cos(ang), jnp.sin(ang)
    return jnp.concatenate([cos, cos], axis=1), jnp.concatenate([-sin, sin], axis=1)


def _decay_tables(chunk):
    lg = jnp.asarray(_log_gamma(), F32)
    idx = jnp.arange(chunk, dtype=F32)
    diff = idx[:, None] - idx[None, :]
    dmat = jnp.exp(jnp.maximum(diff, 0.0)[None] * lg[:, None, None]) * (diff >= 0)[None]
    cross = jnp.exp((idx[None, :] + 1.0) * lg[:, None])[:, :, None]
    kdec = jnp.exp((chunk - 1.0 - idx)[None, :] * lg[:, None])[:, :, None]
    sdec = jnp.exp(chunk * lg)
    return dmat.astype(F32), cross, kdec, sdec


def _rotary(x, cos2, sin2):
    return x * cos2 + pltpu.roll(x, DK // 2, axis=1) * sin2


def _ln_rows(y, eps):
    m = jnp.mean(y, axis=-1, keepdims=True)
    yc = y - m
    var = jnp.mean(yc * yc, axis=-1, keepdims=True)
    return yc * lax.rsqrt(var + eps)


def _ret_post_kernel(qk_ref, vb_ref, ga_ref, ya_ref, gb_ref, x_ref, gate_ref, cos_ref, sin_ref,
                     dmat_ref, cross_ref, kdec_ref, sdec_ref, wout_ref, lng_ref, lnb_ref,
                     y_ref, sfin_ref, s_scr, *, tt, alpha):
    j = pl.program_id(1)

    @pl.when(j == 0)
    def _():
        s_scr[...] = jnp.zeros_like(s_scr)

    H = range(HB)
    nchunk = tt // RET_CHUNK
    merged = [None] * nchunk

    def retention(c):
        rows = slice(c * RET_CHUNK, (c + 1) * RET_CHUNK)
        cos2, sin2 = cos_ref[rows, :], sin_ref[rows, :]
        qb = [_rotary(qk_ref[rows, h * DK:(h + 1) * DK].astype(F32), cos2, sin2).astype(BF16) for h in H]
        k = [_rotary(qk_ref[rows, HB * DK + h * DK:HB * DK + (h + 1) * DK].astype(F32), cos2, sin2) * (DK ** -0.5)
             for h in H]
        v = [vb_ref[rows, h * DV:(h + 1) * DV] for h in H]
        scores = [_bdot_nt(qb[h], k[h]) * dmat_ref[h] for h in H]
        yield
        s = [s_scr[h] for h in H]
        cross = [jnp.dot(qb[h], s[h].astype(BF16), preferred_element_type=F32) * cross_ref[h] for h in H]
        yield
        upd = [_bdot_tn(k[h] * kdec_ref[h], v[h]) for h in H]
        yield
        inner = [jnp.dot(scores[h].astype(BF16), v[h], preferred_element_type=F32) for h in H]
        ob = []
        for h in H:
            s_scr[h] = s[h] * sdec_ref[h] + upd[h]
            ob.append(_ln_rows(inner[h] + cross[h], LN_EPS))
        merged[c] = (ga_ref[rows, :].astype(F32) * ya_ref[rows, :]
                     + gb_ref[rows, :].astype(F32) * jnp.concatenate(ob, axis=1)).astype(BF16)
        yield

    def project(c):
        rows = slice(c * RET_CHUNK, (c + 1) * RET_CHUNK)
        cols = []
        for n0 in range(0, D, 2 * LANES):
            cols.append(jnp.dot(merged[c], wout_ref[:, n0:n0 + 2 * LANES], preferred_element_type=F32))
            yield
        z = alpha * x_ref[rows, :] + gate_ref[...] * jnp.concatenate(cols, axis=1)
        y_ref[rows, :] = _ln_rows(z, LN_EPS) * lng_ref[...] + lnb_ref[...]
        yield

    for c in range(-1, nchunk):
        _interleave(*([project(c)] if c >= 0 else []), *([retention(c + 1)] if c + 1 < nchunk else []))

    @pl.when(j == pl.num_programs(1) - 1)
    def _():
        sfin_ref[...] = s_scr[...]


def _ret_post(qk, vb, ga, ya, gb, x2, gate, wout, ln_gain, ln_bias, *, nseq, t, tt, alpha):
    tiles = t // tt
    cos2, sin2 = _rope_tables(jnp.arange(t, dtype=F32) + 0)
    dmat, cross, kdec, sdec = _decay_tables(RET_CHUNK)
    row_spec = pl.BlockSpec((tt, D), lambda n, j: (n * tiles + j, 0))
    tab_spec = pl.BlockSpec((tt, DK), lambda n, j: (j, 0))
    return pl.pallas_call(
        functools.partial(_ret_post_kernel, tt=tt, alpha=alpha),
        grid=(nseq, tiles),
        in_specs=[row_spec] * 6 + [pl.BlockSpec((None, 1, D), lambda n, j: (n, 0, 0)), tab_spec, tab_spec,
                                   _const_spec((HB, RET_CHUNK, RET_CHUNK)), _const_spec((HB, RET_CHUNK, 1)),
                                   _const_spec((HB, RET_CHUNK, 1)), pl.BlockSpec(memory_space=pltpu.SMEM),
                                   _const_spec((D, D)), _const_spec((1, D)), _const_spec((1, D))],
        out_specs=[row_spec, pl.BlockSpec((None, HB, DK, DV), lambda n, j: (n, 0, 0, 0))],
        out_shape=[jax.ShapeDtypeStruct(x2.shape, F32),
                   jax.ShapeDtypeStruct((nseq, HB, DK, DV), F32)],
        scratch_shapes=[pltpu.VMEM((HB, DK, DV), F32)],
        compiler_params=pltpu.CompilerParams(dimension_semantics=("arbitrary", "arbitrary"),
                                             vmem_limit_bytes=VMEM_LIMIT),
        name="ret_post",
    )(qk, vb, ga, ya, gb, x2, gate, cos2, sin2, dmat, cross, kdec, sdec, wout, ln_gain, ln_bias)


def _ret_sample_kernel(qk_ref, vb_ref, cos_ref, sin_ref, s_ref, dmat_ref, cross_ref, kdec_ref, sdec_ref,
                       o_ref, snew_ref, *, nb):
    cos2, sin2 = cos_ref[...], sin_ref[...]
    for n in range(nb):
        for h in range(HB):
            q = _rotary(qk_ref[:, n, h * DK:(h + 1) * DK], cos2, sin2)
            k = _rotary(qk_ref[:, n, HB * DK + h * DK:HB * DK + (h + 1) * DK], cos2, sin2) * (DK ** -0.5)
            v = vb_ref[:, n, h * DV:(h + 1) * DV].astype(BF16)
            qb = q.astype(BF16)
            scores = lax.dot_general(qb, k.astype(BF16), (((1,), (1,)), ((), ())),
                                     preferred_element_type=F32) * dmat_ref[h]
            s = s_ref[n, h]
            y = (jnp.dot(scores.astype(BF16), v, preferred_element_type=F32)
                 + jnp.dot(qb, s.astype(BF16), preferred_element_type=F32) * cross_ref[h])
            kd = (k * kdec_ref[h]).astype(BF16)
            snew_ref[n, h] = s * sdec_ref[h] + lax.dot_general(kd, v, (((0,), (0,)), ((), ())),
                                                               preferred_element_type=F32)
            o_ref[:, n, h * DV:(h + 1) * DV] = _ln_rows(y, LN_EPS)


def _ret_sample(qk3, vb3, s0, pos0, *, nb):
    t, nseq, _ = qk3.shape
    cos2, sin2 = _rope_tables(jnp.arange(t, dtype=F32) + pos0)
    dmat, cross, kdec, sdec = _decay_tables(t)
    row_spec = pl.BlockSpec((t, nb, D), lambda i: (0, i, 0))
    st_spec = pl.BlockSpec((nb, HB, DK, DV), lambda i: (i, 0, 0, 0))
    return pl.pallas_call(
        functools.partial(_ret_sample_kernel, nb=nb),
        grid=(nseq // nb,),
        in_specs=[row_spec, row_spec, _const_spec((t, DK)), _const_spec((t, DK)), st_spec,
                  _const_spec((HB, t, t)), _const_spec((HB, t, 1)), _const_spec((HB, t, 1)),
                  pl.BlockSpec(memory_space=pltpu.SMEM)],
        out_specs=[row_spec, st_spec],
        out_shape=[jax.ShapeDtypeStruct(qk3.shape, F32), jax.ShapeDtypeStruct(s0.shape, F32)],
        compiler_params=pltpu.CompilerParams(dimension_semantics=("arbitrary",),
                                             vmem_limit_bytes=VMEM_LIMIT),
        name="ret_sample",
    )(qk3, vb3, cos2, sin2, s0, dmat, cross, kdec, sdec)


def _unpair_tokens(ya_ref, ya_scr, pair_rows):
    half = ya_ref.shape[0]
    groups = half // pair_rows
    low = lax.broadcasted_iota(jnp.int32, (half, LANES), 1) < KA
    for m in range(D // LANES):
        ha = ya_ref[:, 2 * m, :]
        hb = ya_ref[:, 2 * m + 1, :]
        lo = jnp.where(low, ha, pltpu.roll(hb, KA, axis=1))
        hi = jnp.where(low, pltpu.roll(ha, KA, axis=1), hb)
        ya_scr[:, 0, :, m * LANES:(m + 1) * LANES] = lo.reshape(groups, pair_rows, LANES)
        ya_scr[:, 1, :, m * LANES:(m + 1) * LANES] = hi.reshape(groups, pair_rows, LANES)
    return ya_scr[...].reshape(2 * half, D)


def _post_kernel(ga_ref, ya_ref, gb_ref, ob_ref, x_ref, gate_ref, wout_ref, lng_ref, lnb_ref, y_ref,
                 ya_scr, *, gs, alpha, pair_rows):
    merged = ga_ref[...] * _unpair_tokens(ya_ref, ya_scr, pair_rows) + gb_ref[...] * ob_ref[...]
    sub = jnp.dot(merged.astype(BF16), wout_ref[...], preferred_element_type=F32)
    x = x_ref[...]
    gate = gate_ref[...]
    if gs == 1:
        z = alpha * x + gate * sub
    else:
        tt = x.shape[0]
        z = (alpha * x.reshape(tt // gs, gs, D) + gate[None] * sub.reshape(tt // gs, gs, D)).reshape(tt, D)
    y_ref[...] = _ln_rows(z, LN_EPS) * lng_ref[...] + lnb_ref[...]


def _post(ga, ya, gb, ob, x2, gate, wout, ln_gain, ln_bias, *, nseq_grid, tiles, tt, gs, alpha, pair_rows):
    row_spec = pl.BlockSpec((tt, D), lambda n, j: (n * tiles + j, 0))
    ya_spec = pl.BlockSpec((tt // 2, HA, LANES), lambda n, j: (n * tiles + j, 0, 0))
    scratch = [pltpu.VMEM((tt // 2 // pair_rows, 2, pair_rows, D), F32)]
    return pl.pallas_call(
        functools.partial(_post_kernel, gs=gs, alpha=alpha, pair_rows=pair_rows),
        grid=(nseq_grid, tiles),
        in_specs=[row_spec, ya_spec, row_spec, row_spec, row_spec,
                  pl.BlockSpec((None, gs, D), lambda n, j: (n, 0, 0)),
                  _const_spec((D, D)), _const_spec((1, D)), _const_spec((1, D))],
        out_specs=row_spec,
        out_shape=jax.ShapeDtypeStruct(x2.shape, F32),
        scratch_shapes=scratch,
        compiler_params=pltpu.CompilerParams(dimension_semantics=("arbitrary", "arbitrary"),
                                             vmem_limit_bytes=VMEM_LIMIT),
        name="post",
    )(ga, ya, gb, ob, x2, gate, wout, ln_gain, ln_bias)


def _head_param_lanes(p, nseq):
    return jnp.tile(p.reshape(HA, KA).T, (1, nseq))


def _layer(x2, ada, first_prev, s_wkv, s_ret, pos0, weights, *, nseq, t, seq_major, tt, tc, tp, depth, wkv_seqs=1):
    (wsh, wrest, pre_params, gn_gain, gn_bias, r_k, wout, ln_gain, ln_bias) = weights
    if seq_major:
        nseq_grid, gs, tiles = nseq, 1, t // tt
    else:
        nseq_grid, gs, tiles = 1, nseq, (t * nseq) // tt
    shift_c = ada[:, :D].reshape(nseq_grid, gs, D)
    scale_c = ada[:, D:2 * D].reshape(nseq_grid, gs, D)
    gate_c = ada[:, 2 * D:].reshape(nseq_grid, gs, D)
    cfg = dict(nseq_grid=nseq_grid, tiles=tiles, tt=tt, gs=gs)

    *vecs, ga, gb, qk, vb, new_shift = _pre(
        x2, scale_c, shift_c, first_prev.reshape(nseq_grid, gs, SHIFT_W), wsh, wrest, pre_params,
        pair_out=not seq_major, gate_dtype=BF16 if seq_major else F32, **cfg)

    if seq_major:
        row = lambda p: p.reshape(1, D)
        ya, new_wkv = _wkv_chunked(vecs, pre_params[3], pre_params[4], row(gn_gain), row(gn_bias), row(r_k),
                                   nseq=nseq, t=t, ct=tc, nbatch=wkv_seqs)
        pair_rows = None
    else:
        s0 = s_wkv.transpose(3, 2, 0, 1).reshape(KA, KA, nseq * HA)
        ya, s_fin = _wkv(vecs, s0, _head_param_lanes(gn_gain, nseq),
                         _head_param_lanes(gn_bias, nseq), _head_param_lanes(r_k.reshape(-1), nseq),
                         nseq=nseq, t=t, tc=tc, seq_major=False)
        pair_rows = (tc // 2) * gs
        new_wkv = s_fin.reshape(KA, KA, nseq, HA).transpose(2, 3, 1, 0)

    alpha = (2.0 * depth) ** 0.25
    if seq_major:
        y, new_ret = _ret_post(qk, vb, ga, ya, gb, x2, gate_c, wout, ln_gain, ln_bias,
                               nseq=nseq, t=t, tt=tp, alpha=alpha)
    else:
        ob3, new_ret = _ret_sample(qk.reshape(t, nseq, D), vb.reshape(t, nseq, D), s_ret, pos0, nb=8)
        y = _post(ga, ya, gb, ob3.reshape(t * nseq, D), x2, gate_c, wout, ln_gain, ln_bias, alpha=alpha,
                  pair_rows=pair_rows, nseq_grid=nseq_grid, tiles=(t * nseq) // tp, tt=tp, gs=gs)
    return y, new_shift.reshape(nseq, SHIFT_W), new_wkv, new_ret


def kernel(x_prompt, x_sample, c_prompt, c_sample, state_shift, state_wkv, state_ret, w_ada, b_ada, w_in,
           mu_shift, w0, w_decay_up, a0, w_icl_up, k_k, k_a, r_k, gn_a_gain, gn_a_bias, w_out, ln_gain, ln_bias):
    depth = w_in.shape[0]
    nb, t_p, _ = x_prompt.shape
    ns, t_s, _ = x_sample.shape
    past_len = 16384
    h_p = x_prompt.reshape(nb * t_p, D)
    h_s = x_sample.transpose(1, 0, 2).reshape(t_s * ns, D)
    outs = [[] for _ in range(6)]
    zero_lora = jnp.zeros((LORA, D), F32)
    for l in range(depth):
        row = lambda p: p[l].reshape(1, -1)
        wsh, wrest = _cast_w_in(w_in, l)
        wd_pad = jnp.concatenate([w_decay_up[l], zero_lora], axis=0).astype(BF16)
        wa_pad = jnp.concatenate([zero_lora, w_icl_up[l]], axis=0).astype(BF16)
        pre_params = (row(mu_shift), row(w0), row(a0), row(k_k), row(k_a), wd_pad, wa_pad)
        weights = (wsh, wrest, pre_params, gn_a_gain[l], gn_a_bias[l], r_k[l], w_out[l].astype(BF16),
                   row(ln_gain), row(ln_bias))
        ada = _ada(jnp.concatenate([c_prompt, c_sample], axis=0), w_ada[l], b_ada[l])
        h_p, s1, s2, s3 = _layer(
            h_p, ada[:nb], jnp.zeros((nb, SHIFT_W), F32), None, None, 0,
            weights, nseq=nb, t=t_p, seq_major=True, depth=depth, **PROMPT_TILES)
        h_s, t1, t2, t3 = _layer(
            h_s, ada[nb:], state_shift[l], state_wkv[l], state_ret[l], past_len,
            weights, nseq=ns, t=t_s, seq_major=False, tt=ns, tc=t_s, tp=t_s * ns, depth=depth)
        for lst, val in zip(outs, (s1, s2, s3, t1, t2, t3)):
            lst.append(val)
    y_p = h_p.reshape(nb, t_p, D)
    y_s = h_s.reshape(t_s, ns, D).transpose(1, 0, 2)
    return (y_p, y_s) + tuple(jnp.stack(o) for o in outs)
```

```python
import functools
import math

import numpy as np
import jax
import jax.numpy as jnp
from jax import lax
from jax.experimental import pallas as pl
from jax.experimental.pallas import tpu as pltpu

F32 = jnp.float32
BF16 = jnp.bfloat16

D = 1024
HA = 16
KA = 64
LORA = 64
GN_EPS = 64e-5
HB = 4
DK = 128
DV = 256
RET_CHUNK = 128
ROPE_BASE = 10000.0
LN_EPS = 1e-5
SHIFT_W = 3 * D + 2 * LORA
REST_W = 6 * D
LANES = 128
SUBLANES = 8
VMEM_LIMIT = 56 * 1024 * 1024
PROMPT_TILES = dict(tt=256, tc=256, wkv_seqs=2, tp=256)


def _sigmoid(x):
    return 1.0 / (1.0 + jnp.exp(-x))


def _log_decay(w_pre):
    zw = -w_pre
    softplus = jnp.maximum(zw, 0.0) + jnp.log(1.0 + jnp.exp(-jnp.abs(zw)))
    return -jnp.exp(-softplus - 0.5)


def _bdot(a, b):
    return jnp.dot(a.astype(BF16), b.astype(BF16), preferred_element_type=F32)


def _bdot_nt(a, b):
    return lax.dot_general(a.astype(BF16), b.astype(BF16), (((1,), (1,)), ((), ())), preferred_element_type=F32)


def _bdot_tn(a, b):
    return lax.dot_general(a.astype(BF16), b.astype(BF16), (((0,), (0,)), ((), ())), preferred_element_type=F32)


def _split_bf16(x, parts):
    out = []
    for _ in range(parts):
        term = x.astype(BF16)
        out.append(term)
        x = x - term.astype(F32)
    return out


def _interleave(*gens):
    live = list(gens)
    while live:
        for gen in list(live):
            try:
                next(gen)
            except StopIteration:
                live.remove(gen)


def _const_spec(shape):
    nd = len(shape)
    return pl.BlockSpec(shape, lambda *_: (0,) * nd, pipeline_mode=pl.Buffered(1))


def _cast_w_in_kernel(w_ref, sh_ref, rest_ref):
    sh_ref[...] = w_ref[:, :SHIFT_W].astype(BF16)
    rest_ref[...] = w_ref[:, SHIFT_W:].astype(BF16)


def _cast_w_in(w_in, layer):
    rb = 256
    return pl.pallas_call(
        _cast_w_in_kernel,
        grid=(D // rb,),
        in_specs=[pl.BlockSpec((None, rb, SHIFT_W + REST_W), lambda i: (layer, i, 0))],
        out_specs=[pl.BlockSpec((rb, SHIFT_W), lambda i: (i, 0)), pl.BlockSpec((rb, REST_W), lambda i: (i, 0))],
        out_shape=[jax.ShapeDtypeStruct((D, SHIFT_W), BF16), jax.ShapeDtypeStruct((D, REST_W), BF16)],
        compiler_params=pltpu.CompilerParams(dimension_semantics=("arbitrary",)),
        name="cast_w_in",
    )(w_in)


def _ada_kernel(c_ref, w_ref, b_ref, o_ref):
    c = c_ref[...]
    s = c * _sigmoid(c)
    o_ref[...] = jnp.dot(s, w_ref[...], preferred_element_type=F32,
                         precision=lax.Precision.HIGHEST) + b_ref[...]


def _ada(c_all, w_ada, b_ada):
    n = c_all.shape[0]
    return pl.pallas_call(
        _ada_kernel,
        grid=(3,),
        in_specs=[pl.BlockSpec((n, D), lambda j: (0, 0)),
                  pl.BlockSpec((D, D), lambda j: (0, j)),
                  pl.BlockSpec((1, D), lambda j: (0, j))],
        out_specs=pl.BlockSpec((n, D), lambda j: (0, j)),
        out_shape=jax.ShapeDtypeStruct((n, 3 * D), F32),
        compiler_params=pltpu.CompilerParams(dimension_semantics=("arbitrary",)),
        name="ada",
    )(c_all, w_ada, b_ada.reshape(1, 3 * D))


def _modulate(x, scale, shift, gs):
    if gs == 1:
        return x * (1.0 + scale) + shift
    tt = x.shape[0]
    x3 = x.reshape(tt // gs, gs, D)
    return (x3 * (1.0 + scale)[None] + shift[None]).reshape(tt, D)


def _put_head_pairs(out_ref, scr, x, y):
    tt = x.shape[0]
    low = lax.broadcasted_iota(jnp.int32, (tt, LANES), 1) < KA
    for m in range(D // LANES):
        xc = x[:, m * LANES:(m + 1) * LANES]
        yc = y[:, m * LANES:(m + 1) * LANES]
        even = jnp.where(low, xc, pltpu.roll(yc, KA, axis=1))
        odd = jnp.where(low, pltpu.roll(xc, KA, axis=1), yc)
        for r in range(tt // SUBLANES):
            rows = slice(r * SUBLANES, (r + 1) * SUBLANES)
            scr[pl.ds((r * HA + 2 * m) * SUBLANES, SUBLANES), :] = even[rows]
            scr[pl.ds((r * HA + 2 * m + 1) * SUBLANES, SUBLANES), :] = odd[rows]

    def row_block(r, carry):
        for s in range(SUBLANES):
            out_ref[r * SUBLANES + s] = scr[pl.ds(r * (HA * SUBLANES) + s, HA, stride=SUBLANES), :]
        return carry

    lax.fori_loop(0, tt // SUBLANES, row_block, 0)


def _pre_kernel(x_ref, scale_ref, shift_ref, prev_ref, wsh_ref, wrest_ref, mu_ref, w0_ref, a0_ref,
                kk_ref, ka_ref, wd_ref, wa_ref, *rest, gs, tt, pad, pair_out):
    if pair_out:
        rk_out, wk_out, va_out, ga_out, gb_out, qk_out, vb_out, ns_out, p_scr, pair_scr = rest
    else:
        r_out, k_out, v_out, wpre_out, apre_out, ga_out, gb_out, qk_out, vb_out, ns_out, p_scr = rest
    j = pl.program_id(1)
    nj = pl.num_programs(1)
    lo = pad - gs

    @pl.when(j == 0)
    def _():
        p_scr[lo:pad, :] = prev_ref[...]

    u = _modulate(x_ref[...], scale_ref[...], shift_ref[...], gs).astype(BF16)

    def mixed(c0, c1):
        p_scr[pad:pad + tt, c0:c1] = jnp.dot(u, wsh_ref[:, c0:c1], preferred_element_type=F32)
        p = p_scr[pad:pad + tt, c0:c1]
        prev = p_scr[lo:lo + tt, c0:c1]
        return p + (prev - p) * mu_ref[:, c0:c1]

    lora_in = mixed(3 * D, SHIFT_W)
    lora_w = _bdot(jnp.tanh(lora_in), wd_ref[...])
    lora_a = _bdot(lora_in, wa_ref[...])
    w_pre = w0_ref[...] + lora_w
    a_pre = a0_ref[...] + lora_a
    k = mixed(D, 2 * D)
    if pair_out:
        a = _sigmoid(a_pre)
        _put_head_pairs(rk_out, pair_scr.at[0], mixed(0, D), k * (1.0 + (a - 1.0) * ka_ref[...]))
        _put_head_pairs(wk_out, pair_scr.at[1], jnp.exp(_log_decay(w_pre)), k * kk_ref[...])
        _put_head_pairs(va_out, pair_scr.at[2], mixed(2 * D, 3 * D), a)
    else:
        wpre_out[...] = w_pre
        apre_out[...] = a_pre
        k_out[...] = k
        r_out[...] = mixed(0, D)
        v_out[...] = mixed(2 * D, 3 * D)

    @pl.when(j == nj - 1)
    def _():
        ns_out[...] = p_scr[pad + tt - gs:pad + tt, :]

    p_scr[lo:pad, :] = p_scr[pad + tt - gs:pad + tt, :]

    def proj(c0, c1):
        return jnp.dot(u, wrest_ref[:, c0:c1], preferred_element_type=F32)

    z_a = proj(0, D)
    g_a = proj(4 * D, 5 * D)
    ga_out[...] = (_sigmoid(g_a) * (z_a * _sigmoid(z_a))).astype(ga_out.dtype)
    z_b = proj(3 * D, 4 * D)
    g_b = proj(5 * D, 6 * D)
    gb_out[...] = (_sigmoid(g_b) * (z_b * _sigmoid(z_b))).astype(gb_out.dtype)
    qk_out[...] = proj(D, 2 * D).astype(qk_out.dtype)
    vb_out[...] = proj(2 * D, 3 * D).astype(vb_out.dtype)


def _pre(x2, scale, shift, first_prev, wsh, wrest, params, *, nseq_grid, tiles, tt, gs, pair_out, gate_dtype):
    rows = x2.shape[0]
    pad = max(SUBLANES, gs)
    mu, w0, a0, k_k, k_a, wd_pad, wa_pad = params
    row_spec = pl.BlockSpec((tt, D), lambda n, j: (n * tiles + j, 0))
    seq_spec = pl.BlockSpec((None, gs, D), lambda n, j: (n, 0, 0))
    scratch = [pltpu.VMEM((pad + tt, SHIFT_W), F32)]
    if pair_out:
        vec_specs = [pl.BlockSpec((tt, HA, LANES), lambda n, j: (n * tiles + j, 0, 0))] * 3
        vec_sds = [jax.ShapeDtypeStruct((rows, HA, LANES), F32)] * 3
        scratch.append(pltpu.VMEM((3, tt * HA, LANES), F32))
    else:
        vec_specs = [row_spec] * 5
        vec_sds = [jax.ShapeDtypeStruct((rows, D), F32)] * 5
    gate_sds = [jax.ShapeDtypeStruct((rows, D), gate_dtype)] * 4
    kern = functools.partial(_pre_kernel, gs=gs, tt=tt, pad=pad, pair_out=pair_out)
    return pl.pallas_call(
        kern,
        grid=(nseq_grid, tiles),
        in_specs=[row_spec, seq_spec, seq_spec,
                  pl.BlockSpec((None, gs, SHIFT_W), lambda n, j: (n, 0, 0)),
                  _const_spec((D, SHIFT_W)), _const_spec((D, REST_W)), _const_spec((1, SHIFT_W)),
                  _const_spec((1, D)), _const_spec((1, D)), _const_spec((1, D)), _const_spec((1, D)),
                  _const_spec((LANES, D)), _const_spec((LANES, D))],
        out_specs=vec_specs + [row_spec] * 4 + [pl.BlockSpec((None, gs, SHIFT_W), lambda n, j: (n, 0, 0))],
        out_shape=vec_sds + gate_sds + [jax.ShapeDtypeStruct((nseq_grid, gs, SHIFT_W), F32)],
        scratch_shapes=scratch,
        compiler_params=pltpu.CompilerParams(dimension_semantics=("arbitrary", "arbitrary"),
                                             vmem_limit_bytes=VMEM_LIMIT),
        name="pre",
    )(x2, scale, shift, first_prev, wsh, wrest, mu, w0, a0, k_k, k_a, wd_pad, wa_pad)


V_KKN, V_B, V_W, V_K, V_R, V_V = range(6)


def _wkv_kernel(rk_ref, wk_ref, va_ref, s0_ref, gain_ref, bias_ref, rkp_ref,
                o_ref, sfin_ref, s_scr, vec_scr, row_scr, y_scr, cur_scr, *, tc):
    j = pl.program_id(1)

    @pl.when(j == 0)
    def _():
        s_scr[...] = s0_ref[...]

    def load_pair(ref, t):
        xt = ref[t].reshape(LANES, LANES).T
        return xt[:KA], xt[KA:]

    def prep(t, carry):
        dec, kkraw = load_pair(wk_ref, t)
        ss = jnp.sum(kkraw * kkraw, axis=0, keepdims=True)
        kkn = kkraw * (1.0 / jnp.maximum(jnp.sqrt(ss), 1e-12))
        v, a = load_pair(va_ref, t)
        r, kmod = load_pair(rk_ref, t)
        vec_scr[t, V_KKN] = kkn
        vec_scr[t, V_B] = kkn * a
        vec_scr[t, V_W] = dec
        vec_scr[t, V_K] = kmod
        vec_scr[t, V_R] = r
        vec_scr[t, V_V] = v
        rk = jnp.sum(r * kmod * rkp_ref[...], axis=0, keepdims=True)
        row_scr[t] = jnp.broadcast_to(rk, (SUBLANES, LANES))
        return carry

    def state_pass(t, carry):
        cur_scr[...] = vec_scr[t]
        skk = jnp.zeros((KA, LANES), F32)
        for k in range(KA):
            skk = skk + s_scr[k] * cur_scr[V_KKN, k:k + 1, :]
        vv = cur_scr[V_V]
        y = jnp.zeros((KA, LANES), F32)
        for k in range(KA):
            s_new = (s_scr[k] * cur_scr[V_W, k:k + 1, :] - skk * cur_scr[V_B, k:k + 1, :]
                     + vv * cur_scr[V_K, k:k + 1, :])
            s_scr[k] = s_new
            y = y + s_new * cur_scr[V_R, k:k + 1, :]
        y_scr[t] = y
        return carry

    def out_token(t):
        y = y_scr[t]
        m = jnp.mean(y, axis=0, keepdims=True)
        yc = y - m
        var = jnp.mean(yc * yc, axis=0, keepdims=True)
        yn = yc * lax.rsqrt(var + GN_EPS) * gain_ref[...] + bias_ref[...]
        return yn + row_scr[t, 0:1, :] * vec_scr[t, V_V]

    def finish(i, carry):
        o2 = jnp.concatenate([out_token(i), out_token(i + tc // 2)], axis=0)
        o_ref[i] = o2.T.reshape(LANES // HA, HA, LANES)
        return carry

    lax.fori_loop(0, tc, prep, 0, unroll=2)
    lax.fori_loop(0, tc, state_pass, 0)
    lax.fori_loop(0, tc // 2, finish, 0, unroll=2)

    @pl.when(j == pl.num_programs(1) - 1)
    def _():
        sfin_ref[...] = s_scr[...]


def _wkv(pairs, s0, gain_l, bias_l, rk_l, *, nseq, t, tc):
    nsl = LANES // HA
    groups = nseq // nsl
    shape4, oshape4 = (t, nseq, HA, LANES), (t // 2, nseq, HA, LANES)
    vec_spec = pl.BlockSpec((tc, nsl, HA, LANES), lambda g, j: (j, g, 0, 0))
    out_spec = pl.BlockSpec((tc // 2, nsl, HA, LANES), lambda g, j: (j, g, 0, 0))
    st_spec = pl.BlockSpec((KA, KA, LANES), lambda g, j: (0, 0, g))
    par_spec = pl.BlockSpec((KA, LANES), lambda g, j: (0, g))
    o4, s_fin = pl.pallas_call(
        functools.partial(_wkv_kernel, tc=tc),
        grid=(groups, t // tc),
        in_specs=[vec_spec] * 3 + [st_spec, par_spec, par_spec, par_spec],
        out_specs=[out_spec, st_spec],
        out_shape=[jax.ShapeDtypeStruct(oshape4, F32),
                   jax.ShapeDtypeStruct((KA, KA, nseq * HA), F32)],
        scratch_shapes=[pltpu.VMEM((KA, KA, LANES), F32), pltpu.VMEM((tc, 6, KA, LANES), F32),
                        pltpu.VMEM((tc, SUBLANES, LANES), F32), pltpu.VMEM((tc, KA, LANES), F32),
                        pltpu.VMEM((6, KA, LANES), F32)],
        compiler_params=pltpu.CompilerParams(dimension_semantics=("arbitrary", "arbitrary"),
                                             vmem_limit_bytes=VMEM_LIMIT),
        name="wkv",
    )(*[z.reshape(shape4) for z in pairs], s0, gain_l, bias_l, rk_l)
    return o4.reshape(nseq * t // 2, HA, LANES), s_fin


CH = 64
NPAIR = D // LANES


def _wkv_chunk_kernel(r_ref, k_ref, v_ref, wpre_ref, apre_ref, kk_ref, ka_ref, gain_ref, bias_ref, rk_ref,
                      tri_ref, ones_ref, o_ref, tfin_ref, t_scr, *, ct):
    j = pl.program_id(1)

    @pl.when(j == 0)
    def _():
        t_scr[...] = jnp.zeros_like(t_scr)

    lane = lax.broadcasted_iota(jnp.int32, (CH, LANES), 1)
    row = lax.broadcasted_iota(jnp.int32, (CH, LANES), 0)
    low = lane < KA
    col = jnp.bitwise_and(lane, KA - 1)
    strict = row > col
    incl = row >= col
    eye = (row == col).astype(F32)
    low2 = lax.broadcasted_iota(jnp.int32, (LANES, LANES), 1) < KA
    diag_blocks = (lax.broadcasted_iota(jnp.int32, (LANES, LANES), 0) < KA) == low2
    tri = tri_ref[...]
    ones_bd = ones_ref[...]

    def bd(z):
        return jnp.concatenate([jnp.where(low, z, 0.0), jnp.where(low, 0.0, z)], axis=0)

    def abd(z):
        return jnp.concatenate([jnp.where(low, 0.0, z), jnp.where(low, z, 0.0)], axis=0)

    def seg_sum_pairs(xs):
        tot = _bdot(jnp.concatenate(xs, axis=0), ones_bd)
        return [tot[i * CH:(i + 1) * CH] for i in range(len(xs))]

    nbatch = r_ref.shape[0]
    B = range(nbatch)
    P = range(nbatch * NPAIR)
    seq = [i // NPAIR for i in P]
    pair = [i % NPAIR for i in P]
    ln = [slice(pair[i] * LANES, (pair[i] + 1) * LANES) for i in P]

    def chunk_prep(c0, ctx):
        rows = slice(c0, c0 + CH)
        lw_all = [_log_decay(wpre_ref[b, rows, :]) for b in B]
        cs_all = [sum(jnp.dot(tri, term, preferred_element_type=F32) for term in _split_bf16(lw_all[b], 2))
                  for b in B]
        yield
        lw = [lw_all[seq[m]][:, ln[m]] for m in P]
        cs = [cs_all[seq[m]][:, ln[m]] for m in P]
        k_in = [k_ref[seq[m], rows, ln[m]] for m in P]
        kkraw = [k_in[m] * kk_ref[:, ln[m]] for m in P]
        ss = seg_sum_pairs([kkraw[m] * kkraw[m] for m in P])
        yield
        a = [_sigmoid(apre_ref[seq[m], rows, ln[m]]) for m in P]
        kkn = [kkraw[m] / jnp.maximum(jnp.sqrt(ss[m]), 1e-12) for m in P]
        b = [kkn[m] * a[m] for m in P]
        r = [r_ref[seq[m], rows, ln[m]] for m in P]
        k = [k_in[m] * (1.0 + (a[m] - 1.0) * ka_ref[:, ln[m]]) for m in P]
        v = [v_ref[seq[m], rows, ln[m]] for m in P]
        cs_last = [cs[m][CH - 1:CH, :] for m in P]
        e_neg = [jnp.exp(-cs[m]) for m in P]
        a_t = [-kkn[m] * jnp.exp(cs[m] - lw[m]) for m in P]
        r_t = [r[m] * jnp.exp(cs[m]) for m in P]
        b_t = [b[m] * e_neg[m] for m in P]
        k_t = [k[m] * e_neg[m] for m in P]
        lhs = [jnp.concatenate([a_t[m], r_t[m]], axis=0) for m in P]
        out0 = [_bdot_nt(jnp.where(low2, lhs[m], 0.0), jnp.concatenate([b_t[m], k_t[m]], axis=0)) for m in P]
        yield
        out1 = [_bdot_nt(jnp.where(low2, 0.0, lhs[m]), jnp.concatenate([k_t[m], b_t[m]], axis=0)) for m in P]
        yield
        l_cat = [jnp.where(strict, jnp.where(low, out0[m][:CH], out1[m][:CH]), 0.0) for m in P]
        ak_cat = [jnp.where(strict, jnp.where(low, out1[m][:CH], out0[m][:CH]), 0.0) for m in P]
        rb_cat = [jnp.where(incl, jnp.where(low, out0[m][CH:], out1[m][CH:]), 0.0) for m in P]
        rk_cat = [jnp.where(incl, jnp.where(low, out1[m][CH:], out0[m][CH:]), 0.0) for m in P]
        rkv = seg_sum_pairs([r[m] * k[m] * rk_ref[:, ln[m]] for m in P])
        e_end = [jnp.exp(cs_last[m] - cs[m]) for m in P]
        ctx.update(
            rows=rows, l_cat=l_cat, v=v, rkv=rkv,
            g_lhs=[jnp.concatenate([a_t[m], ak_cat[m]], axis=1) for m in P],
            y_lhs=[jnp.concatenate([r_t[m], rb_cat[m], rk_cat[m]], axis=1) for m in P],
            upd_lhs=[jnp.concatenate([b[m] * e_end[m], k[m] * e_end[m]], axis=0) for m in P],
            p_end=[jnp.exp(cs_last[m]) for m in P])
        yield

    def chunk_inverse(ctx):
        l_cat = ctx["l_cat"]
        x = [eye + l_cat[m] for m in P]
        lp = [_bdot(l_cat[m], bd(l_cat[m])) for m in P]
        yield
        for _ in range(4):
            both = [_bdot(jnp.concatenate([lp[m], x[m]], axis=0), bd(lp[m])) for m in P]
            yield
            x = [x[m] + both[m][CH:] for m in P]
            lp = [both[m][:CH] for m in P]
        ctx["x"] = [x[m] + _bdot(x[m], bd(lp[m])) for m in P]
        yield

    def chunk_state(ctx):
        x, v = ctx["x"], ctx["v"]
        t0 = [t_scr[seq[m], pair[m]] for m in P]
        v_abd = [abd(v[m]) for m in P]
        g = [_bdot(ctx["g_lhs"][m], jnp.concatenate([t0[m], v_abd[m]], axis=0)) for m in P]
        yield
        u = [_bdot(x[m], bd(g[m])) for m in P]
        yield
        upd = [_bdot_tn(ctx["upd_lhs"][m], jnp.concatenate([u[m], v[m]], axis=0)) for m in P]
        yield
        for m in P:
            p_col = jnp.broadcast_to(ctx["p_end"][m], (LANES, LANES)).T
            t_scr[seq[m], pair[m]] = t0[m] * p_col + jnp.where(diag_blocks, upd[m], 0.0)
        y = [_bdot(ctx["y_lhs"][m], jnp.concatenate([t0[m], bd(u[m]), v_abd[m]], axis=0)) for m in P]
        yield
        mean = seg_sum_pairs(y)
        yield
        yc = [y[m] - mean[m] * (1.0 / KA) for m in P]
        var = seg_sum_pairs([yc[m] * yc[m] for m in P])
        yield
        for m in P:
            yn = yc[m] * lax.rsqrt(var[m] * (1.0 / KA) + GN_EPS) * gain_ref[:, ln[m]] + bias_ref[:, ln[m]]
            o_ref[seq[m], ctx["rows"], ln[m]] = yn + ctx["rkv"][m] * v[m]
        yield

    nchunk = ct // CH
    ctxs = [dict() for _ in range(nchunk)]
    for c in range(-2, nchunk):
        stages = []
        if c >= 0:
            stages.append(chunk_state(ctxs[c]))
        if 0 <= c + 1 < nchunk:
            stages.append(chunk_inverse(ctxs[c + 1]))
        if c + 2 < nchunk:
            stages.append(chunk_prep((c + 2) * CH, ctxs[c + 2]))
        _interleave(*stages)

    @pl.when(j == pl.num_programs(1) - 1)
    def _():
        tfin_ref[...] = t_scr[...]


def _wkv_chunked(vecs, k_k, k_a, gain, bias, rk, *, nseq, t, ct, nbatch):
    row_spec = pl.BlockSpec((nbatch, ct, D), lambda n, j: (n, j, 0))
    tri = jnp.asarray(np.tril(np.ones((CH, CH), np.float32)), BF16)
    head_of_lane = np.arange(LANES) // KA
    ones_bd = jnp.asarray((head_of_lane[:, None] == head_of_lane[None, :]).astype(np.float32), BF16)
    ya, t_fin = pl.pallas_call(
        functools.partial(_wkv_chunk_kernel, ct=ct),
        grid=(nseq // nbatch, t // ct),
        in_specs=[row_spec] * 5 + [_const_spec((1, D))] * 5 + [_const_spec((CH, CH)), _const_spec((LANES, LANES))],
        out_specs=[row_spec, pl.BlockSpec((nbatch, NPAIR, LANES, LANES), lambda n, j: (n, 0, 0, 0))],
        out_shape=[jax.ShapeDtypeStruct((nseq, t, D), F32),
                   jax.ShapeDtypeStruct((nseq, NPAIR, LANES, LANES), F32)],
        scratch_shapes=[pltpu.VMEM((nbatch, NPAIR, LANES, LANES), F32)],
        compiler_params=pltpu.CompilerParams(dimension_semantics=("arbitrary", "arbitrary"),
                                             vmem_limit_bytes=VMEM_LIMIT),
        name="wkv_chunk",
    )(*[z.reshape(nseq, t, D) for z in vecs], k_k, k_a, gain, bias, rk, tri, ones_bd)
    t6 = t_fin.reshape(nseq, NPAIR, 2, KA, 2, KA)
    diag = jnp.stack([t6[:, :, 0, :, 0, :], t6[:, :, 1, :, 1, :]], axis=2)
    return ya.reshape(nseq * t, D), diag.reshape(nseq, HA, KA, KA).transpose(0, 1, 3, 2)


def _log_gamma():
    return [math.log1p(-2.0 ** (-5.0 - h)) for h in range(HB)]


def _rope_tables(pos):
    half = DK // 2
    theta = 1.0 / (ROPE_BASE ** jnp.linspace(0.0, 1.0, half, dtype=F32))
    ang = pos[:, None] * theta[None, :]
    cos, sin = jnp.cos(ang), jnp.sin(ang)
    return jnp.concatenate([cos, cos], axis=1), jnp.concatenate([-sin, sin], axis=1)


def _decay_tables(chunk):
    lg = jnp.asarray(_log_gamma(), F32)
    idx = jnp.arange(chunk, dtype=F32)
    diff = idx[:, None] - idx[None, :]
    dmat = jnp.exp(jnp.maximum(diff, 0.0)[None] * lg[:, None, None]) * (diff >= 0)[None]
    cross = jnp.exp((idx[None, :] + 1.0) * lg[:, None])[:, :, None]
    kdec = jnp.exp((chunk - 1.0 - idx)[None, :] * lg[:, None])[:, :, None]
    sdec = jnp.exp(chunk * lg)
    return dmat.astype(F32), cross, kdec, sdec


def _rotary(x, cos2, sin2):
    return x * cos2 + pltpu.roll(x, DK // 2, axis=1) * sin2


def _ln_rows(y, eps):
    m = jnp.mean(y, axis=-1, keepdims=True)
    yc = y - m
    var = jnp.mean(yc * yc, axis=-1, keepdims=True)
    return yc * lax.rsqrt(var + eps)


def _ret_post_kernel(qk_ref, vb_ref, ga_ref, ya_ref, gb_ref, x_ref, gate_ref, cos_ref, sin_ref,
                     dmat_ref, cross_ref, kdec_ref, sdec_ref, wout_ref, lng_ref, lnb_ref,
                     y_ref, sfin_ref, s_scr, *, tt, alpha):
    j = pl.program_id(1)

    @pl.when(j == 0)
    def _():
        s_scr[...] = jnp.zeros_like(s_scr)

    nb = x_ref.shape[0]
    B = range(nb)
    U = [(b, h) for b in B for h in range(HB)]
    nchunk = tt // RET_CHUNK

    ctx = [dict() for _ in range(nchunk)]

    def rotate(c):
        rows = slice(c * RET_CHUNK, (c + 1) * RET_CHUNK)
        cos2, sin2 = cos_ref[rows, :], sin_ref[rows, :]
        ctx[c]["qb"] = [_rotary(qk_ref[b, rows, h * DK:(h + 1) * DK].astype(F32), cos2, sin2).astype(BF16)
                        for b, h in U]
        yield
        k = [_rotary(qk_ref[b, rows, (HB + h) * DK:(HB + h + 1) * DK].astype(F32), cos2, sin2) for b, h in U]
        ctx[c]["kb"] = [k[u].astype(BF16) for u in range(len(U))]
        yield
        ctx[c]["kd"] = [(k[u] * kdec_ref[h]).astype(BF16) for u, (b, h) in enumerate(U)]
        yield

    def retention(c):
        rows = slice(c * RET_CHUNK, (c + 1) * RET_CHUNK)
        qb, kb, kd = ctx[c]["qb"], ctx[c]["kb"], ctx[c]["kd"]
        v = [vb_ref[b, rows, h * DV:(h + 1) * DV] for b, h in U]
        dn = (((1,), (1,)), ((), ()))
        scores = [lax.dot_general(qb[u], kb[u], dn, preferred_element_type=F32) * dmat_ref[h]
                  for u, (b, h) in enumerate(U)]
        yield
        s = [s_scr[b, h] for b, h in U]
        cross = [jnp.dot(qb[u], s[u].astype(BF16), preferred_element_type=F32) * cross_ref[h]
                 for u, (b, h) in enumerate(U)]
        yield
        upd = [lax.dot_general(kd[u], v[u], (((0,), (0,)), ((), ())), preferred_element_type=F32)
               for u in range(len(U))]
        yield
        inner = [jnp.dot(scores[u].astype(BF16), v[u], preferred_element_type=F32) for u in range(len(U))]
        for u, (b, h) in enumerate(U):
            s_scr[b, h] = s[u] * sdec_ref[h] + upd[u]
        ctx[c]["yb"] = [inner[u] + cross[u] for u in range(len(U))]
        yield

    def project(c):
        rows = slice(c * RET_CHUNK, (c + 1) * RET_CHUNK)
        ob = [_ln_rows(yb, LN_EPS) for yb in ctx[c]["yb"]]
        yield
        lhs = jnp.concatenate(
            [(ga_ref[b, rows, :].astype(F32) * ya_ref[b, rows, :]
              + gb_ref[b, rows, :].astype(F32) * jnp.concatenate(ob[b * HB:(b + 1) * HB], axis=1)).astype(BF16)
             for b in B], axis=0)
        cols = []
        for n0 in range(0, D, 2 * LANES):
            cols.append(jnp.dot(lhs, wout_ref[:, n0:n0 + 2 * LANES], preferred_element_type=F32))
            yield
        sub = jnp.concatenate(cols, axis=1)
        for b in B:
            z = alpha * x_ref[b, rows, :] + gate_ref[b] * sub[b * RET_CHUNK:(b + 1) * RET_CHUNK]
            y_ref[b, rows, :] = _ln_rows(z, LN_EPS) * lng_ref[...] + lnb_ref[...]
        yield

    for c in range(-2, nchunk):
        _interleave(*([project(c)] if c >= 0 else []),
                    *([retention(c + 1)] if 0 <= c + 1 < nchunk else []),
                    *([rotate(c + 2)] if c + 2 < nchunk else []))

    @pl.when(j == pl.num_programs(1) - 1)
    def _():
        sfin_ref[...] = s_scr[...]


def _ret_post(qk, vb, ga, ya, gb, x2, gate, wout, ln_gain, ln_bias, *, nseq, t, tt, nbatch, alpha):
    cos2, sin2 = _rope_tables(jnp.arange(t, dtype=F32) + 0)
    dmat, cross, kdec, sdec = _decay_tables(RET_CHUNK)
    key_scale = DK ** -0.5
    row_spec = pl.BlockSpec((nbatch, tt, D), lambda n, j: (n, j, 0))
    tab_spec = pl.BlockSpec((tt, DK), lambda n, j: (j, 0))
    rows3 = lambda z: z.reshape(nseq, t, D)
    y, s_fin = pl.pallas_call(
        functools.partial(_ret_post_kernel, tt=tt, alpha=alpha),
        grid=(nseq // nbatch, t // tt),
        in_specs=[row_spec] * 6 + [pl.BlockSpec((nbatch, 1, D), lambda n, j: (n, 0, 0)), tab_spec, tab_spec,
                                   _const_spec((HB, RET_CHUNK, RET_CHUNK)), _const_spec((HB, RET_CHUNK, 1)),
                                   _const_spec((HB, RET_CHUNK, 1)), pl.BlockSpec(memory_space=pltpu.SMEM),
                                   _const_spec((D, D)), _const_spec((1, D)), _const_spec((1, D))],
        out_specs=[row_spec, pl.BlockSpec((nbatch, HB, DK, DV), lambda n, j: (n, 0, 0, 0))],
        out_shape=[jax.ShapeDtypeStruct((nseq, t, D), F32),
                   jax.ShapeDtypeStruct((nseq, HB, DK, DV), F32)],
        scratch_shapes=[pltpu.VMEM((nbatch, HB, DK, DV), F32)],
        compiler_params=pltpu.CompilerParams(dimension_semantics=("arbitrary", "arbitrary"),
                                             vmem_limit_bytes=VMEM_LIMIT),
        name="ret_post",
    )(rows3(qk), rows3(vb), rows3(ga), rows3(ya), rows3(gb), rows3(x2), gate, cos2, sin2,
      dmat * key_scale, cross, kdec * key_scale, sdec, wout, ln_gain, ln_bias)
    return y.reshape(nseq * t, D), s_fin


def _ret_sample_kernel(qk_ref, vb_ref, cos_ref, sin_ref, s_ref, dmat_ref, cross_ref, kdec_ref, sdec_ref,
                       o_ref, snew_ref, *, nb):
    cos2, sin2 = cos_ref[...], sin_ref[...]
    U = [(n, h) for n in range(nb) for h in range(HB)]
    idx = range(len(U))
    qb = [_rotary(qk_ref[:, n, h * DK:(h + 1) * DK], cos2, sin2).astype(BF16) for n, h in U]
    k = [_rotary(qk_ref[:, n, (HB + h) * DK:(HB + h + 1) * DK], cos2, sin2) * (DK ** -0.5) for n, h in U]
    v = [vb_ref[:, n, h * DV:(h + 1) * DV].astype(BF16) for n, h in U]
    s = [s_ref[n, h] for n, h in U]
    cross = [jnp.dot(qb[u], s[u].astype(BF16), preferred_element_type=F32) * cross_ref[U[u][1]] for u in idx]
    scores = [_bdot_nt(qb[u], k[u]) * dmat_ref[U[u][1]] for u in idx]
    upd = [_bdot_tn(k[u] * kdec_ref[U[u][1]], v[u]) for u in idx]
    inner = [jnp.dot(scores[u].astype(BF16), v[u], preferred_element_type=F32) for u in idx]
    for u, (n, h) in enumerate(U):
        snew_ref[n, h] = s[u] * sdec_ref[h] + upd[u]
        o_ref[:, n, h * DV:(h + 1) * DV] = _ln_rows(inner[u] + cross[u], LN_EPS)


def _ret_sample(qk3, vb3, s0, pos0, *, nb):
    t, nseq, _ = qk3.shape
    cos2, sin2 = _rope_tables(jnp.arange(t, dtype=F32) + pos0)
    dmat, cross, kdec, sdec = _decay_tables(t)
    row_spec = pl.BlockSpec((t, nb, D), lambda i: (0, i, 0))
    st_spec = pl.BlockSpec((nb, HB, DK, DV), lambda i: (i, 0, 0, 0))
    return pl.pallas_call(
        functools.partial(_ret_sample_kernel, nb=nb),
        grid=(nseq // nb,),
        in_specs=[row_spec, row_spec, _const_spec((t, DK)), _const_spec((t, DK)), st_spec,
                  _const_spec((HB, t, t)), _const_spec((HB, t, 1)), _const_spec((HB, t, 1)),
                  pl.BlockSpec(memory_space=pltpu.SMEM)],
        out_specs=[row_spec, st_spec],
        out_shape=[jax.ShapeDtypeStruct(qk3.shape, F32), jax.ShapeDtypeStruct(s0.shape, F32)],
        compiler_params=pltpu.CompilerParams(dimension_semantics=("arbitrary",),
                                             vmem_limit_bytes=VMEM_LIMIT),
        name="ret_sample",
    )(qk3, vb3, cos2, sin2, s0, dmat, cross, kdec, sdec)


def _unpair_tokens(ya_ref, ya_scr, pair_rows):
    half = ya_ref.shape[0]
    groups = half // pair_rows
    low = lax.broadcasted_iota(jnp.int32, (half, LANES), 1) < KA
    for m in range(D // LANES):
        ha = ya_ref[:, 2 * m, :]
        hb = ya_ref[:, 2 * m + 1, :]
        lo = jnp.where(low, ha, pltpu.roll(hb, KA, axis=1))
        hi = jnp.where(low, pltpu.roll(ha, KA, axis=1), hb)
        ya_scr[:, 0, :, m * LANES:(m + 1) * LANES] = lo.reshape(groups, pair_rows, LANES)
        ya_scr[:, 1, :, m * LANES:(m + 1) * LANES] = hi.reshape(groups, pair_rows, LANES)
    return ya_scr[...].reshape(2 * half, D)


def _post_kernel(ga_ref, ya_ref, gb_ref, ob_ref, x_ref, gate_ref, wout_ref, lng_ref, lnb_ref, y_ref,
                 ya_scr, *, gs, alpha, pair_rows):
    merged = ga_ref[...] * _unpair_tokens(ya_ref, ya_scr, pair_rows) + gb_ref[...] * ob_ref[...]
    sub = jnp.dot(merged.astype(BF16), wout_ref[...], preferred_element_type=F32)
    x = x_ref[...]
    gate = gate_ref[...]
    if gs == 1:
        z = alpha * x + gate * sub
    else:
        tt = x.shape[0]
        z = (alpha * x.reshape(tt // gs, gs, D) + gate[None] * sub.reshape(tt // gs, gs, D)).reshape(tt, D)
    y_ref[...] = _ln_rows(z, LN_EPS) * lng_ref[...] + lnb_ref[...]


def _post(ga, ya, gb, ob, x2, gate, wout, ln_gain, ln_bias, *, nseq_grid, tiles, tt, gs, alpha, pair_rows):
    row_spec = pl.BlockSpec((tt, D), lambda n, j: (n * tiles + j, 0))
    ya_spec = pl.BlockSpec((tt // 2, HA, LANES), lambda n, j: (n * tiles + j, 0, 0))
    scratch = [pltpu.VMEM((tt // 2 // pair_rows, 2, pair_rows, D), F32)]
    return pl.pallas_call(
        functools.partial(_post_kernel, gs=gs, alpha=alpha, pair_rows=pair_rows),
        grid=(nseq_grid, tiles),
        in_specs=[row_spec, ya_spec, row_spec, row_spec, row_spec,
                  pl.BlockSpec((None, gs, D), lambda n, j: (n, 0, 0)),
                  _const_spec((D, D)), _const_spec((1, D)), _const_spec((1, D))],
        out_specs=row_spec,
        out_shape=jax.ShapeDtypeStruct(x2.shape, F32),
        scratch_shapes=scratch,
        compiler_params=pltpu.CompilerParams(dimension_semantics=("arbitrary", "arbitrary"),
                                             vmem_limit_bytes=VMEM_LIMIT),
        name="post",
    )(ga, ya, gb, ob, x2, gate, wout, ln_gain, ln_bias)


def _head_param_lanes(p, nseq):
    return jnp.tile(p.reshape(HA, KA).T, (1, nseq))


def _layer(x2, ada, first_prev, s_wkv, s_ret, pos0, weights, *, nseq, t, seq_major, tt, tc, tp, depth, wkv_seqs=1):
    (wsh, wrest, pre_params, gn_gain, gn_bias, r_k, wout, ln_gain, ln_bias) = weights
    if seq_major:
        nseq_grid, gs, tiles = nseq, 1, t // tt
    else:
        nseq_grid, gs, tiles = 1, nseq, (t * nseq) // tt
    shift_c = ada[:, :D].reshape(nseq_grid, gs, D)
    scale_c = ada[:, D:2 * D].reshape(nseq_grid, gs, D)
    gate_c = ada[:, 2 * D:].reshape(nseq_grid, gs, D)
    cfg = dict(nseq_grid=nseq_grid, tiles=tiles, tt=tt, gs=gs)

    *vecs, ga, gb, qk, vb, new_shift = _pre(
        x2, scale_c, shift_c, first_prev.reshape(nseq_grid, gs, SHIFT_W), wsh, wrest, pre_params,
        pair_out=not seq_major, gate_dtype=BF16 if seq_major else F32, **cfg)

    if seq_major:
        row = lambda p: p.reshape(1, D)
        ya, new_wkv = _wkv_chunked(vecs, pre_params[3], pre_params[4], row(gn_gain), row(gn_bias), row(r_k),
                                   nseq=nseq, t=t, ct=tc, nbatch=wkv_seqs)
        pair_rows = None
    else:
        s0 = s_wkv.transpose(3, 2, 0, 1).reshape(KA, KA, nseq * HA)
        ya, s_fin = _wkv(vecs, s0, _head_param_lanes(gn_gain, nseq), _head_param_lanes(gn_bias, nseq),
                         _head_param_lanes(r_k.reshape(-1), nseq), nseq=nseq, t=t, tc=tc)
        pair_rows = (tc // 2) * gs
        new_wkv = s_fin.reshape(KA, KA, nseq, HA).transpose(2, 3, 1, 0)

    alpha = (2.0 * depth) ** 0.25
    if seq_major:
        y, new_ret = _ret_post(qk, vb, ga, ya, gb, x2, gate_c, wout, ln_gain, ln_bias,
                               nseq=nseq, t=t, tt=tp, nbatch=wkv_seqs, alpha=alpha)
    else:
        ob3, new_ret = _ret_sample(qk.reshape(t, nseq, D), vb.reshape(t, nseq, D), s_ret, pos0, nb=8)
        y = _post(ga, ya, gb, ob3.reshape(t * nseq, D), x2, gate_c, wout, ln_gain, ln_bias, alpha=alpha,
                  pair_rows=pair_rows, nseq_grid=nseq_grid, tiles=(t * nseq) // tp, tt=tp, gs=gs)
    return y, new_shift.reshape(nseq, SHIFT_W), new_wkv, new_ret


def kernel(x_prompt, x_sample, c_prompt, c_sample, state_shift, state_wkv, state_ret, w_ada, b_ada, w_in,
           mu_shift, w0, w_decay_up, a0, w_icl_up, k_k, k_a, r_k, gn_a_gain, gn_a_bias, w_out, ln_gain, ln_bias):
    depth = w_in.shape[0]
    nb, t_p, _ = x_prompt.shape
    ns, t_s, _ = x_sample.shape
    past_len = 16384
    h_p = x_prompt.reshape(nb * t_p, D)
    h_s = x_sample.transpose(1, 0, 2).reshape(t_s * ns, D)
    outs = [[] for _ in range(6)]
    zero_lora = jnp.zeros((LORA, D), F32)
    for l in range(depth):
        row = lambda p: p[l].reshape(1, -1)
        wsh, wrest = _cast_w_in(w_in, l)
        wd_pad = jnp.concatenate([w_decay_up[l], zero_lora], axis=0).astype(BF16)
        wa_pad = jnp.concatenate([zero_lora, w_icl_up[l]], axis=0).astype(BF16)
        pre_params = (row(mu_shift), row(w0), row(a0), row(k_k), row(k_a), wd_pad, wa_pad)
        weights = (wsh, wrest, pre_params, gn_a_gain[l], gn_a_bias[l], r_k[l], w_out[l].astype(BF16),
                   row(ln_gain), row(ln_bias))
        ada = _ada(jnp.concatenate([c_prompt, c_sample], axis=0), w_ada[l], b_ada[l])
        h_p, s1, s2, s3 = _layer(
            h_p, ada[:nb], jnp.zeros((nb, SHIFT_W), F32), None, None, 0,
            weights, nseq=nb, t=t_p, seq_major=True, depth=depth, **PROMPT_TILES)
        h_s, t1, t2, t3 = _layer(
            h_s, ada[nb:], state_shift[l], state_wkv[l], state_ret[l], past_len,
            weights, nseq=ns, t=t_s, seq_major=False, tt=ns, tc=t_s, tp=t_s * ns, depth=depth)
        for lst, val in zip(outs, (s1, s2, s3, t1, t2, t3)):
            lst.append(val)
    y_p = h_p.reshape(nb, t_p, D)
    y_s = h_s.reshape(t_s, ns, D).transpose(1, 0, 2)
    return (y_p, y_s) + tuple(jnp.stack(o) for o in outs)
```

```python
import functools
import math

import numpy as np
import jax
import jax.numpy as jnp
from jax import lax
from jax.experimental import pallas as pl
from jax.experimental.pallas import tpu as pltpu

F32 = jnp.float32
BF16 = jnp.bfloat16

D = 1024
HA = 16
KA = 64
LORA = 64
GN_EPS = 64e-5
HB = 4
DK = 128
DV = 256
RET_CHUNK = 128
PAST_LEN = 16384
ROPE_BASE = 10000.0
LN_EPS = 1e-5
SHIFT_W = 3 * D + 2 * LORA
REST_W = 6 * D
LANES = 128
SUBLANES = 8
VMEM_LIMIT = 56 * 1024 * 1024
PROMPT_TILES = dict(tt=256, tc=256, wkv_seqs=2, tp=256)


def _sigmoid(x):
    return 1.0 / (1.0 + jnp.exp(-x))


def _log_decay(w_pre):
    zw = -w_pre
    softplus = jnp.maximum(zw, 0.0) + jnp.log(1.0 + jnp.exp(-jnp.abs(zw)))
    return -jnp.exp(-softplus - 0.5)


def _bdot(a, b):
    return jnp.dot(a.astype(BF16), b.astype(BF16), preferred_element_type=F32)


def _bdot_nt(a, b):
    return lax.dot_general(a.astype(BF16), b.astype(BF16), (((1,), (1,)), ((), ())), preferred_element_type=F32)


def _bdot_tn(a, b):
    return lax.dot_general(a.astype(BF16), b.astype(BF16), (((0,), (0,)), ((), ())), preferred_element_type=F32)


def _split_bf16(x, parts):
    out = []
    for _ in range(parts):
        term = x.astype(BF16)
        out.append(term)
        x = x - term.astype(F32)
    return out


def _interleave(*gens):
    live = list(gens)
    while live:
        for gen in list(live):
            try:
                next(gen)
            except StopIteration:
                live.remove(gen)


def _const_spec(shape):
    nd = len(shape)
    return pl.BlockSpec(shape, lambda *_: (0,) * nd, pipeline_mode=pl.Buffered(1))


def _cast_w_in_kernel(w_ref, sh_ref, rest_ref):
    sh_ref[...] = w_ref[:, :SHIFT_W].astype(BF16)
    rest_ref[...] = w_ref[:, SHIFT_W:].astype(BF16)


def _cast_w_in(w_in, layer):
    rb = 256
    return pl.pallas_call(
        _cast_w_in_kernel,
        grid=(D // rb,),
        in_specs=[pl.BlockSpec((None, rb, SHIFT_W + REST_W), lambda i: (layer, i, 0))],
        out_specs=[pl.BlockSpec((rb, SHIFT_W), lambda i: (i, 0)), pl.BlockSpec((rb, REST_W), lambda i: (i, 0))],
        out_shape=[jax.ShapeDtypeStruct((D, SHIFT_W), BF16), jax.ShapeDtypeStruct((D, REST_W), BF16)],
        compiler_params=pltpu.CompilerParams(dimension_semantics=("arbitrary",)),
        name="cast_w_in",
    )(w_in)


def _ada_kernel(c_ref, w_ref, b_ref, o_ref):
    c = c_ref[...]
    s = c * _sigmoid(c)
    o_ref[...] = jnp.dot(s, w_ref[...], preferred_element_type=F32,
                         precision=lax.Precision.HIGHEST) + b_ref[...]


def _ada(c_all, w_ada, b_ada):
    n = c_all.shape[0]
    return pl.pallas_call(
        _ada_kernel,
        grid=(3,),
        in_specs=[pl.BlockSpec((n, D), lambda j: (0, 0)),
                  pl.BlockSpec((D, D), lambda j: (0, j)),
                  pl.BlockSpec((1, D), lambda j: (0, j))],
        out_specs=pl.BlockSpec((n, D), lambda j: (0, j)),
        out_shape=jax.ShapeDtypeStruct((n, 3 * D), F32),
        compiler_params=pltpu.CompilerParams(dimension_semantics=("arbitrary",)),
        name="ada",
    )(c_all, w_ada, b_ada.reshape(1, 3 * D))


def _modulate(x, scale, shift, gs):
    if gs == 1:
        return x * (1.0 + scale) + shift
    tt = x.shape[0]
    x3 = x.reshape(tt // gs, gs, D)
    return (x3 * (1.0 + scale)[None] + shift[None]).reshape(tt, D)


def _put_head_pairs(out_ref, scr, x, y):
    tt = x.shape[0]
    low = lax.broadcasted_iota(jnp.int32, (tt, LANES), 1) < KA
    for m in range(D // LANES):
        xc = x[:, m * LANES:(m + 1) * LANES]
        yc = y[:, m * LANES:(m + 1) * LANES]
        even = jnp.where(low, xc, pltpu.roll(yc, KA, axis=1))
        odd = jnp.where(low, pltpu.roll(xc, KA, axis=1), yc)
        for r in range(tt // SUBLANES):
            rows = slice(r * SUBLANES, (r + 1) * SUBLANES)
            scr[pl.ds((r * HA + 2 * m) * SUBLANES, SUBLANES), :] = even[rows]
            scr[pl.ds((r * HA + 2 * m + 1) * SUBLANES, SUBLANES), :] = odd[rows]

    def row_block(r, carry):
        for s in range(SUBLANES):
            out_ref[r * SUBLANES + s] = scr[pl.ds(r * (HA * SUBLANES) + s, HA, stride=SUBLANES), :]
        return carry

    lax.fori_loop(0, tt // SUBLANES, row_block, 0)


def _pre_kernel(x_ref, scale_ref, shift_ref, prev_ref, wsh_ref, wrest_ref, mu_ref, w0_ref, a0_ref,
                kk_ref, ka_ref, wd_ref, wa_ref, *rest, gs, tt, pad, pair_out):
    if pair_out:
        rk_out, wk_out, va_out, gate_out, ns_out, p_scr, pair_scr = rest
    else:
        vec_out, gate_out, ns_out, p_scr = rest
        r_out, k_out, v_out, wpre_out, apre_out = [vec_out.at[:, i * D:(i + 1) * D] for i in range(5)]
    ga_out, gb_out, qk_out, vb_out = [gate_out.at[:, i * D:(i + 1) * D] for i in range(4)]
    j = pl.program_id(1)
    nj = pl.num_programs(1)
    lo = pad - gs

    @pl.when(j == 0)
    def _():
        p_scr[lo:pad, :] = prev_ref[...]

    u = _modulate(x_ref[...], scale_ref[...], shift_ref[...], gs).astype(BF16)

    def mixed(c0, c1):
        p_scr[pad:pad + tt, c0:c1] = jnp.dot(u, wsh_ref[:, c0:c1], preferred_element_type=F32)
        p = p_scr[pad:pad + tt, c0:c1]
        prev = p_scr[lo:lo + tt, c0:c1]
        return p + (prev - p) * mu_ref[:, c0:c1]

    def proj(c0, c1):
        return jnp.dot(u, wrest_ref[:, c0:c1], preferred_element_type=F32)

    def branch_a():
        lora_in = mixed(3 * D, SHIFT_W)
        yield
        lora_w = _bdot(jnp.tanh(lora_in), wd_ref[...])
        lora_a = _bdot(lora_in, wa_ref[...])
        w_pre = w0_ref[...] + lora_w
        a_pre = a0_ref[...] + lora_a
        yield
        k = mixed(D, 2 * D)
        yield
        if pair_out:
            a = _sigmoid(a_pre)
            _put_head_pairs(rk_out, pair_scr.at[0], mixed(0, D), k * (1.0 + (a - 1.0) * ka_ref[...]))
            yield
            _put_head_pairs(wk_out, pair_scr.at[1], jnp.exp(_log_decay(w_pre)), k * kk_ref[...])
            _put_head_pairs(va_out, pair_scr.at[2], mixed(2 * D, 3 * D), a)
        else:
            wpre_out[...] = w_pre
            apre_out[...] = a_pre
            k_out[...] = k
            r_out[...] = mixed(0, D)
            yield
            v_out[...] = mixed(2 * D, 3 * D)
        yield

    def gates():
        z_a = proj(0, D)
        yield
        g_a = proj(4 * D, 5 * D)
        ga_out[...] = (_sigmoid(g_a) * (z_a * _sigmoid(z_a))).astype(ga_out.dtype)
        yield
        z_b = proj(3 * D, 4 * D)
        yield
        g_b = proj(5 * D, 6 * D)
        gb_out[...] = (_sigmoid(g_b) * (z_b * _sigmoid(z_b))).astype(gb_out.dtype)
        yield
        qk_out[...] = proj(D, 2 * D).astype(qk_out.dtype)
        yield
        vb_out[...] = proj(2 * D, 3 * D).astype(vb_out.dtype)
        yield

    _interleave(branch_a(), gates())

    @pl.when(j == nj - 1)
    def _():
        ns_out[...] = p_scr[pad + tt - gs:pad + tt, :]

    p_scr[lo:pad, :] = p_scr[pad + tt - gs:pad + tt, :]


def _pre(x2, scale, shift, first_prev, wsh, wrest, params, *, nseq_grid, tiles, tt, gs, pair_out, gate_dtype):
    rows = x2.shape[0]
    pad = max(SUBLANES, gs)
    mu, w0, a0, k_k, k_a, wd_pad, wa_pad = params
    row_spec = pl.BlockSpec((tt, D), lambda n, j: (n * tiles + j, 0))
    seq_spec = pl.BlockSpec((None, gs, D), lambda n, j: (n, 0, 0))
    scratch = [pltpu.VMEM((pad + tt, SHIFT_W), F32)]
    if pair_out:
        vec_specs = [pl.BlockSpec((tt, HA, LANES), lambda n, j: (n * tiles + j, 0, 0))] * 3
        vec_sds = [jax.ShapeDtypeStruct((rows, HA, LANES), F32)] * 3
        scratch.append(pltpu.VMEM((3, tt * HA, LANES), F32))
    else:
        vec_specs = [pl.BlockSpec((tt, 5 * D), lambda n, j: (n * tiles + j, 0))]
        vec_sds = [jax.ShapeDtypeStruct((rows, 5 * D), F32)]
    gate_spec = pl.BlockSpec((tt, 4 * D), lambda n, j: (n * tiles + j, 0))
    gate_sds = jax.ShapeDtypeStruct((rows, 4 * D), gate_dtype)
    kern = functools.partial(_pre_kernel, gs=gs, tt=tt, pad=pad, pair_out=pair_out)
    return pl.pallas_call(
        kern,
        grid=(nseq_grid, tiles),
        in_specs=[row_spec, seq_spec, seq_spec,
                  pl.BlockSpec((None, gs, SHIFT_W), lambda n, j: (n, 0, 0)),
                  _const_spec((D, SHIFT_W)), _const_spec((D, REST_W)), _const_spec((1, SHIFT_W)),
                  _const_spec((1, D)), _const_spec((1, D)), _const_spec((1, D)), _const_spec((1, D)),
                  _const_spec((LANES, D)), _const_spec((LANES, D))],
        out_specs=vec_specs + [gate_spec, pl.BlockSpec((None, gs, SHIFT_W), lambda n, j: (n, 0, 0))],
        out_shape=vec_sds + [gate_sds, jax.ShapeDtypeStruct((nseq_grid, gs, SHIFT_W), F32)],
        scratch_shapes=scratch,
        compiler_params=pltpu.CompilerParams(dimension_semantics=("arbitrary", "arbitrary"),
                                             vmem_limit_bytes=VMEM_LIMIT),
        name="pre",
    )(x2, scale, shift, first_prev, wsh, wrest, mu, w0, a0, k_k, k_a, wd_pad, wa_pad)


V_KKN, V_B, V_W, V_K, V_R, V_V = range(6)


def _wkv_kernel(rk_ref, wk_ref, va_ref, s0_ref, gain_ref, bias_ref, rkp_ref,
                o_ref, sfin_ref, s_scr, vec_scr, row_scr, y_scr, cur_scr, *, tc):
    j = pl.program_id(1)

    @pl.when(j == 0)
    def _():
        s_scr[...] = s0_ref[...]

    def load_pair(ref, t):
        xt = ref[t].reshape(LANES, LANES).T
        return xt[:KA], xt[KA:]

    def prep(t, carry):
        dec, kkraw = load_pair(wk_ref, t)
        ss = jnp.sum(kkraw * kkraw, axis=0, keepdims=True)
        kkn = kkraw * (1.0 / jnp.maximum(jnp.sqrt(ss), 1e-12))
        v, a = load_pair(va_ref, t)
        r, kmod = load_pair(rk_ref, t)
        vec_scr[t, V_KKN] = kkn
        vec_scr[t, V_B] = kkn * a
        vec_scr[t, V_W] = dec
        vec_scr[t, V_K] = kmod
        vec_scr[t, V_R] = r
        vec_scr[t, V_V] = v
        rk = jnp.sum(r * kmod * rkp_ref[...], axis=0, keepdims=True)
        row_scr[t] = jnp.broadcast_to(rk, (SUBLANES, LANES))
        return carry

    def state_pass(t, carry):
        cur_scr[...] = vec_scr[t]
        skk = jnp.zeros((KA, LANES), F32)
        for k in range(KA):
            skk = skk + s_scr[k] * cur_scr[V_KKN, k:k + 1, :]
        vv = cur_scr[V_V]
        y = jnp.zeros((KA, LANES), F32)
        for k in range(KA):
            s_new = (s_scr[k] * cur_scr[V_W, k:k + 1, :] - skk * cur_scr[V_B, k:k + 1, :]
                     + vv * cur_scr[V_K, k:k + 1, :])
            s_scr[k] = s_new
            y = y + s_new * cur_scr[V_R, k:k + 1, :]
        y_scr[t] = y
        return carry

    def out_token(t):
        y = y_scr[t]
        m = jnp.mean(y, axis=0, keepdims=True)
        yc = y - m
        var = jnp.mean(yc * yc, axis=0, keepdims=True)
        yn = yc * lax.rsqrt(var + GN_EPS) * gain_ref[...] + bias_ref[...]
        return yn + row_scr[t, 0:1, :] * vec_scr[t, V_V]

    def finish(i, carry):
        o2 = jnp.concatenate([out_token(i), out_token(i + tc // 2)], axis=0)
        o_ref[i] = o2.T.reshape(LANES // HA, HA, LANES)
        return carry

    lax.fori_loop(0, tc, prep, 0, unroll=2)
    lax.fori_loop(0, tc, state_pass, 0)
    lax.fori_loop(0, tc // 2, finish, 0, unroll=2)

    @pl.when(j == pl.num_programs(1) - 1)
    def _():
        sfin_ref[...] = s_scr[...]


def _wkv(pairs, s0, gain_l, bias_l, rk_l, *, nseq, t, tc):
    nsl = LANES // HA
    groups = nseq // nsl
    shape4, oshape4 = (t, nseq, HA, LANES), (t // 2, nseq, HA, LANES)
    vec_spec = pl.BlockSpec((tc, nsl, HA, LANES), lambda g, j: (j, g, 0, 0))
    out_spec = pl.BlockSpec((tc // 2, nsl, HA, LANES), lambda g, j: (j, g, 0, 0))
    st_spec = pl.BlockSpec((KA, KA, LANES), lambda g, j: (0, 0, g))
    par_spec = pl.BlockSpec((KA, LANES), lambda g, j: (0, g))
    o4, s_fin = pl.pallas_call(
        functools.partial(_wkv_kernel, tc=tc),
        grid=(groups, t // tc),
        in_specs=[vec_spec] * 3 + [st_spec, par_spec, par_spec, par_spec],
        out_specs=[out_spec, st_spec],
        out_shape=[jax.ShapeDtypeStruct(oshape4, F32),
                   jax.ShapeDtypeStruct((KA, KA, nseq * HA), F32)],
        scratch_shapes=[pltpu.VMEM((KA, KA, LANES), F32), pltpu.VMEM((tc, 6, KA, LANES), F32),
                        pltpu.VMEM((tc, SUBLANES, LANES), F32), pltpu.VMEM((tc, KA, LANES), F32),
                        pltpu.VMEM((6, KA, LANES), F32)],
        compiler_params=pltpu.CompilerParams(dimension_semantics=("arbitrary", "arbitrary"),
                                             vmem_limit_bytes=VMEM_LIMIT),
        name="wkv",
    )(*[z.reshape(shape4) for z in pairs], s0, gain_l, bias_l, rk_l)
    return o4.reshape(nseq * t // 2, HA, LANES), s_fin


CH = 64
NPAIR = D // LANES


def _wkv_chunk_kernel(r_ref, k_ref, v_ref, wpre_ref, apre_ref, kk_ref, ka_ref, gain_ref, bias_ref, rk_ref,
                      tri_ref, ones_ref, o_ref, tfin_ref, t_scr, *, ct):
    j = pl.program_id(1)

    @pl.when(j == 0)
    def _():
        t_scr[...] = jnp.zeros_like(t_scr)

    lane = lax.broadcasted_iota(jnp.int32, (CH, LANES), 1)
    row = lax.broadcasted_iota(jnp.int32, (CH, LANES), 0)
    low = lane < KA
    col = jnp.bitwise_and(lane, KA - 1)
    strict = row > col
    incl = row >= col
    eye = (row == col).astype(F32)
    low2 = lax.broadcasted_iota(jnp.int32, (LANES, LANES), 1) < KA
    diag_blocks = (lax.broadcasted_iota(jnp.int32, (LANES, LANES), 0) < KA) == low2
    tri = tri_ref[...]
    ones_bd = ones_ref[...]

    def bd(z):
        return jnp.concatenate([jnp.where(low, z, 0.0), jnp.where(low, 0.0, z)], axis=0)

    def abd(z):
        return jnp.concatenate([jnp.where(low, 0.0, z), jnp.where(low, z, 0.0)], axis=0)

    def seg_sum_pairs(xs):
        tot = _bdot(jnp.concatenate(xs, axis=0), ones_bd)
        return [tot[i * CH:(i + 1) * CH] for i in range(len(xs))]

    nbatch = r_ref.shape[0]
    B = range(nbatch)
    P = range(nbatch * NPAIR)
    seq = [i // NPAIR for i in P]
    pair = [i % NPAIR for i in P]
    ln = [slice(pair[i] * LANES, (pair[i] + 1) * LANES) for i in P]

    def chunk_prep(c0, ctx):
        rows = slice(c0, c0 + CH)
        lw_all = [_log_decay(wpre_ref[b, rows, :]) for b in B]
        cs_all = [sum(jnp.dot(tri, term, preferred_element_type=F32) for term in _split_bf16(lw_all[b], 2))
                  for b in B]
        yield
        lw = [lw_all[seq[m]][:, ln[m]] for m in P]
        cs = [cs_all[seq[m]][:, ln[m]] for m in P]
        k_in = [k_ref[seq[m], rows, ln[m]] for m in P]
        kkraw = [k_in[m] * kk_ref[:, ln[m]] for m in P]
        ss = seg_sum_pairs([kkraw[m] * kkraw[m] for m in P])
        yield
        a = [_sigmoid(apre_ref[seq[m], rows, ln[m]]) for m in P]
        kkn = [kkraw[m] / jnp.maximum(jnp.sqrt(ss[m]), 1e-12) for m in P]
        b = [kkn[m] * a[m] for m in P]
        r = [r_ref[seq[m], rows, ln[m]] for m in P]
        k = [k_in[m] * (1.0 + (a[m] - 1.0) * ka_ref[:, ln[m]]) for m in P]
        v = [v_ref[seq[m], rows, ln[m]] for m in P]
        cs_last = [cs[m][CH - 1:CH, :] for m in P]
        e_neg = [jnp.exp(-cs[m]) for m in P]
        a_t = [-kkn[m] * jnp.exp(cs[m] - lw[m]) for m in P]
        r_t = [r[m] * jnp.exp(cs[m]) for m in P]
        b_t = [b[m] * e_neg[m] for m in P]
        k_t = [k[m] * e_neg[m] for m in P]
        lhs = [jnp.concatenate([a_t[m], r_t[m]], axis=0) for m in P]
        out0 = [_bdot_nt(jnp.where(low2, lhs[m], 0.0), jnp.concatenate([b_t[m], k_t[m]], axis=0)) for m in P]
        yield
        out1 = [_bdot_nt(jnp.where(low2, 0.0, lhs[m]), jnp.concatenate([k_t[m], b_t[m]], axis=0)) for m in P]
        yield
        l_cat = [jnp.where(strict, jnp.where(low, out0[m][:CH], out1[m][:CH]), 0.0) for m in P]
        ak_cat = [jnp.where(strict, jnp.where(low, out1[m][:CH], out0[m][:CH]), 0.0) for m in P]
        rb_cat = [jnp.where(incl, jnp.where(low, out0[m][CH:], out1[m][CH:]), 0.0) for m in P]
        rk_cat = [jnp.where(incl, jnp.where(low, out1[m][CH:], out0[m][CH:]), 0.0) for m in P]
        rkv = seg_sum_pairs([r[m] * k[m] * rk_ref[:, ln[m]] for m in P])
        e_end = [jnp.exp(cs_last[m] - cs[m]) for m in P]
        ctx.update(
            rows=rows, l_cat=l_cat, v=v, rkv=rkv,
            g_lhs=[jnp.concatenate([a_t[m], ak_cat[m]], axis=1) for m in P],
            y_lhs=[jnp.concatenate([r_t[m], rb_cat[m], rk_cat[m]], axis=1) for m in P],
            upd_lhs=[jnp.concatenate([b[m] * e_end[m], k[m] * e_end[m]], axis=0) for m in P],
            p_end=[jnp.exp(cs_last[m]) for m in P])
        yield

    def chunk_inverse(ctx):
        l_cat = ctx["l_cat"]
        x = [eye + l_cat[m] for m in P]
        lp = [_bdot(l_cat[m], bd(l_cat[m])) for m in P]
        yield
        for _ in range(4):
            both = [_bdot(jnp.concatenate([lp[m], x[m]], axis=0), bd(lp[m])) for m in P]
            yield
            x = [x[m] + both[m][CH:] for m in P]
            lp = [both[m][:CH] for m in P]
        ctx["x"] = [x[m] + _bdot(x[m], bd(lp[m])) for m in P]
        yield

    def chunk_state(ctx):
        x, v = ctx["x"], ctx["v"]
        t0 = [t_scr[seq[m], pair[m]] for m in P]
        v_abd = [abd(v[m]) for m in P]
        g = [_bdot(ctx["g_lhs"][m], jnp.concatenate([t0[m], v_abd[m]], axis=0)) for m in P]
        yield
        u = [_bdot(x[m], bd(g[m])) for m in P]
        yield
        upd = [_bdot_tn(ctx["upd_lhs"][m], jnp.concatenate([u[m], v[m]], axis=0)) for m in P]
        yield
        for m in P:
            p_col = jnp.broadcast_to(ctx["p_end"][m], (LANES, LANES)).T
            t_scr[seq[m], pair[m]] = t0[m] * p_col + jnp.where(diag_blocks, upd[m], 0.0)
        y = [_bdot(ctx["y_lhs"][m], jnp.concatenate([t0[m], bd(u[m]), v_abd[m]], axis=0)) for m in P]
        yield
        mean = seg_sum_pairs(y)
        yield
        yc = [y[m] - mean[m] * (1.0 / KA) for m in P]
        var = seg_sum_pairs([yc[m] * yc[m] for m in P])
        yield
        for m in P:
            yn = yc[m] * lax.rsqrt(var[m] * (1.0 / KA) + GN_EPS) * gain_ref[:, ln[m]] + bias_ref[:, ln[m]]
            o_ref[seq[m], ctx["rows"], ln[m]] = yn + ctx["rkv"][m] * v[m]
        yield

    nchunk = ct // CH
    ctxs = [dict() for _ in range(nchunk)]
    for c in range(-2, nchunk):
        stages = []
        if c >= 0:
            stages.append(chunk_state(ctxs[c]))
        if 0 <= c + 1 < nchunk:
            stages.append(chunk_inverse(ctxs[c + 1]))
        if c + 2 < nchunk:
            stages.append(chunk_prep((c + 2) * CH, ctxs[c + 2]))
        _interleave(*stages)

    @pl.when(j == pl.num_programs(1) - 1)
    def _():
        tfin_ref[...] = t_scr[...]


def _wkv_chunked(vec, k_k, k_a, gain, bias, rk, *, nseq, t, ct, nbatch):
    row_spec = pl.BlockSpec((nbatch, ct, D), lambda n, j: (n, j, 0))
    tri = jnp.asarray(np.tril(np.ones((CH, CH), np.float32)), BF16)
    head_of_lane = np.arange(LANES) // KA
    ones_bd = jnp.asarray((head_of_lane[:, None] == head_of_lane[None, :]).astype(np.float32), BF16)
    ya, t_fin = pl.pallas_call(
        functools.partial(_wkv_chunk_kernel, ct=ct),
        grid=(nseq // nbatch, t // ct),
        in_specs=[pl.BlockSpec((nbatch, ct, D), lambda n, j, i=i: (n, j, i)) for i in range(5)]
        + [_const_spec((1, D))] * 5 + [_const_spec((CH, CH)), _const_spec((LANES, LANES))],
        out_specs=[row_spec, pl.BlockSpec((nbatch, NPAIR, LANES, LANES), lambda n, j: (n, 0, 0, 0))],
        out_shape=[jax.ShapeDtypeStruct((nseq, t, D), F32),
                   jax.ShapeDtypeStruct((nseq, NPAIR, LANES, LANES), F32)],
        scratch_shapes=[pltpu.VMEM((nbatch, NPAIR, LANES, LANES), F32)],
        compiler_params=pltpu.CompilerParams(dimension_semantics=("arbitrary", "arbitrary"),
                                             vmem_limit_bytes=VMEM_LIMIT),
        name="wkv_chunk",
    )(*[vec.reshape(nseq, t, 5 * D)] * 5, k_k, k_a, gain, bias, rk, tri, ones_bd)
    t6 = t_fin.reshape(nseq, NPAIR, 2, KA, 2, KA)
    diag = jnp.stack([t6[:, :, 0, :, 0, :], t6[:, :, 1, :, 1, :]], axis=2)
    return ya.reshape(nseq * t, D), diag.reshape(nseq, HA, KA, KA).transpose(0, 1, 3, 2)


def _log_gamma():
    return [math.log1p(-2.0 ** (-5.0 - h)) for h in range(HB)]


def _rope_tables(pos):
    half = DK // 2
    theta = 1.0 / (ROPE_BASE ** jnp.linspace(0.0, 1.0, half, dtype=F32))
    ang = pos[:, None] * theta[None, :]
    cos, sin = jnp.cos(ang), jnp.sin(ang)
    return jnp.concatenate([cos, cos], axis=1), jnp.concatenate([-sin, sin], axis=1)


def _decay_tables(chunk):
    lg = jnp.asarray(_log_gamma(), F32)
    idx = jnp.arange(chunk, dtype=F32)
    diff = idx[:, None] - idx[None, :]
    dmat = jnp.exp(jnp.maximum(diff, 0.0)[None] * lg[:, None, None]) * (diff >= 0)[None]
    cross = jnp.exp((idx[None, :] + 1.0) * lg[:, None])[:, :, None]
    kdec = jnp.exp((chunk - 1.0 - idx)[None, :] * lg[:, None])[:, :, None]
    sdec = jnp.exp(chunk * lg)
    return dmat.astype(F32), cross, kdec, sdec


def _rotary(x, cos2, sin2):
    return x * cos2 + pltpu.roll(x, DK // 2, axis=1) * sin2


def _ln_rows(y, eps):
    m = jnp.mean(y, axis=-1, keepdims=True)
    yc = y - m
    var = jnp.mean(yc * yc, axis=-1, keepdims=True)
    return yc * lax.rsqrt(var + eps)


def _ret_post_kernel(qk_ref, vb_ref, ga_ref, ya_ref, gb_ref, x_ref, gate_ref, cos_ref, sin_ref,
                     dmat_ref, cross_ref, kdec_ref, sdec_ref, wout_ref, lng_ref, lnb_ref,
                     y_ref, sfin_ref, s_scr, *, tt, alpha):
    j = pl.program_id(1)

    @pl.when(j == 0)
    def _():
        s_scr[...] = jnp.zeros_like(s_scr)

    nb = x_ref.shape[0]
    B = range(nb)
    U = [(b, h) for b in B for h in range(HB)]
    nchunk = tt // RET_CHUNK

    ctx = [dict() for _ in range(nchunk)]

    def rotate(c):
        rows = slice(c * RET_CHUNK, (c + 1) * RET_CHUNK)
        cos2, sin2 = cos_ref[rows, :], sin_ref[rows, :]
        ctx[c]["qb"] = [_rotary(qk_ref[b, rows, h * DK:(h + 1) * DK].astype(F32), cos2, sin2).astype(BF16)
                        for b, h in U]
        yield
        k = [_rotary(qk_ref[b, rows, (HB + h) * DK:(HB + h + 1) * DK].astype(F32), cos2, sin2) for b, h in U]
        ctx[c]["kb"] = [k[u].astype(BF16) for u in range(len(U))]
        yield
        ctx[c]["kd"] = [(k[u] * kdec_ref[h]).astype(BF16) for u, (b, h) in enumerate(U)]
        yield

    def retention(c):
        rows = slice(c * RET_CHUNK, (c + 1) * RET_CHUNK)
        qb, kb, kd = ctx[c]["qb"], ctx[c]["kb"], ctx[c]["kd"]
        v = [vb_ref[b, rows, h * DV:(h + 1) * DV] for b, h in U]
        dn = (((1,), (1,)), ((), ()))
        scores = [lax.dot_general(qb[u], kb[u], dn, preferred_element_type=F32) * dmat_ref[h]
                  for u, (b, h) in enumerate(U)]
        yield
        s = [s_scr[b, h] for b, h in U]
        cross = [jnp.dot(qb[u], s[u].astype(BF16), preferred_element_type=F32) * cross_ref[h]
                 for u, (b, h) in enumerate(U)]
        yield
        upd = [lax.dot_general(kd[u], v[u], (((0,), (0,)), ((), ())), preferred_element_type=F32)
               for u in range(len(U))]
        yield
        inner = [jnp.dot(scores[u].astype(BF16), v[u], preferred_element_type=F32) for u in range(len(U))]
        for u, (b, h) in enumerate(U):
            s_scr[b, h] = s[u] * sdec_ref[h] + upd[u]
        ctx[c]["yb"] = [inner[u] + cross[u] for u in range(len(U))]
        yield

    def project(c):
        rows = slice(c * RET_CHUNK, (c + 1) * RET_CHUNK)
        ob = [_ln_rows(yb, LN_EPS) for yb in ctx[c]["yb"]]
        yield
        lhs = jnp.concatenate(
            [(ga_ref[b, rows, :].astype(F32) * ya_ref[b, rows, :]
              + gb_ref[b, rows, :].astype(F32) * jnp.concatenate(ob[b * HB:(b + 1) * HB], axis=1)).astype(BF16)
             for b in B], axis=0)
        cols = []
        for n0 in range(0, D, 2 * LANES):
            cols.append(jnp.dot(lhs, wout_ref[:, n0:n0 + 2 * LANES], preferred_element_type=F32))
            yield
        sub = jnp.concatenate(cols, axis=1)
        for b in B:
            z = alpha * x_ref[b, rows, :] + gate_ref[b] * sub[b * RET_CHUNK:(b + 1) * RET_CHUNK]
            y_ref[b, rows, :] = _ln_rows(z, LN_EPS) * lng_ref[...] + lnb_ref[...]
        yield

    for c in range(-2, nchunk):
        _interleave(*([project(c)] if c >= 0 else []),
                    *([retention(c + 1)] if 0 <= c + 1 < nchunk else []),
                    *([rotate(c + 2)] if c + 2 < nchunk else []))

    @pl.when(j == pl.num_programs(1) - 1)
    def _():
        sfin_ref[...] = s_scr[...]


def _ret_post(gates, ya, x2, gate, wout, ln_gain, ln_bias, *, nseq, t, tt, nbatch, alpha):
    cos2, sin2 = _rope_tables(jnp.arange(t, dtype=F32) + 0)
    dmat, cross, kdec, sdec = _decay_tables(RET_CHUNK)
    key_scale = DK ** -0.5
    row_spec = pl.BlockSpec((nbatch, tt, D), lambda n, j: (n, j, 0))
    tab_spec = pl.BlockSpec((tt, DK), lambda n, j: (j, 0))
    rows3 = lambda z: z.reshape(nseq, t, D)
    col_spec = lambda i: pl.BlockSpec((nbatch, tt, D), lambda n, j: (n, j, i))
    gates3 = gates.reshape(nseq, t, 4 * D)
    y, s_fin = pl.pallas_call(
        functools.partial(_ret_post_kernel, tt=tt, alpha=alpha),
        grid=(nseq // nbatch, t // tt),
        in_specs=[col_spec(2), col_spec(3), col_spec(0), row_spec, col_spec(1), row_spec,
                  pl.BlockSpec((nbatch, 1, D), lambda n, j: (n, 0, 0)), tab_spec, tab_spec,
                                   _const_spec((HB, RET_CHUNK, RET_CHUNK)), _const_spec((HB, RET_CHUNK, 1)),
                                   _const_spec((HB, RET_CHUNK, 1)), pl.BlockSpec(memory_space=pltpu.SMEM),
                                   _const_spec((D, D)), _const_spec((1, D)), _const_spec((1, D))],
        out_specs=[row_spec, pl.BlockSpec((nbatch, HB, DK, DV), lambda n, j: (n, 0, 0, 0))],
        out_shape=[jax.ShapeDtypeStruct((nseq, t, D), F32),
                   jax.ShapeDtypeStruct((nseq, HB, DK, DV), F32)],
        scratch_shapes=[pltpu.VMEM((nbatch, HB, DK, DV), F32)],
        compiler_params=pltpu.CompilerParams(dimension_semantics=("arbitrary", "arbitrary"),
                                             vmem_limit_bytes=VMEM_LIMIT),
        name="ret_post",
    )(gates3, gates3, gates3, rows3(ya), gates3, rows3(x2), gate, cos2, sin2,
      dmat * key_scale, cross, kdec * key_scale, sdec, wout, ln_gain, ln_bias)
    return y.reshape(nseq * t, D), s_fin


def _ret_sample_kernel(qk_ref, vb_ref, cos_ref, sin_ref, s_ref, dmat_ref, cross_ref, kdec_ref, sdec_ref,
                       o_ref, snew_ref, *, nb):
    cos2, sin2 = cos_ref[...], sin_ref[...]
    U = [(n, h) for n in range(nb) for h in range(HB)]
    idx = range(len(U))
    qb = [_rotary(qk_ref[:, n, h * DK:(h + 1) * DK], cos2, sin2).astype(BF16) for n, h in U]
    k = [_rotary(qk_ref[:, n, (HB + h) * DK:(HB + h + 1) * DK], cos2, sin2) * (DK ** -0.5) for n, h in U]
    v = [vb_ref[:, n, h * DV:(h + 1) * DV].astype(BF16) for n, h in U]
    s = [s_ref[n, h] for n, h in U]
    cross = [jnp.dot(qb[u], s[u].astype(BF16), preferred_element_type=F32) * cross_ref[U[u][1]] for u in idx]
    scores = [_bdot_nt(qb[u], k[u]) * dmat_ref[U[u][1]] for u in idx]
    upd = [_bdot_tn(k[u] * kdec_ref[U[u][1]], v[u]) for u in idx]
    inner = [jnp.dot(scores[u].astype(BF16), v[u], preferred_element_type=F32) for u in idx]
    for u, (n, h) in enumerate(U):
        snew_ref[n, h] = s[u] * sdec_ref[h] + upd[u]
        o_ref[:, n, h * DV:(h + 1) * DV] = _ln_rows(inner[u] + cross[u], LN_EPS)


def _ret_sample(qk3, vb3, s0, pos0, *, nb):
    t, nseq, _ = qk3.shape
    cos2, sin2 = _rope_tables(jnp.arange(t, dtype=F32) + pos0)
    dmat, cross, kdec, sdec = _decay_tables(t)
    row_spec = pl.BlockSpec((t, nb, D), lambda i: (0, i, 0))
    st_spec = pl.BlockSpec((nb, HB, DK, DV), lambda i: (i, 0, 0, 0))
    return pl.pallas_call(
        functools.partial(_ret_sample_kernel, nb=nb),
        grid=(nseq // nb,),
        in_specs=[row_spec, row_spec, _const_spec((t, DK)), _const_spec((t, DK)), st_spec,
                  _const_spec((HB, t, t)), _const_spec((HB, t, 1)), _const_spec((HB, t, 1)),
                  pl.BlockSpec(memory_space=pltpu.SMEM)],
        out_specs=[row_spec, st_spec],
        out_shape=[jax.ShapeDtypeStruct(qk3.shape, F32), jax.ShapeDtypeStruct(s0.shape, F32)],
        compiler_params=pltpu.CompilerParams(dimension_semantics=("arbitrary",),
                                             vmem_limit_bytes=VMEM_LIMIT),
        name="ret_sample",
    )(qk3, vb3, cos2, sin2, s0, dmat, cross, kdec, sdec)


def _unpair_tokens(ya_ref, ya_scr, pair_rows):
    half = ya_ref.shape[0]
    groups = half // pair_rows
    low = lax.broadcasted_iota(jnp.int32, (half, LANES), 1) < KA
    for m in range(D // LANES):
        ha = ya_ref[:, 2 * m, :]
        hb = ya_ref[:, 2 * m + 1, :]
        lo = jnp.where(low, ha, pltpu.roll(hb, KA, axis=1))
        hi = jnp.where(low, pltpu.roll(ha, KA, axis=1), hb)
        ya_scr[:, 0, :, m * LANES:(m + 1) * LANES] = lo.reshape(groups, pair_rows, LANES)
        ya_scr[:, 1, :, m * LANES:(m + 1) * LANES] = hi.reshape(groups, pair_rows, LANES)
    return ya_scr[...].reshape(2 * half, D)


def _post_kernel(ga_ref, ya_ref, gb_ref, ob_ref, x_ref, gate_ref, wout_ref, lng_ref, lnb_ref, y_ref,
                 ya_scr, *, gs, alpha, pair_rows):
    merged = ga_ref[...] * _unpair_tokens(ya_ref, ya_scr, pair_rows) + gb_ref[...] * ob_ref[...]
    sub = jnp.dot(merged.astype(BF16), wout_ref[...], preferred_element_type=F32)
    x = x_ref[...]
    gate = gate_ref[...]
    if gs == 1:
        z = alpha * x + gate * sub
    else:
        tt = x.shape[0]
        z = (alpha * x.reshape(tt // gs, gs, D) + gate[None] * sub.reshape(tt // gs, gs, D)).reshape(tt, D)
    y_ref[...] = _ln_rows(z, LN_EPS) * lng_ref[...] + lnb_ref[...]


def _post(ga, ya, gb, ob, x2, gate, wout, ln_gain, ln_bias, *, nseq_grid, tiles, tt, gs, alpha, pair_rows):
    row_spec = pl.BlockSpec((tt, D), lambda n, j: (n * tiles + j, 0))
    ya_spec = pl.BlockSpec((tt // 2, HA, LANES), lambda n, j: (n * tiles + j, 0, 0))
    scratch = [pltpu.VMEM((tt // 2 // pair_rows, 2, pair_rows, D), F32)]
    return pl.pallas_call(
        functools.partial(_post_kernel, gs=gs, alpha=alpha, pair_rows=pair_rows),
        grid=(nseq_grid, tiles),
        in_specs=[row_spec, ya_spec, row_spec, row_spec, row_spec,
                  pl.BlockSpec((None, gs, D), lambda n, j: (n, 0, 0)),
                  _const_spec((D, D)), _const_spec((1, D)), _const_spec((1, D))],
        out_specs=row_spec,
        out_shape=jax.ShapeDtypeStruct(x2.shape, F32),
        scratch_shapes=scratch,
        compiler_params=pltpu.CompilerParams(dimension_semantics=("arbitrary", "arbitrary"),
                                             vmem_limit_bytes=VMEM_LIMIT),
        name="post",
    )(ga, ya, gb, ob, x2, gate, wout, ln_gain, ln_bias)


def _head_param_lanes(p, nseq):
    return jnp.tile(p.reshape(HA, KA).T, (1, nseq))


def _layer(x2, ada, first_prev, s_wkv, s_ret, pos0, weights, *, nseq, t, seq_major, tt, tc, tp, depth, wkv_seqs=1):
    (wsh, wrest, pre_params, gn_gain, gn_bias, r_k, wout, ln_gain, ln_bias) = weights
    if seq_major:
        nseq_grid, gs, tiles = nseq, 1, t // tt
    else:
        nseq_grid, gs, tiles = 1, nseq, (t * nseq) // tt
    shift_c = ada[:, :D].reshape(nseq_grid, gs, D)
    scale_c = ada[:, D:2 * D].reshape(nseq_grid, gs, D)
    gate_c = ada[:, 2 * D:].reshape(nseq_grid, gs, D)
    cfg = dict(nseq_grid=nseq_grid, tiles=tiles, tt=tt, gs=gs)

    *vecs, gates, new_shift = _pre(
        x2, scale_c, shift_c, first_prev.reshape(nseq_grid, gs, SHIFT_W), wsh, wrest, pre_params,
        pair_out=not seq_major, gate_dtype=BF16 if seq_major else F32, **cfg)

    if seq_major:
        row = lambda p: p.reshape(1, D)
        ya, new_wkv = _wkv_chunked(vecs[0], pre_params[3], pre_params[4], row(gn_gain), row(gn_bias), row(r_k),
                                   nseq=nseq, t=t, ct=tc, nbatch=wkv_seqs)
    else:
        s0 = s_wkv.transpose(3, 2, 0, 1).reshape(KA, KA, nseq * HA)
        ya, s_fin = _wkv(vecs, s0, _head_param_lanes(gn_gain, nseq), _head_param_lanes(gn_bias, nseq),
                         _head_param_lanes(r_k.reshape(-1), nseq), nseq=nseq, t=t, tc=tc)
        pair_rows = (tc // 2) * gs
        new_wkv = s_fin.reshape(KA, KA, nseq, HA).transpose(2, 3, 1, 0)

    alpha = (2.0 * depth) ** 0.25
    if seq_major:
        y, new_ret = _ret_post(gates, ya, x2, gate_c, wout, ln_gain, ln_bias,
                               nseq=nseq, t=t, tt=tp, nbatch=wkv_seqs, alpha=alpha)
    else:
        ga, gb, qk, vb = [gates[:, i * D:(i + 1) * D] for i in range(4)]
        ob3, new_ret = _ret_sample(qk.reshape(t, nseq, D), vb.reshape(t, nseq, D), s_ret, pos0, nb=8)
        y = _post(ga, ya, gb, ob3.reshape(t * nseq, D), x2, gate_c, wout, ln_gain, ln_bias, alpha=alpha,
                  pair_rows=pair_rows, nseq_grid=nseq_grid, tiles=(t * nseq) // tp, tt=tp, gs=gs)
    return y, new_shift.reshape(nseq, SHIFT_W), new_wkv, new_ret


def kernel(x_prompt, x_sample, c_prompt, c_sample, state_shift, state_wkv, state_ret, w_ada, b_ada, w_in,
           mu_shift, w0, w_decay_up, a0, w_icl_up, k_k, k_a, r_k, gn_a_gain, gn_a_bias, w_out, ln_gain, ln_bias):
    depth = w_in.shape[0]
    nb, t_p, _ = x_prompt.shape
    ns, t_s, _ = x_sample.shape
    h_p = x_prompt.reshape(nb * t_p, D)
    h_s = x_sample.transpose(1, 0, 2).reshape(t_s * ns, D)
    outs = [[] for _ in range(6)]
    zero_lora = jnp.zeros((LORA, D), F32)
    for l in range(depth):
        row = lambda p: p[l].reshape(1, -1)
        wsh, wrest = _cast_w_in(w_in, l)
        wd_pad = jnp.concatenate([w_decay_up[l], zero_lora], axis=0).astype(BF16)
        wa_pad = jnp.concatenate([zero_lora, w_icl_up[l]], axis=0).astype(BF16)
        pre_params = (row(mu_shift), row(w0), row(a0), row(k_k), row(k_a), wd_pad, wa_pad)
        weights = (wsh, wrest, pre_params, gn_a_gain[l], gn_a_bias[l], r_k[l], w_out[l].astype(BF16),
                   row(ln_gain), row(ln_bias))
        ada = _ada(jnp.concatenate([c_prompt, c_sample], axis=0), w_ada[l], b_ada[l])
        h_p, s1, s2, s3 = _layer(
            h_p, ada[:nb], jnp.zeros((nb, SHIFT_W), F32), None, None, 0,
            weights, nseq=nb, t=t_p, seq_major=True, depth=depth, **PROMPT_TILES)
        h_s, t1, t2, t3 = _layer(
            h_s, ada[nb:], state_shift[l], state_wkv[l], state_ret[l], PAST_LEN,
            weights, nseq=ns, t=t_s, seq_major=False, tt=ns, tc=t_s, tp=t_s * ns, depth=depth)
        for lst, val in zip(outs, (s1, s2, s3, t1, t2, t3)):
            lst.append(val)
    y_p = h_p.reshape(nb, t_p, D)
    y_s = h_s.reshape(t_s, ns, D).transpose(1, 0, 2)
    return (y_p, y_s) + tuple(jnp.stack(o) for o in outs)
```

```python
import functools
import math

import numpy as np
import jax
import jax.numpy as jnp
from jax import lax
from jax.experimental import pallas as pl
from jax.experimental.pallas import tpu as pltpu

F32 = jnp.float32
BF16 = jnp.bfloat16

D = 1024
HA = 16
KA = 64
LORA = 64
GN_EPS = 64e-5
HB = 4
DK = 128
DV = 256
RET_CHUNK = 128
PAST_LEN = 16384
ROPE_BASE = 10000.0
LN_EPS = 1e-5
SHIFT_W = 3 * D + 2 * LORA
REST_W = 6 * D
LANES = 128
SUBLANES = 8
VMEM_LIMIT = 56 * 1024 * 1024
PROMPT_TILES = dict(tt=256, tc=256, wkv_seqs=2, tp=256)


def _sigmoid(x):
    return 1.0 / (1.0 + jnp.exp(-x))


def _log_decay(w_pre):
    zw = -w_pre
    softplus = jnp.maximum(zw, 0.0) + jnp.log(1.0 + jnp.exp(-jnp.abs(zw)))
    return -jnp.exp(-softplus - 0.5)


def _bdot(a, b):
    return jnp.dot(a.astype(BF16), b.astype(BF16), preferred_element_type=F32)


def _bdot_nt(a, b):
    return lax.dot_general(a.astype(BF16), b.astype(BF16), (((1,), (1,)), ((), ())), preferred_element_type=F32)


def _bdot_tn(a, b):
    return lax.dot_general(a.astype(BF16), b.astype(BF16), (((0,), (0,)), ((), ())), preferred_element_type=F32)


def _split_bf16(x, parts):
    out = []
    for _ in range(parts):
        term = x.astype(BF16)
        out.append(term)
        x = x - term.astype(F32)
    return out


def _interleave(*gens):
    live = list(gens)
    while live:
        for gen in list(live):
            try:
                next(gen)
            except StopIteration:
                live.remove(gen)


def _const_spec(shape):
    nd = len(shape)
    return pl.BlockSpec(shape, lambda *_: (0,) * nd, pipeline_mode=pl.Buffered(1))


def _cast_w_in_kernel(w_ref, sh_ref, rest_ref):
    sh_ref[...] = w_ref[:, :SHIFT_W].astype(BF16)
    rest_ref[...] = w_ref[:, SHIFT_W:].astype(BF16)


def _cast_w_in(w_in, layer):
    rb = 256
    return pl.pallas_call(
        _cast_w_in_kernel,
        grid=(D // rb,),
        in_specs=[pl.BlockSpec((None, rb, SHIFT_W + REST_W), lambda i: (layer, i, 0))],
        out_specs=[pl.BlockSpec((rb, SHIFT_W), lambda i: (i, 0)), pl.BlockSpec((rb, REST_W), lambda i: (i, 0))],
        out_shape=[jax.ShapeDtypeStruct((D, SHIFT_W), BF16), jax.ShapeDtypeStruct((D, REST_W), BF16)],
        compiler_params=pltpu.CompilerParams(dimension_semantics=("arbitrary",)),
        name="cast_w_in",
    )(w_in)


def _ada_kernel(c_ref, w_ref, b_ref, o_ref):
    c = c_ref[...]
    s = c * _sigmoid(c)
    o_ref[...] = jnp.dot(s, w_ref[...], preferred_element_type=F32,
                         precision=lax.Precision.HIGHEST) + b_ref[...]


def _ada(c_all, w_ada, b_ada):
    n = c_all.shape[0]
    return pl.pallas_call(
        _ada_kernel,
        grid=(3,),
        in_specs=[pl.BlockSpec((n, D), lambda j: (0, 0)),
                  pl.BlockSpec((D, D), lambda j: (0, j)),
                  pl.BlockSpec((1, D), lambda j: (0, j))],
        out_specs=pl.BlockSpec((n, D), lambda j: (0, j)),
        out_shape=jax.ShapeDtypeStruct((n, 3 * D), F32),
        compiler_params=pltpu.CompilerParams(dimension_semantics=("arbitrary",)),
        name="ada",
    )(c_all, w_ada, b_ada.reshape(1, 3 * D))


def _modulate(x, scale, shift, gs):
    if gs == 1:
        return x * (1.0 + scale) + shift
    tt = x.shape[0]
    x3 = x.reshape(tt // gs, gs, D)
    return (x3 * (1.0 + scale)[None] + shift[None]).reshape(tt, D)


def _put_head_pairs(out_ref, scr, x, y):
    tt = x.shape[0]
    low = lax.broadcasted_iota(jnp.int32, (tt, LANES), 1) < KA
    for m in range(D // LANES):
        xc = x[:, m * LANES:(m + 1) * LANES]
        yc = y[:, m * LANES:(m + 1) * LANES]
        even = jnp.where(low, xc, pltpu.roll(yc, KA, axis=1))
        odd = jnp.where(low, pltpu.roll(xc, KA, axis=1), yc)
        for r in range(tt // SUBLANES):
            rows = slice(r * SUBLANES, (r + 1) * SUBLANES)
            scr[pl.ds((r * HA + 2 * m) * SUBLANES, SUBLANES), :] = even[rows]
            scr[pl.ds((r * HA + 2 * m + 1) * SUBLANES, SUBLANES), :] = odd[rows]

    def row_block(r, carry):
        for s in range(SUBLANES):
            out_ref[r * SUBLANES + s] = scr[pl.ds(r * (HA * SUBLANES) + s, HA, stride=SUBLANES), :]
        return carry

    lax.fori_loop(0, tt // SUBLANES, row_block, 0)


def _pre_kernel(x_ref, scale_ref, shift_ref, prev_ref, wsh_ref, wrest_ref, mu_ref, w0_ref, a0_ref,
                kk_ref, ka_ref, wd_ref, wa_ref, *rest, gs, tt, pad, pair_out):
    if pair_out:
        rk_out, wk_out, va_out, gate_out, ns_out, p_scr, pair_scr = rest
    else:
        vec_out, gate_out, ns_out, p_scr = rest
        r_out, k_out, v_out, wpre_out, apre_out = [vec_out.at[:, i * D:(i + 1) * D] for i in range(5)]
    ga_out, gb_out, qk_out, vb_out = [gate_out.at[:, i * D:(i + 1) * D] for i in range(4)]
    j = pl.program_id(1)
    nj = pl.num_programs(1)
    lo = pad - gs

    @pl.when(j == 0)
    def _():
        p_scr[lo:pad, :] = prev_ref[...]

    u = _modulate(x_ref[...], scale_ref[...], shift_ref[...], gs).astype(BF16)

    def mixed(c0, c1):
        p_scr[pad:pad + tt, c0:c1] = jnp.dot(u, wsh_ref[:, c0:c1], preferred_element_type=F32)
        p = p_scr[pad:pad + tt, c0:c1]
        prev = p_scr[lo:lo + tt, c0:c1]
        return p + (prev - p) * mu_ref[:, c0:c1]

    def proj(c0, c1):
        return jnp.dot(u, wrest_ref[:, c0:c1], preferred_element_type=F32)

    def branch_a():
        lora_in = mixed(3 * D, SHIFT_W)
        yield
        lora_w = _bdot(jnp.tanh(lora_in), wd_ref[...])
        lora_a = _bdot(lora_in, wa_ref[...])
        w_pre = w0_ref[...] + lora_w
        a_pre = a0_ref[...] + lora_a
        yield
        k = mixed(D, 2 * D)
        yield
        if pair_out:
            a = _sigmoid(a_pre)
            _put_head_pairs(rk_out, pair_scr.at[0], mixed(0, D), k * (1.0 + (a - 1.0) * ka_ref[...]))
            yield
            _put_head_pairs(wk_out, pair_scr.at[1], jnp.exp(_log_decay(w_pre)), k * kk_ref[...])
            _put_head_pairs(va_out, pair_scr.at[2], mixed(2 * D, 3 * D), a)
        else:
            wpre_out[...] = w_pre
            apre_out[...] = a_pre
            k_out[...] = k
            r_out[...] = mixed(0, D)
            yield
            v_out[...] = mixed(2 * D, 3 * D)
        yield

    def gates():
        z_a = proj(0, D)
        yield
        g_a = proj(4 * D, 5 * D)
        ga_out[...] = (_sigmoid(g_a) * (z_a * _sigmoid(z_a))).astype(ga_out.dtype)
        yield
        z_b = proj(3 * D, 4 * D)
        yield
        g_b = proj(5 * D, 6 * D)
        gb_out[...] = (_sigmoid(g_b) * (z_b * _sigmoid(z_b))).astype(gb_out.dtype)
        yield
        qk_out[...] = proj(D, 2 * D).astype(qk_out.dtype)
        yield
        vb_out[...] = proj(2 * D, 3 * D).astype(vb_out.dtype)
        yield

    _interleave(branch_a(), gates())

    @pl.when(j == nj - 1)
    def _():
        ns_out[...] = p_scr[pad + tt - gs:pad + tt, :]

    p_scr[lo:pad, :] = p_scr[pad + tt - gs:pad + tt, :]


def _pre(x2, scale, shift, first_prev, wsh, wrest, params, *, nseq_grid, tiles, tt, gs, pair_out, gate_dtype):
    rows = x2.shape[0]
    pad = max(SUBLANES, gs)
    mu, w0, a0, k_k, k_a, wd_pad, wa_pad = params
    row_spec = pl.BlockSpec((tt, D), lambda n, j: (n * tiles + j, 0))
    seq_spec = pl.BlockSpec((None, gs, D), lambda n, j: (n, 0, 0))
    scratch = [pltpu.VMEM((pad + tt, SHIFT_W), F32)]
    if pair_out:
        vec_specs = [pl.BlockSpec((tt, HA, LANES), lambda n, j: (n * tiles + j, 0, 0))] * 3
        vec_sds = [jax.ShapeDtypeStruct((rows, HA, LANES), F32)] * 3
        scratch.append(pltpu.VMEM((3, tt * HA, LANES), F32))
    else:
        vec_specs = [pl.BlockSpec((tt, 5 * D), lambda n, j: (n * tiles + j, 0))]
        vec_sds = [jax.ShapeDtypeStruct((rows, 5 * D), F32)]
    gate_spec = pl.BlockSpec((tt, 4 * D), lambda n, j: (n * tiles + j, 0))
    gate_sds = jax.ShapeDtypeStruct((rows, 4 * D), gate_dtype)
    kern = functools.partial(_pre_kernel, gs=gs, tt=tt, pad=pad, pair_out=pair_out)
    return pl.pallas_call(
        kern,
        grid=(nseq_grid, tiles),
        in_specs=[row_spec, seq_spec, seq_spec,
                  pl.BlockSpec((None, gs, SHIFT_W), lambda n, j: (n, 0, 0)),
                  _const_spec((D, SHIFT_W)), _const_spec((D, REST_W)), _const_spec((1, SHIFT_W)),
                  _const_spec((1, D)), _const_spec((1, D)), _const_spec((1, D)), _const_spec((1, D)),
                  _const_spec((LANES, D)), _const_spec((LANES, D))],
        out_specs=vec_specs + [gate_spec, pl.BlockSpec((None, gs, SHIFT_W), lambda n, j: (n, 0, 0))],
        out_shape=vec_sds + [gate_sds, jax.ShapeDtypeStruct((nseq_grid, gs, SHIFT_W), F32)],
        scratch_shapes=scratch,
        compiler_params=pltpu.CompilerParams(dimension_semantics=("arbitrary", "arbitrary"),
                                             vmem_limit_bytes=VMEM_LIMIT),
        name="pre",
    )(x2, scale, shift, first_prev, wsh, wrest, mu, w0, a0, k_k, k_a, wd_pad, wa_pad)


V_KKN, V_B, V_W, V_K, V_R, V_V = range(6)


def _wkv_kernel(rk_ref, wk_ref, va_ref, s0_ref, gain_ref, bias_ref, rkp_ref,
                o_ref, sfin_ref, s_scr, vec_scr, row_scr, y_scr, cur_scr, *, tc):
    j = pl.program_id(1)

    @pl.when(j == 0)
    def _():
        s_scr[...] = s0_ref[...]

    def load_pair(ref, t):
        xt = ref[t].reshape(LANES, LANES).T
        return xt[:KA], xt[KA:]

    def prep(t, carry):
        dec, kkraw = load_pair(wk_ref, t)
        ss = jnp.sum(kkraw * kkraw, axis=0, keepdims=True)
        kkn = kkraw * (1.0 / jnp.maximum(jnp.sqrt(ss), 1e-12))
        v, a = load_pair(va_ref, t)
        r, kmod = load_pair(rk_ref, t)
        vec_scr[t, V_KKN] = kkn
        vec_scr[t, V_B] = kkn * a
        vec_scr[t, V_W] = dec
        vec_scr[t, V_K] = kmod
        vec_scr[t, V_R] = r
        vec_scr[t, V_V] = v
        rk = jnp.sum(r * kmod * rkp_ref[...], axis=0, keepdims=True)
        row_scr[t] = jnp.broadcast_to(rk, (SUBLANES, LANES))
        return carry

    def state_pass(t, carry):
        cur_scr[...] = vec_scr[t]
        skk = jnp.zeros((KA, LANES), F32)
        for k in range(KA):
            skk = skk + s_scr[k] * cur_scr[V_KKN, k:k + 1, :]
        vv = cur_scr[V_V]
        y = jnp.zeros((KA, LANES), F32)
        for k in range(KA):
            s_new = (s_scr[k] * cur_scr[V_W, k:k + 1, :] - skk * cur_scr[V_B, k:k + 1, :]
                     + vv * cur_scr[V_K, k:k + 1, :])
            s_scr[k] = s_new
            y = y + s_new * cur_scr[V_R, k:k + 1, :]
        y_scr[t] = y
        return carry

    def out_token(t):
        y = y_scr[t]
        m = jnp.mean(y, axis=0, keepdims=True)
        yc = y - m
        var = jnp.mean(yc * yc, axis=0, keepdims=True)
        yn = yc * lax.rsqrt(var + GN_EPS) * gain_ref[...] + bias_ref[...]
        return yn + row_scr[t, 0:1, :] * vec_scr[t, V_V]

    def finish(i, carry):
        o2 = jnp.concatenate([out_token(i), out_token(i + tc // 2)], axis=0)
        o_ref[i] = o2.T.reshape(LANES // HA, HA, LANES)
        return carry

    lax.fori_loop(0, tc, prep, 0, unroll=2)
    lax.fori_loop(0, tc, state_pass, 0)
    lax.fori_loop(0, tc // 2, finish, 0, unroll=2)

    @pl.when(j == pl.num_programs(1) - 1)
    def _():
        sfin_ref[...] = s_scr[...]


def _wkv(pairs, s0, gain_l, bias_l, rk_l, *, nseq, t, tc):
    nsl = LANES // HA
    groups = nseq // nsl
    shape4, oshape4 = (t, nseq, HA, LANES), (t // 2, nseq, HA, LANES)
    vec_spec = pl.BlockSpec((tc, nsl, HA, LANES), lambda g, j: (j, g, 0, 0))
    out_spec = pl.BlockSpec((tc // 2, nsl, HA, LANES), lambda g, j: (j, g, 0, 0))
    st_spec = pl.BlockSpec((KA, KA, LANES), lambda g, j: (0, 0, g))
    par_spec = pl.BlockSpec((KA, LANES), lambda g, j: (0, g))
    o4, s_fin = pl.pallas_call(
        functools.partial(_wkv_kernel, tc=tc),
        grid=(groups, t // tc),
        in_specs=[vec_spec] * 3 + [st_spec, par_spec, par_spec, par_spec],
        out_specs=[out_spec, st_spec],
        out_shape=[jax.ShapeDtypeStruct(oshape4, F32),
                   jax.ShapeDtypeStruct((KA, KA, nseq * HA), F32)],
        scratch_shapes=[pltpu.VMEM((KA, KA, LANES), F32), pltpu.VMEM((tc, 6, KA, LANES), F32),
                        pltpu.VMEM((tc, SUBLANES, LANES), F32), pltpu.VMEM((tc, KA, LANES), F32),
                        pltpu.VMEM((6, KA, LANES), F32)],
        compiler_params=pltpu.CompilerParams(dimension_semantics=("arbitrary", "arbitrary"),
                                             vmem_limit_bytes=VMEM_LIMIT),
        name="wkv",
    )(*[z.reshape(shape4) for z in pairs], s0, gain_l, bias_l, rk_l)
    return o4.reshape(nseq * t // 2, HA, LANES), s_fin


CH = 64
NPAIR = D // LANES


def _wkv_chunk_kernel(r_ref, k_ref, v_ref, wpre_ref, apre_ref, kk_ref, ka_ref, gain_ref, bias_ref, rk_ref,
                      tri_ref, ones_ref, o_ref, tfin_ref, t_scr, *, ct):
    j = pl.program_id(1)

    @pl.when(j == 0)
    def _():
        t_scr[...] = jnp.zeros_like(t_scr)

    lane = lax.broadcasted_iota(jnp.int32, (CH, LANES), 1)
    row = lax.broadcasted_iota(jnp.int32, (CH, LANES), 0)
    low = lane < KA
    col = jnp.bitwise_and(lane, KA - 1)
    strict = row > col
    incl = row >= col
    eye = (row == col).astype(F32)
    low2 = lax.broadcasted_iota(jnp.int32, (LANES, LANES), 1) < KA
    diag_blocks = (lax.broadcasted_iota(jnp.int32, (LANES, LANES), 0) < KA) == low2
    tri = tri_ref[...]
    ones_bd = ones_ref[...]

    def bd(z):
        return jnp.concatenate([jnp.where(low, z, 0.0), jnp.where(low, 0.0, z)], axis=0)

    def abd(z):
        return jnp.concatenate([jnp.where(low, 0.0, z), jnp.where(low, z, 0.0)], axis=0)

    def seg_sum_pairs(xs):
        tot = _bdot(jnp.concatenate(xs, axis=0), ones_bd)
        return [tot[i * CH:(i + 1) * CH] for i in range(len(xs))]

    nbatch = r_ref.shape[0]
    B = range(nbatch)
    P = range(nbatch * NPAIR)
    seq = [i // NPAIR for i in P]
    pair = [i % NPAIR for i in P]
    ln = [slice(pair[i] * LANES, (pair[i] + 1) * LANES) for i in P]

    def chunk_prep(c0, ctx):
        rows = slice(c0, c0 + CH)
        lw_all = [_log_decay(wpre_ref[b, rows, :]) for b in B]
        cs_all = [sum(jnp.dot(tri, term, preferred_element_type=F32) for term in _split_bf16(lw_all[b], 2))
                  for b in B]
        yield
        lw = [lw_all[seq[m]][:, ln[m]] for m in P]
        cs = [cs_all[seq[m]][:, ln[m]] for m in P]
        k_in = [k_ref[seq[m], rows, ln[m]] for m in P]
        kkraw = [k_in[m] * kk_ref[:, ln[m]] for m in P]
        ss = seg_sum_pairs([kkraw[m] * kkraw[m] for m in P])
        yield
        a = [_sigmoid(apre_ref[seq[m], rows, ln[m]]) for m in P]
        kkn = [kkraw[m] / jnp.maximum(jnp.sqrt(ss[m]), 1e-12) for m in P]
        b = [kkn[m] * a[m] for m in P]
        r = [r_ref[seq[m], rows, ln[m]] for m in P]
        k = [k_in[m] * (1.0 + (a[m] - 1.0) * ka_ref[:, ln[m]]) for m in P]
        v = [v_ref[seq[m], rows, ln[m]] for m in P]
        cs_last = [cs[m][CH - 1:CH, :] for m in P]
        e_neg = [jnp.exp(-cs[m]) for m in P]
        a_t = [-kkn[m] * jnp.exp(cs[m] - lw[m]) for m in P]
        r_t = [r[m] * jnp.exp(cs[m]) for m in P]
        b_t = [b[m] * e_neg[m] for m in P]
        k_t = [k[m] * e_neg[m] for m in P]
        lhs = [jnp.concatenate([a_t[m], r_t[m]], axis=0) for m in P]
        out0 = [_bdot_nt(jnp.where(low2, lhs[m], 0.0), jnp.concatenate([b_t[m], k_t[m]], axis=0)) for m in P]
        yield
        out1 = [_bdot_nt(jnp.where(low2, 0.0, lhs[m]), jnp.concatenate([k_t[m], b_t[m]], axis=0)) for m in P]
        yield
        l_cat = [jnp.where(strict, jnp.where(low, out0[m][:CH], out1[m][:CH]), 0.0) for m in P]
        ak_cat = [jnp.where(strict, jnp.where(low, out1[m][:CH], out0[m][:CH]), 0.0) for m in P]
        rb_cat = [jnp.where(incl, jnp.where(low, out0[m][CH:], out1[m][CH:]), 0.0) for m in P]
        rk_cat = [jnp.where(incl, jnp.where(low, out1[m][CH:], out0[m][CH:]), 0.0) for m in P]
        rkv = seg_sum_pairs([r[m] * k[m] * rk_ref[:, ln[m]] for m in P])
        e_end = [jnp.exp(cs_last[m] - cs[m]) for m in P]
        ctx.update(
            rows=rows, l_cat=l_cat, v=v, rkv=rkv,
            g_lhs=[jnp.concatenate([a_t[m], ak_cat[m]], axis=1) for m in P],
            y_lhs=[jnp.concatenate([r_t[m], rb_cat[m], rk_cat[m]], axis=1) for m in P],
            upd_lhs=[jnp.concatenate([b[m] * e_end[m], k[m] * e_end[m]], axis=0) for m in P],
            p_end=[jnp.exp(cs_last[m]) for m in P])
        yield

    def chunk_inverse(ctx):
        l_cat = ctx["l_cat"]
        x = [eye + l_cat[m] for m in P]
        lp = [_bdot(l_cat[m], bd(l_cat[m])) for m in P]
        yield
        for _ in range(4):
            both = [_bdot(jnp.concatenate([lp[m], x[m]], axis=0), bd(lp[m])) for m in P]
            yield
            x = [x[m] + both[m][CH:] for m in P]
            lp = [both[m][:CH] for m in P]
        ctx["x"] = [x[m] + _bdot(x[m], bd(lp[m])) for m in P]
        yield

    def chunk_state(ctx):
        x, v = ctx["x"], ctx["v"]
        t0 = [t_scr[seq[m], pair[m]] for m in P]
        v_abd = [abd(v[m]) for m in P]
        g = [_bdot(ctx["g_lhs"][m], jnp.concatenate([t0[m], v_abd[m]], axis=0)) for m in P]
        yield
        u = [_bdot(x[m], bd(g[m])) for m in P]
        yield
        upd = [_bdot_tn(ctx["upd_lhs"][m], jnp.concatenate([u[m], v[m]], axis=0)) for m in P]
        yield
        for m in P:
            p_col = jnp.broadcast_to(ctx["p_end"][m], (LANES, LANES)).T
            t_scr[seq[m], pair[m]] = t0[m] * p_col + jnp.where(diag_blocks, upd[m], 0.0)
        y = [_bdot(ctx["y_lhs"][m], jnp.concatenate([t0[m], bd(u[m]), v_abd[m]], axis=0)) for m in P]
        yield
        mean = seg_sum_pairs(y)
        yield
        yc = [y[m] - mean[m] * (1.0 / KA) for m in P]
        var = seg_sum_pairs([yc[m] * yc[m] for m in P])
        yield
        for m in P:
            yn = yc[m] * lax.rsqrt(var[m] * (1.0 / KA) + GN_EPS) * gain_ref[:, ln[m]] + bias_ref[:, ln[m]]
            o_ref[seq[m], ctx["rows"], ln[m]] = yn + ctx["rkv"][m] * v[m]
        yield

    nchunk = ct // CH
    ctxs = [dict() for _ in range(nchunk)]
    for c in range(-2, nchunk):
        stages = []
        if c >= 0:
            stages.append(chunk_state(ctxs[c]))
        if 0 <= c + 1 < nchunk:
            stages.append(chunk_inverse(ctxs[c + 1]))
        if c + 2 < nchunk:
            stages.append(chunk_prep((c + 2) * CH, ctxs[c + 2]))
        _interleave(*stages)

    @pl.when(j == pl.num_programs(1) - 1)
    def _():
        tfin_ref[...] = t_scr[...]


def _wkv_chunked(vec, k_k, k_a, gain, bias, rk, *, nseq, t, ct, nbatch):
    row_spec = pl.BlockSpec((nbatch, ct, D), lambda n, j: (n, j, 0))
    tri = jnp.asarray(np.tril(np.ones((CH, CH), np.float32)), BF16)
    head_of_lane = np.arange(LANES) // KA
    ones_bd = jnp.asarray((head_of_lane[:, None] == head_of_lane[None, :]).astype(np.float32), BF16)
    ya, t_fin = pl.pallas_call(
        functools.partial(_wkv_chunk_kernel, ct=ct),
        grid=(nseq // nbatch, t // ct),
        in_specs=[pl.BlockSpec((nbatch, ct, D), lambda n, j, i=i: (n, j, i)) for i in range(5)]
        + [_const_spec((1, D))] * 5 + [_const_spec((CH, CH)), _const_spec((LANES, LANES))],
        out_specs=[row_spec, pl.BlockSpec((nbatch, NPAIR, LANES, LANES), lambda n, j: (n, 0, 0, 0))],
        out_shape=[jax.ShapeDtypeStruct((nseq, t, D), F32),
                   jax.ShapeDtypeStruct((nseq, NPAIR, LANES, LANES), F32)],
        scratch_shapes=[pltpu.VMEM((nbatch, NPAIR, LANES, LANES), F32)],
        compiler_params=pltpu.CompilerParams(dimension_semantics=("arbitrary", "arbitrary"),
                                             vmem_limit_bytes=VMEM_LIMIT),
        name="wkv_chunk",
    )(*[vec.reshape(nseq, t, 5 * D)] * 5, k_k, k_a, gain, bias, rk, tri, ones_bd)
    t6 = t_fin.reshape(nseq, NPAIR, 2, KA, 2, KA)
    diag = jnp.stack([t6[:, :, 0, :, 0, :], t6[:, :, 1, :, 1, :]], axis=2)
    return ya.reshape(nseq * t, D), diag.reshape(nseq, HA, KA, KA).transpose(0, 1, 3, 2)


def _log_gamma():
    return [math.log1p(-2.0 ** (-5.0 - h)) for h in range(HB)]


def _rope_tables(pos):
    half = DK // 2
    theta = 1.0 / (ROPE_BASE ** jnp.linspace(0.0, 1.0, half, dtype=F32))
    ang = pos[:, None] * theta[None, :]
    cos, sin = jnp.cos(ang), jnp.sin(ang)
    return jnp.concatenate([cos, cos], axis=1), jnp.concatenate([-sin, sin], axis=1)


def _decay_tables(chunk):
    lg = jnp.asarray(_log_gamma(), F32)
    idx = jnp.arange(chunk, dtype=F32)
    diff = idx[:, None] - idx[None, :]
    dmat = jnp.exp(jnp.maximum(diff, 0.0)[None] * lg[:, None, None]) * (diff >= 0)[None]
    cross = jnp.exp((idx[None, :] + 1.0) * lg[:, None])[:, :, None]
    kdec = jnp.exp((chunk - 1.0 - idx)[None, :] * lg[:, None])[:, :, None]
    sdec = jnp.exp(chunk * lg)
    return dmat.astype(F32), cross, kdec, sdec


def _rotary(x, cos2, sin2):
    return x * cos2 + pltpu.roll(x, DK // 2, axis=1) * sin2


def _ln_rows(y, eps):
    m = jnp.mean(y, axis=-1, keepdims=True)
    yc = y - m
    var = jnp.mean(yc * yc, axis=-1, keepdims=True)
    return yc * lax.rsqrt(var + eps)


def _ret_post_kernel(qk_ref, vb_ref, ga_ref, ya_ref, gb_ref, x_ref, gate_ref, cos_ref, sin_ref,
                     dmat_ref, cross_ref, kdec_ref, sdec_ref, wout_ref, lng_ref, lnb_ref,
                     y_ref, sfin_ref, s_scr, *, tt, alpha):
    j = pl.program_id(1)

    @pl.when(j == 0)
    def _():
        s_scr[...] = jnp.zeros_like(s_scr)

    nb = x_ref.shape[0]
    B = range(nb)
    U = [(b, h) for b in B for h in range(HB)]
    nchunk = tt // RET_CHUNK

    ctx = [dict() for _ in range(nchunk)]

    def rotate(c):
        rows = slice(c * RET_CHUNK, (c + 1) * RET_CHUNK)
        cos2, sin2 = cos_ref[rows, :], sin_ref[rows, :]
        ctx[c]["qb"] = [_rotary(qk_ref[b, rows, h * DK:(h + 1) * DK].astype(F32), cos2, sin2).astype(BF16)
                        for b, h in U]
        yield
        k = [_rotary(qk_ref[b, rows, (HB + h) * DK:(HB + h + 1) * DK].astype(F32), cos2, sin2) for b, h in U]
        ctx[c]["kb"] = [k[u].astype(BF16) for u in range(len(U))]
        yield
        ctx[c]["kd"] = [(k[u] * kdec_ref[h]).astype(BF16) for u, (b, h) in enumerate(U)]
        yield

    def retention(c):
        rows = slice(c * RET_CHUNK, (c + 1) * RET_CHUNK)
        qb, kb, kd = ctx[c]["qb"], ctx[c]["kb"], ctx[c]["kd"]
        v = [vb_ref[b, rows, h * DV:(h + 1) * DV] for b, h in U]
        dn = (((1,), (1,)), ((), ()))
        scores = [lax.dot_general(qb[u], kb[u], dn, preferred_element_type=F32) * dmat_ref[h]
                  for u, (b, h) in enumerate(U)]
        yield
        s = [s_scr[b, h] for b, h in U]
        cross = [jnp.dot(qb[u], s[u].astype(BF16), preferred_element_type=F32) * cross_ref[h]
                 for u, (b, h) in enumerate(U)]
        yield
        upd = [lax.dot_general(kd[u], v[u], (((0,), (0,)), ((), ())), preferred_element_type=F32)
               for u in range(len(U))]
        yield
        inner = [jnp.dot(scores[u].astype(BF16), v[u], preferred_element_type=F32) for u in range(len(U))]
        for u, (b, h) in enumerate(U):
            s_scr[b, h] = s[u] * sdec_ref[h] + upd[u]
        ctx[c]["yb"] = [inner[u] + cross[u] for u in range(len(U))]
        yield

    def project(c):
        rows = slice(c * RET_CHUNK, (c + 1) * RET_CHUNK)
        ob = [_ln_rows(yb, LN_EPS) for yb in ctx[c]["yb"]]
        yield
        lhs = jnp.concatenate(
            [(ga_ref[b, rows, :].astype(F32) * ya_ref[b, rows, :]
              + gb_ref[b, rows, :].astype(F32) * jnp.concatenate(ob[b * HB:(b + 1) * HB], axis=1)).astype(BF16)
             for b in B], axis=0)
        cols = []
        for n0 in range(0, D, 2 * LANES):
            cols.append(jnp.dot(lhs, wout_ref[:, n0:n0 + 2 * LANES], preferred_element_type=F32))
            yield
        sub = jnp.concatenate(cols, axis=1)
        for b in B:
            z = alpha * x_ref[b, rows, :] + gate_ref[b] * sub[b * RET_CHUNK:(b + 1) * RET_CHUNK]
            y_ref[b, rows, :] = _ln_rows(z, LN_EPS) * lng_ref[...] + lnb_ref[...]
        yield

    for c in range(-2, nchunk):
        _interleave(*([project(c)] if c >= 0 else []),
                    *([retention(c + 1)] if 0 <= c + 1 < nchunk else []),
                    *([rotate(c + 2)] if c + 2 < nchunk else []))

    @pl.when(j == pl.num_programs(1) - 1)
    def _():
        sfin_ref[...] = s_scr[...]


def _ret_post(gates, ya, x2, gate, wout, ln_gain, ln_bias, *, nseq, t, tt, nbatch, alpha):
    cos2, sin2 = _rope_tables(jnp.arange(t, dtype=F32) + 0)
    dmat, cross, kdec, sdec = _decay_tables(RET_CHUNK)
    key_scale = DK ** -0.5
    row_spec = pl.BlockSpec((nbatch, tt, D), lambda n, j: (n, j, 0))
    tab_spec = pl.BlockSpec((tt, DK), lambda n, j: (j, 0))
    rows3 = lambda z: z.reshape(nseq, t, D)
    col_spec = lambda i: pl.BlockSpec((nbatch, tt, D), lambda n, j: (n, j, i))
    gates3 = gates.reshape(nseq, t, 4 * D)
    y, s_fin = pl.pallas_call(
        functools.partial(_ret_post_kernel, tt=tt, alpha=alpha),
        grid=(nseq // nbatch, t // tt),
        in_specs=[col_spec(2), col_spec(3), col_spec(0), row_spec, col_spec(1), row_spec,
                  pl.BlockSpec((nbatch, 1, D), lambda n, j: (n, 0, 0)), tab_spec, tab_spec,
                                   _const_spec((HB, RET_CHUNK, RET_CHUNK)), _const_spec((HB, RET_CHUNK, 1)),
                                   _const_spec((HB, RET_CHUNK, 1)), pl.BlockSpec(memory_space=pltpu.SMEM),
                                   _const_spec((D, D)), _const_spec((1, D)), _const_spec((1, D))],
        out_specs=[row_spec, pl.BlockSpec((nbatch, HB, DK, DV), lambda n, j: (n, 0, 0, 0))],
        out_shape=[jax.ShapeDtypeStruct((nseq, t, D), F32),
                   jax.ShapeDtypeStruct((nseq, HB, DK, DV), F32)],
        scratch_shapes=[pltpu.VMEM((nbatch, HB, DK, DV), F32)],
        compiler_params=pltpu.CompilerParams(dimension_semantics=("arbitrary", "arbitrary"),
                                             vmem_limit_bytes=VMEM_LIMIT),
        name="ret_post",
    )(gates3, gates3, gates3, rows3(ya), gates3, rows3(x2), gate, cos2, sin2,
      dmat * key_scale, cross, kdec * key_scale, sdec, wout, ln_gain, ln_bias)
    return y.reshape(nseq * t, D), s_fin


def _ret_sample_kernel(qk_ref, vb_ref, cos_ref, sin_ref, s_ref, dmat_ref, cross_ref, kdec_ref, sdec_ref,
                       o_ref, snew_ref, *, nb):
    cos2, sin2 = cos_ref[...], sin_ref[...]
    U = [(n, h) for n in range(nb) for h in range(HB)]
    idx = range(len(U))
    qb = [_rotary(qk_ref[:, n, h * DK:(h + 1) * DK], cos2, sin2).astype(BF16) for n, h in U]
    k = [_rotary(qk_ref[:, n, (HB + h) * DK:(HB + h + 1) * DK], cos2, sin2) * (DK ** -0.5) for n, h in U]
    v = [vb_ref[:, n, h * DV:(h + 1) * DV].astype(BF16) for n, h in U]
    s = [s_ref[n, h] for n, h in U]
    cross = [jnp.dot(qb[u], s[u].astype(BF16), preferred_element_type=F32) * cross_ref[U[u][1]] for u in idx]
    scores = [_bdot_nt(qb[u], k[u]) * dmat_ref[U[u][1]] for u in idx]
    upd = [_bdot_tn(k[u] * kdec_ref[U[u][1]], v[u]) for u in idx]
    inner = [jnp.dot(scores[u].astype(BF16), v[u], preferred_element_type=F32) for u in idx]
    for u, (n, h) in enumerate(U):
        snew_ref[n, h] = s[u] * sdec_ref[h] + upd[u]
        o_ref[:, n, h * DV:(h + 1) * DV] = _ln_rows(inner[u] + cross[u], LN_EPS)


def _ret_sample(gates3, s0, pos0, *, nb):
    t, nseq, _ = gates3.shape
    cos2, sin2 = _rope_tables(jnp.arange(t, dtype=F32) + pos0)
    dmat, cross, kdec, sdec = _decay_tables(t)
    row_spec = pl.BlockSpec((t, nb, D), lambda i: (0, i, 0))
    qk_spec = pl.BlockSpec((t, nb, D), lambda i: (0, i, 2))
    vb_spec = pl.BlockSpec((t, nb, D), lambda i: (0, i, 3))
    st_spec = pl.BlockSpec((nb, HB, DK, DV), lambda i: (i, 0, 0, 0))
    return pl.pallas_call(
        functools.partial(_ret_sample_kernel, nb=nb),
        grid=(nseq // nb,),
        in_specs=[qk_spec, vb_spec, _const_spec((t, DK)), _const_spec((t, DK)), st_spec,
                  _const_spec((HB, t, t)), _const_spec((HB, t, 1)), _const_spec((HB, t, 1)),
                  pl.BlockSpec(memory_space=pltpu.SMEM)],
        out_specs=[row_spec, st_spec],
        out_shape=[jax.ShapeDtypeStruct((t, nseq, D), F32), jax.ShapeDtypeStruct(s0.shape, F32)],
        compiler_params=pltpu.CompilerParams(dimension_semantics=("arbitrary",),
                                             vmem_limit_bytes=VMEM_LIMIT),
        name="ret_sample",
    )(gates3, gates3, cos2, sin2, s0, dmat, cross, kdec, sdec)


def _unpair_tokens(ya_ref, ya_scr, pair_rows):
    half = ya_ref.shape[0]
    groups = half // pair_rows
    low = lax.broadcasted_iota(jnp.int32, (half, LANES), 1) < KA
    for m in range(D // LANES):
        ha = ya_ref[:, 2 * m, :]
        hb = ya_ref[:, 2 * m + 1, :]
        lo = jnp.where(low, ha, pltpu.roll(hb, KA, axis=1))
        hi = jnp.where(low, pltpu.roll(ha, KA, axis=1), hb)
        ya_scr[:, 0, :, m * LANES:(m + 1) * LANES] = lo.reshape(groups, pair_rows, LANES)
        ya_scr[:, 1, :, m * LANES:(m + 1) * LANES] = hi.reshape(groups, pair_rows, LANES)
    return ya_scr[...].reshape(2 * half, D)


def _post_kernel(ga_ref, ya_ref, gb_ref, ob_ref, x_ref, gate_ref, wout_ref, lng_ref, lnb_ref, y_ref,
                 ya_scr, *, gs, alpha, pair_rows):
    merged = ga_ref[...] * _unpair_tokens(ya_ref, ya_scr, pair_rows) + gb_ref[...] * ob_ref[...]
    sub = jnp.dot(merged.astype(BF16), wout_ref[...], preferred_element_type=F32)
    x = x_ref[...]
    gate = gate_ref[...]
    if gs == 1:
        z = alpha * x + gate * sub
    else:
        tt = x.shape[0]
        z = (alpha * x.reshape(tt // gs, gs, D) + gate[None] * sub.reshape(tt // gs, gs, D)).reshape(tt, D)
    y_ref[...] = _ln_rows(z, LN_EPS) * lng_ref[...] + lnb_ref[...]


def _post(gates, ya, ob, x2, gate, wout, ln_gain, ln_bias, *, nseq_grid, tiles, tt, gs, alpha, pair_rows):
    row_spec = pl.BlockSpec((tt, D), lambda n, j: (n * tiles + j, 0))
    col_spec = lambda i: pl.BlockSpec((tt, D), lambda n, j: (n * tiles + j, i))
    ya_spec = pl.BlockSpec((tt // 2, HA, LANES), lambda n, j: (n * tiles + j, 0, 0))
    scratch = [pltpu.VMEM((tt // 2 // pair_rows, 2, pair_rows, D), F32)]
    return pl.pallas_call(
        functools.partial(_post_kernel, gs=gs, alpha=alpha, pair_rows=pair_rows),
        grid=(nseq_grid, tiles),
        in_specs=[col_spec(0), ya_spec, col_spec(1), row_spec, row_spec,
                  pl.BlockSpec((None, gs, D), lambda n, j: (n, 0, 0)),
                  _const_spec((D, D)), _const_spec((1, D)), _const_spec((1, D))],
        out_specs=row_spec,
        out_shape=jax.ShapeDtypeStruct(x2.shape, F32),
        scratch_shapes=scratch,
        compiler_params=pltpu.CompilerParams(dimension_semantics=("arbitrary", "arbitrary"),
                                             vmem_limit_bytes=VMEM_LIMIT),
        name="post",
    )(gates, ya, gates, ob, x2, gate, wout, ln_gain, ln_bias)


def _head_param_lanes(p, nseq):
    return jnp.tile(p.reshape(HA, KA).T, (1, nseq))


def _layer(x2, ada, first_prev, s_wkv, s_ret, pos0, weights, *, nseq, t, seq_major, tt, tc, tp, depth, wkv_seqs=1):
    (wsh, wrest, pre_params, gn_gain, gn_bias, r_k, wout, ln_gain, ln_bias) = weights
    if seq_major:
        nseq_grid, gs, tiles = nseq, 1, t // tt
    else:
        nseq_grid, gs, tiles = 1, nseq, (t * nseq) // tt
    shift_c = ada[:, :D].reshape(nseq_grid, gs, D)
    scale_c = ada[:, D:2 * D].reshape(nseq_grid, gs, D)
    gate_c = ada[:, 2 * D:].reshape(nseq_grid, gs, D)
    cfg = dict(nseq_grid=nseq_grid, tiles=tiles, tt=tt, gs=gs)

    *vecs, gates, new_shift = _pre(
        x2, scale_c, shift_c, first_prev.reshape(nseq_grid, gs, SHIFT_W), wsh, wrest, pre_params,
        pair_out=not seq_major, gate_dtype=BF16 if seq_major else F32, **cfg)

    if seq_major:
        row = lambda p: p.reshape(1, D)
        ya, new_wkv = _wkv_chunked(vecs[0], pre_params[3], pre_params[4], row(gn_gain), row(gn_bias), row(r_k),
                                   nseq=nseq, t=t, ct=tc, nbatch=wkv_seqs)
    else:
        s0 = s_wkv.transpose(3, 2, 0, 1).reshape(KA, KA, nseq * HA)
        ya, s_fin = _wkv(vecs, s0, _head_param_lanes(gn_gain, nseq), _head_param_lanes(gn_bias, nseq),
                         _head_param_lanes(r_k.reshape(-1), nseq), nseq=nseq, t=t, tc=tc)
        pair_rows = (tc // 2) * gs
        new_wkv = s_fin.reshape(KA, KA, nseq, HA).transpose(2, 3, 1, 0)

    alpha = (2.0 * depth) ** 0.25
    if seq_major:
        y, new_ret = _ret_post(gates, ya, x2, gate_c, wout, ln_gain, ln_bias,
                               nseq=nseq, t=t, tt=tp, nbatch=wkv_seqs, alpha=alpha)
    else:
        ob3, new_ret = _ret_sample(gates.reshape(t, nseq, 4 * D), s_ret, pos0, nb=8)
        y = _post(gates, ya, ob3.reshape(t * nseq, D), x2, gate_c, wout, ln_gain, ln_bias, alpha=alpha,
                  pair_rows=pair_rows, nseq_grid=nseq_grid, tiles=(t * nseq) // tp, tt=tp, gs=gs)
    return y, new_shift.reshape(nseq, SHIFT_W), new_wkv, new_ret


def kernel(x_prompt, x_sample, c_prompt, c_sample, state_shift, state_wkv, state_ret, w_ada, b_ada, w_in,
           mu_shift, w0, w_decay_up, a0, w_icl_up, k_k, k_a, r_k, gn_a_gain, gn_a_bias, w_out, ln_gain, ln_bias):
    depth = w_in.shape[0]
    nb, t_p, _ = x_prompt.shape
    ns, t_s, _ = x_sample.shape
    h_p = x_prompt.reshape(nb * t_p, D)
    h_s = x_sample.transpose(1, 0, 2).reshape(t_s * ns, D)
    outs = [[] for _ in range(6)]
    zero_lora = jnp.zeros((LORA, D), F32)
    for l in range(depth):
        row = lambda p: p[l].reshape(1, -1)
        wsh, wrest = _cast_w_in(w_in, l)
        wd_pad = jnp.concatenate([w_decay_up[l], zero_lora], axis=0).astype(BF16)
        wa_pad = jnp.concatenate([zero_lora, w_icl_up[l]], axis=0).astype(BF16)
        pre_params = (row(mu_shift), row(w0), row(a0), row(k_k), row(k_a), wd_pad, wa_pad)
        weights = (wsh, wrest, pre_params, gn_a_gain[l], gn_a_bias[l], r_k[l], w_out[l].astype(BF16),
                   row(ln_gain), row(ln_bias))
        ada = _ada(jnp.concatenate([c_prompt, c_sample], axis=0), w_ada[l], b_ada[l])
        h_p, s1, s2, s3 = _layer(
            h_p, ada[:nb], jnp.zeros((nb, SHIFT_W), F32), None, None, 0,
            weights, nseq=nb, t=t_p, seq_major=True, depth=depth, **PROMPT_TILES)
        h_s, t1, t2, t3 = _layer(
            h_s, ada[nb:], state_shift[l], state_wkv[l], state_ret[l], PAST_LEN,
            weights, nseq=ns, t=t_s, seq_major=False, tt=ns, tc=t_s, tp=t_s * ns, depth=depth)
        for lst, val in zip(outs, (s1, s2, s3, t1, t2, t3)):
            lst.append(val)
    y_p = h_p.reshape(nb, t_p, D)
    y_s = h_s.reshape(t_s, ns, D).transpose(1, 0, 2)
    return (y_p, y_s) + tuple(jnp.stack(o) for o in outs)
```

```python
import functools
import math

import numpy as np
import jax
import jax.numpy as jnp
from jax import lax
from jax.experimental import pallas as pl
from jax.experimental.pallas import tpu as pltpu

F32 = jnp.float32
BF16 = jnp.bfloat16

D = 1024
HA = 16
KA = 64
LORA = 64
GN_EPS = 64e-5
HB = 4
DK = 128
DV = 256
RET_CHUNK = 128
ROPE_BASE = 10000.0
LN_EPS = 1e-5
SHIFT_W = 3 * D + 2 * LORA
REST_W = 6 * D
LANES = 128
SUBLANES = 8
VMEM_LIMIT = 56 * 1024 * 1024
PROMPT_TILES = dict(tt=256, tc=256, wkv_seqs=2, tp=256)


def _sigmoid(x):
    return 1.0 / (1.0 + jnp.exp(-x))


def _log_decay(w_pre):
    zw = -w_pre
    softplus = jnp.maximum(zw, 0.0) + jnp.log(1.0 + jnp.exp(-jnp.abs(zw)))
    return -jnp.exp(-softplus - 0.5)


def _bdot(a, b):
    return jnp.dot(a.astype(BF16), b.astype(BF16), preferred_element_type=F32)


def _bdot_nt(a, b):
    return lax.dot_general(a.astype(BF16), b.astype(BF16), (((1,), (1,)), ((), ())), preferred_element_type=F32)


def _bdot_tn(a, b):
    return lax.dot_general(a.astype(BF16), b.astype(BF16), (((0,), (0,)), ((), ())), preferred_element_type=F32)


def _split_bf16(x, parts):
    out = []
    for _ in range(parts):
        term = x.astype(BF16)
        out.append(term)
        x = x - term.astype(F32)
    return out


def _interleave(*gens):
    live = list(gens)
    while live:
        for gen in list(live):
            try:
                next(gen)
            except StopIteration:
                live.remove(gen)


def _const_spec(shape):
    nd = len(shape)
    return pl.BlockSpec(shape, lambda *_: (0,) * nd, pipeline_mode=pl.Buffered(1))


def _cast_w_in_kernel(w_hbm, sh_hbm, rest_hbm, in_buf, sh_buf, rest_buf, in_sem, out_sem, *, layer, rb, nbuf):
    nchunk = D // rb

    def load(i):
        return pltpu.make_async_copy(w_hbm.at[layer, pl.ds(i * rb, rb), :], in_buf.at[i % nbuf], in_sem.at[i % nbuf])

    def stores(i):
        return (pltpu.make_async_copy(sh_buf.at[i % nbuf], sh_hbm.at[pl.ds(i * rb, rb), :], out_sem.at[0, i % nbuf]),
                pltpu.make_async_copy(rest_buf.at[i % nbuf], rest_hbm.at[pl.ds(i * rb, rb), :],
                                      out_sem.at[1, i % nbuf]))

    for i in range(min(nbuf, nchunk)):
        load(i).start()
    for i in range(nchunk):
        slot = i % nbuf
        load(i).wait()
        if i >= nbuf:
            for cp in stores(i - nbuf):
                cp.wait()
        sh_buf[slot] = in_buf[slot, :, :SHIFT_W].astype(BF16)
        rest_buf[slot] = in_buf[slot, :, SHIFT_W:].astype(BF16)
        for cp in stores(i):
            cp.start()
        if i + nbuf < nchunk:
            load(i + nbuf).start()
    for i in range(max(0, nchunk - nbuf), nchunk):
        for cp in stores(i):
            cp.wait()


def _cast_w_in(w_in, layer):
    rb, nbuf = 64, 4
    return pl.pallas_call(
        functools.partial(_cast_w_in_kernel, layer=layer, rb=rb, nbuf=nbuf),
        in_specs=[pl.BlockSpec(memory_space=pl.ANY)],
        out_specs=[pl.BlockSpec(memory_space=pl.ANY), pl.BlockSpec(memory_space=pl.ANY)],
        out_shape=[jax.ShapeDtypeStruct((D, SHIFT_W), BF16), jax.ShapeDtypeStruct((D, REST_W), BF16)],
        scratch_shapes=[pltpu.VMEM((nbuf, rb, SHIFT_W + REST_W), F32), pltpu.VMEM((nbuf, rb, SHIFT_W), BF16),
                        pltpu.VMEM((nbuf, rb, REST_W), BF16), pltpu.SemaphoreType.DMA((nbuf,)),
                        pltpu.SemaphoreType.DMA((2, nbuf))],
        compiler_params=pltpu.CompilerParams(vmem_limit_bytes=VMEM_LIMIT),
        name="cast_w_in",
    )(w_in)


def _ada_kernel(c_ref, w_ref, b_ref, o_ref):
    c = c_ref[...]
    s = c * _sigmoid(c)
    o_ref[...] = jnp.dot(s, w_ref[...], preferred_element_type=F32,
                         precision=lax.Precision.HIGHEST) + b_ref[...]


def _ada(c_all, w_ada, b_ada):
    n = c_all.shape[0]
    return pl.pallas_call(
        _ada_kernel,
        grid=(3,),
        in_specs=[pl.BlockSpec((n, D), lambda j: (0, 0)),
                  pl.BlockSpec((D, D), lambda j: (0, j)),
                  pl.BlockSpec((1, D), lambda j: (0, j))],
        out_specs=pl.BlockSpec((n, D), lambda j: (0, j)),
        out_shape=jax.ShapeDtypeStruct((n, 3 * D), F32),
        compiler_params=pltpu.CompilerParams(dimension_semantics=("arbitrary",)),
        name="ada",
    )(c_all, w_ada, b_ada.reshape(1, 3 * D))


def _modulate(x, scale, shift, gs):
    if gs == 1:
        return x * (1.0 + scale) + shift
    tt = x.shape[0]
    x3 = x.reshape(tt // gs, gs, D)
    return (x3 * (1.0 + scale)[None] + shift[None]).reshape(tt, D)


def _put_head_pairs(out_ref, scr, x, y):
    tt = x.shape[0]
    low = lax.broadcasted_iota(jnp.int32, (tt, LANES), 1) < KA
    for m in range(D // LANES):
        xc = x[:, m * LANES:(m + 1) * LANES]
        yc = y[:, m * LANES:(m + 1) * LANES]
        even = jnp.where(low, xc, pltpu.roll(yc, KA, axis=1))
        odd = jnp.where(low, pltpu.roll(xc, KA, axis=1), yc)
        for r in range(tt // SUBLANES):
            rows = slice(r * SUBLANES, (r + 1) * SUBLANES)
            scr[pl.ds((r * HA + 2 * m) * SUBLANES, SUBLANES), :] = even[rows]
            scr[pl.ds((r * HA + 2 * m + 1) * SUBLANES, SUBLANES), :] = odd[rows]

    def row_block(r, carry):
        for s in range(SUBLANES):
            out_ref[r * SUBLANES + s] = scr[pl.ds(r * (HA * SUBLANES) + s, HA, stride=SUBLANES), :]
        return carry

    lax.fori_loop(0, tt // SUBLANES, row_block, 0)


def _pre_kernel(x_ref, scale_ref, shift_ref, prev_ref, wsh_ref, wrest_ref, mu_ref, w0_ref, a0_ref,
                kk_ref, ka_ref, wd_ref, wa_ref, *rest, gs, tt, pad, pair_out):
    if pair_out:
        rk_out, wk_out, va_out, ga_out, gb_out, qk_out, vb_out, ns_out, p_scr, pair_scr = rest
    else:
        r_out, k_out, v_out, wpre_out, apre_out, ga_out, gb_out, qk_out, vb_out, ns_out, p_scr = rest
    j = pl.program_id(1)
    nj = pl.num_programs(1)
    lo = pad - gs

    @pl.when(j == 0)
    def _():
        p_scr[lo:pad, :] = prev_ref[...]

    u = _modulate(x_ref[...], scale_ref[...], shift_ref[...], gs).astype(BF16)

    def mixed(c0, c1):
        p_scr[pad:pad + tt, c0:c1] = jnp.dot(u, wsh_ref[:, c0:c1], preferred_element_type=F32)
        p = p_scr[pad:pad + tt, c0:c1]
        prev = p_scr[lo:lo + tt, c0:c1]
        return p + (prev - p) * mu_ref[:, c0:c1]

    lora_in = mixed(3 * D, SHIFT_W)
    lora_w = _bdot(jnp.tanh(lora_in), wd_ref[...])
    lora_a = _bdot(lora_in, wa_ref[...])
    w_pre = w0_ref[...] + lora_w
    a_pre = a0_ref[...] + lora_a
    k = mixed(D, 2 * D)
    if pair_out:
        a = _sigmoid(a_pre)
        _put_head_pairs(rk_out, pair_scr.at[0], mixed(0, D), k * (1.0 + (a - 1.0) * ka_ref[...]))
        _put_head_pairs(wk_out, pair_scr.at[1], jnp.exp(_log_decay(w_pre)), k * kk_ref[...])
        _put_head_pairs(va_out, pair_scr.at[2], mixed(2 * D, 3 * D), a)
    else:
        wpre_out[...] = w_pre
        apre_out[...] = a_pre
        k_out[...] = k
        r_out[...] = mixed(0, D)
        v_out[...] = mixed(2 * D, 3 * D)

    @pl.when(j == nj - 1)
    def _():
        ns_out[...] = p_scr[pad + tt - gs:pad + tt, :]

    p_scr[lo:pad, :] = p_scr[pad + tt - gs:pad + tt, :]

    def proj(c0, c1):
        return jnp.dot(u, wrest_ref[:, c0:c1], preferred_element_type=F32)

    z_a = proj(0, D)
    g_a = proj(4 * D, 5 * D)
    ga_out[...] = (_sigmoid(g_a) * (z_a * _sigmoid(z_a))).astype(ga_out.dtype)
    z_b = proj(3 * D, 4 * D)
    g_b = proj(5 * D, 6 * D)
    gb_out[...] = (_sigmoid(g_b) * (z_b * _sigmoid(z_b))).astype(gb_out.dtype)
    qk_out[...] = proj(D, 2 * D).astype(qk_out.dtype)
    vb_out[...] = proj(2 * D, 3 * D).astype(vb_out.dtype)


def _pre(x2, scale, shift, first_prev, wsh, wrest, params, *, nseq_grid, tiles, tt, gs, pair_out, gate_dtype):
    rows = x2.shape[0]
    pad = max(SUBLANES, gs)
    mu, w0, a0, k_k, k_a, wd_pad, wa_pad = params
    row_spec = pl.BlockSpec((tt, D), lambda n, j: (n * tiles + j, 0))
    seq_spec = pl.BlockSpec((None, gs, D), lambda n, j: (n, 0, 0))
    scratch = [pltpu.VMEM((pad + tt, SHIFT_W), F32)]
    if pair_out:
        vec_specs = [pl.BlockSpec((tt, HA, LANES), lambda n, j: (n * tiles + j, 0, 0))] * 3
        vec_sds = [jax.ShapeDtypeStruct((rows, HA, LANES), F32)] * 3
        scratch.append(pltpu.VMEM((3, tt * HA, LANES), F32))
    else:
        vec_specs = [row_spec] * 5
        vec_sds = [jax.ShapeDtypeStruct((rows, D), F32)] * 5
    gate_sds = [jax.ShapeDtypeStruct((rows, D), gate_dtype)] * 4
    kern = functools.partial(_pre_kernel, gs=gs, tt=tt, pad=pad, pair_out=pair_out)
    return pl.pallas_call(
        kern,
        grid=(nseq_grid, tiles),
        in_specs=[row_spec, seq_spec, seq_spec,
                  pl.BlockSpec((None, gs, SHIFT_W), lambda n, j: (n, 0, 0)),
                  _const_spec((D, SHIFT_W)), _const_spec((D, REST_W)), _const_spec((1, SHIFT_W)),
                  _const_spec((1, D)), _const_spec((1, D)), _const_spec((1, D)), _const_spec((1, D)),
                  _const_spec((LANES, D)), _const_spec((LANES, D))],
        out_specs=vec_specs + [row_spec] * 4 + [pl.BlockSpec((None, gs, SHIFT_W), lambda n, j: (n, 0, 0))],
        out_shape=vec_sds + gate_sds + [jax.ShapeDtypeStruct((nseq_grid, gs, SHIFT_W), F32)],
        scratch_shapes=scratch,
        compiler_params=pltpu.CompilerParams(dimension_semantics=("arbitrary", "arbitrary"),
                                             vmem_limit_bytes=VMEM_LIMIT),
        name="pre",
    )(x2, scale, shift, first_prev, wsh, wrest, mu, w0, a0, k_k, k_a, wd_pad, wa_pad)


V_KKN, V_B, V_W, V_K, V_R, V_V = range(6)


def _wkv_kernel(rk_ref, wk_ref, va_ref, s0_ref, gain_ref, bias_ref, rkp_ref,
                o_ref, sfin_ref, s_scr, vec_scr, row_scr, y_scr, cur_scr, *, tc):
    j = pl.program_id(1)

    @pl.when(j == 0)
    def _():
        s_scr[...] = s0_ref[...]

    def load_pair(ref, t):
        xt = ref[t].reshape(LANES, LANES).T
        return xt[:KA], xt[KA:]

    def prep(t, carry):
        dec, kkraw = load_pair(wk_ref, t)
        ss = jnp.sum(kkraw * kkraw, axis=0, keepdims=True)
        kkn = kkraw * (1.0 / jnp.maximum(jnp.sqrt(ss), 1e-12))
        v, a = load_pair(va_ref, t)
        r, kmod = load_pair(rk_ref, t)
        vec_scr[t, V_KKN] = kkn
        vec_scr[t, V_B] = kkn * a
        vec_scr[t, V_W] = dec
        vec_scr[t, V_K] = kmod
        vec_scr[t, V_R] = r
        vec_scr[t, V_V] = v
        rk = jnp.sum(r * kmod * rkp_ref[...], axis=0, keepdims=True)
        row_scr[t] = jnp.broadcast_to(rk, (SUBLANES, LANES))
        return carry

    def state_pass(t, carry):
        cur_scr[...] = vec_scr[t]
        skk = jnp.zeros((KA, LANES), F32)
        for k in range(KA):
            skk = skk + s_scr[k] * cur_scr[V_KKN, k:k + 1, :]
        vv = cur_scr[V_V]
        y = jnp.zeros((KA, LANES), F32)
        for k in range(KA):
            s_new = (s_scr[k] * cur_scr[V_W, k:k + 1, :] - skk * cur_scr[V_B, k:k + 1, :]
                     + vv * cur_scr[V_K, k:k + 1, :])
            s_scr[k] = s_new
            y = y + s_new * cur_scr[V_R, k:k + 1, :]
        y_scr[t] = y
        return carry

    def out_token(t):
        y = y_scr[t]
        m = jnp.mean(y, axis=0, keepdims=True)
        yc = y - m
        var = jnp.mean(yc * yc, axis=0, keepdims=True)
        yn = yc * lax.rsqrt(var + GN_EPS) * gain_ref[...] + bias_ref[...]
        return yn + row_scr[t, 0:1, :] * vec_scr[t, V_V]

    def finish(i, carry):
        o2 = jnp.concatenate([out_token(i), out_token(i + tc // 2)], axis=0)
        o_ref[i] = o2.T.reshape(LANES // HA, HA, LANES)
        return carry

    lax.fori_loop(0, tc, prep, 0, unroll=2)
    lax.fori_loop(0, tc, state_pass, 0)
    lax.fori_loop(0, tc // 2, finish, 0, unroll=2)

    @pl.when(j == pl.num_programs(1) - 1)
    def _():
        sfin_ref[...] = s_scr[...]


def _wkv(pairs, s0, gain_l, bias_l, rk_l, *, nseq, t, tc):
    nsl = LANES // HA
    groups = nseq // nsl
    shape4, oshape4 = (t, nseq, HA, LANES), (t // 2, nseq, HA, LANES)
    vec_spec = pl.BlockSpec((tc, nsl, HA, LANES), lambda g, j: (j, g, 0, 0))
    out_spec = pl.BlockSpec((tc // 2, nsl, HA, LANES), lambda g, j: (j, g, 0, 0))
    st_spec = pl.BlockSpec((KA, KA, LANES), lambda g, j: (0, 0, g))
    par_spec = pl.BlockSpec((KA, LANES), lambda g, j: (0, g))
    o4, s_fin = pl.pallas_call(
        functools.partial(_wkv_kernel, tc=tc),
        grid=(groups, t // tc),
        in_specs=[vec_spec] * 3 + [st_spec, par_spec, par_spec, par_spec],
        out_specs=[out_spec, st_spec],
        out_shape=[jax.ShapeDtypeStruct(oshape4, F32),
                   jax.ShapeDtypeStruct((KA, KA, nseq * HA), F32)],
        scratch_shapes=[pltpu.VMEM((KA, KA, LANES), F32), pltpu.VMEM((tc, 6, KA, LANES), F32),
                        pltpu.VMEM((tc, SUBLANES, LANES), F32), pltpu.VMEM((tc, KA, LANES), F32),
                        pltpu.VMEM((6, KA, LANES), F32)],
        compiler_params=pltpu.CompilerParams(dimension_semantics=("arbitrary", "arbitrary"),
                                             vmem_limit_bytes=VMEM_LIMIT),
        name="wkv",
    )(*[z.reshape(shape4) for z in pairs], s0, gain_l, bias_l, rk_l)
    return o4.reshape(nseq * t // 2, HA, LANES), s_fin


CH = 64
NPAIR = D // LANES


def _wkv_chunk_kernel(r_ref, k_ref, v_ref, wpre_ref, apre_ref, kk_ref, ka_ref, gain_ref, bias_ref, rk_ref,
                      tri_ref, ones_ref, o_ref, tfin_ref, t_scr, *, ct):
    j = pl.program_id(1)

    @pl.when(j == 0)
    def _():
        t_scr[...] = jnp.zeros_like(t_scr)

    lane = lax.broadcasted_iota(jnp.int32, (CH, LANES), 1)
    row = lax.broadcasted_iota(jnp.int32, (CH, LANES), 0)
    low = lane < KA
    col = jnp.bitwise_and(lane, KA - 1)
    strict = row > col
    incl = row >= col
    eye = (row == col).astype(F32)
    low2 = lax.broadcasted_iota(jnp.int32, (LANES, LANES), 1) < KA
    diag_blocks = (lax.broadcasted_iota(jnp.int32, (LANES, LANES), 0) < KA) == low2
    tri = tri_ref[...]
    ones_bd = ones_ref[...]

    def bd(z):
        return jnp.concatenate([jnp.where(low, z, 0.0), jnp.where(low, 0.0, z)], axis=0)

    def abd(z):
        return jnp.concatenate([jnp.where(low, 0.0, z), jnp.where(low, z, 0.0)], axis=0)

    def seg_sum_pairs(xs):
        tot = _bdot(jnp.concatenate(xs, axis=0), ones_bd)
        return [tot[i * CH:(i + 1) * CH] for i in range(len(xs))]

    nbatch = r_ref.shape[0]
    B = range(nbatch)
    P = range(nbatch * NPAIR)
    seq = [i // NPAIR for i in P]
    pair = [i % NPAIR for i in P]
    ln = [slice(pair[i] * LANES, (pair[i] + 1) * LANES) for i in P]

    def chunk_prep(c0, ctx):
        rows = slice(c0, c0 + CH)
        lw_all = [_log_decay(wpre_ref[b, rows, :]) for b in B]
        cs_all = [sum(jnp.dot(tri, term, preferred_element_type=F32) for term in _split_bf16(lw_all[b], 2))
                  for b in B]
        yield
        lw = [lw_all[seq[m]][:, ln[m]] for m in P]
        cs = [cs_all[seq[m]][:, ln[m]] for m in P]
        k_in = [k_ref[seq[m], rows, ln[m]] for m in P]
        kkraw = [k_in[m] * kk_ref[:, ln[m]] for m in P]
        ss = seg_sum_pairs([kkraw[m] * kkraw[m] for m in P])
        yield
        a = [_sigmoid(apre_ref[seq[m], rows, ln[m]]) for m in P]
        kkn = [kkraw[m] / jnp.maximum(jnp.sqrt(ss[m]), 1e-12) for m in P]
        b = [kkn[m] * a[m] for m in P]
        r = [r_ref[seq[m], rows, ln[m]] for m in P]
        k = [k_in[m] * (1.0 + (a[m] - 1.0) * ka_ref[:, ln[m]]) for m in P]
        v = [v_ref[seq[m], rows, ln[m]] for m in P]
        cs_last = [cs[m][CH - 1:CH, :] for m in P]
        e_neg = [jnp.exp(-cs[m]) for m in P]
        a_t = [-kkn[m] * jnp.exp(cs[m] - lw[m]) for m in P]
        r_t = [r[m] * jnp.exp(cs[m]) for m in P]
        b_t = [b[m] * e_neg[m] for m in P]
        k_t = [k[m] * e_neg[m] for m in P]
        lhs = [jnp.concatenate([a_t[m], r_t[m]], axis=0) for m in P]
        out0 = [_bdot_nt(jnp.where(low2, lhs[m], 0.0), jnp.concatenate([b_t[m], k_t[m]], axis=0)) for m in P]
        yield
        out1 = [_bdot_nt(jnp.where(low2, 0.0, lhs[m]), jnp.concatenate([k_t[m], b_t[m]], axis=0)) for m in P]
        yield
        l_cat = [jnp.where(strict, jnp.where(low, out0[m][:CH], out1[m][:CH]), 0.0) for m in P]
        ak_cat = [jnp.where(strict, jnp.where(low, out1[m][:CH], out0[m][:CH]), 0.0) for m in P]
        rb_cat = [jnp.where(incl, jnp.where(low, out0[m][CH:], out1[m][CH:]), 0.0) for m in P]
        rk_cat = [jnp.where(incl, jnp.where(low, out1[m][CH:], out0[m][CH:]), 0.0) for m in P]
        rkv = seg_sum_pairs([r[m] * k[m] * rk_ref[:, ln[m]] for m in P])
        e_end = [jnp.exp(cs_last[m] - cs[m]) for m in P]
        ctx.update(
            rows=rows, l_cat=l_cat, v=v, rkv=rkv,
            g_lhs=[jnp.concatenate([a_t[m], ak_cat[m]], axis=1) for m in P],
            y_lhs=[jnp.concatenate([r_t[m], rb_cat[m], rk_cat[m]], axis=1) for m in P],
            upd_lhs=[jnp.concatenate([b[m] * e_end[m], k[m] * e_end[m]], axis=0) for m in P],
            p_end=[jnp.exp(cs_last[m]) for m in P])
        yield

    def chunk_inverse(ctx):
        l_cat = ctx["l_cat"]
        x = [eye + l_cat[m] for m in P]
        lp = [_bdot(l_cat[m], bd(l_cat[m])) for m in P]
        yield
        for _ in range(4):
            both = [_bdot(jnp.concatenate([lp[m], x[m]], axis=0), bd(lp[m])) for m in P]
            yield
            x = [x[m] + both[m][CH:] for m in P]
            lp = [both[m][:CH] for m in P]
        ctx["x"] = [x[m] + _bdot(x[m], bd(lp[m])) for m in P]
        yield

    def chunk_state(ctx):
        x, v = ctx["x"], ctx["v"]
        t0 = [t_scr[seq[m], pair[m]] for m in P]
        v_abd = [abd(v[m]) for m in P]
        g = [_bdot(ctx["g_lhs"][m], jnp.concatenate([t0[m], v_abd[m]], axis=0)) for m in P]
        yield
        u = [_bdot(x[m], bd(g[m])) for m in P]
        yield
        upd = [_bdot_tn(ctx["upd_lhs"][m], jnp.concatenate([u[m], v[m]], axis=0)) for m in P]
        yield
        for m in P:
            p_col = jnp.broadcast_to(ctx["p_end"][m], (LANES, LANES)).T
            t_scr[seq[m], pair[m]] = t0[m] * p_col + jnp.where(diag_blocks, upd[m], 0.0)
        y = [_bdot(ctx["y_lhs"][m], jnp.concatenate([t0[m], bd(u[m]), v_abd[m]], axis=0)) for m in P]
        yield
        mean = seg_sum_pairs(y)
        yield
        yc = [y[m] - mean[m] * (1.0 / KA) for m in P]
        var = seg_sum_pairs([yc[m] * yc[m] for m in P])
        yield
        for m in P:
            yn = yc[m] * lax.rsqrt(var[m] * (1.0 / KA) + GN_EPS) * gain_ref[:, ln[m]] + bias_ref[:, ln[m]]
            o_ref[seq[m], ctx["rows"], ln[m]] = yn + ctx["rkv"][m] * v[m]
        yield

    nchunk = ct // CH
    ctxs = [dict() for _ in range(nchunk)]
    for c in range(-2, nchunk):
        stages = []
        if c + 2 < nchunk:
            stages.append(chunk_prep((c + 2) * CH, ctxs[c + 2]))
        if 0 <= c + 1 < nchunk:
            stages.append(chunk_inverse(ctxs[c + 1]))
        if c >= 0:
            stages.append(chunk_state(ctxs[c]))
        _interleave(*stages)

    @pl.when(j == pl.num_programs(1) - 1)
    def _():
        tfin_ref[...] = t_scr[...]


def _wkv_chunked(vecs, k_k, k_a, gain, bias, rk, *, nseq, t, ct, nbatch):
    row_spec = pl.BlockSpec((nbatch, ct, D), lambda n, j: (n, j, 0))
    tri = jnp.asarray(np.tril(np.ones((CH, CH), np.float32)), BF16)
    head_of_lane = np.arange(LANES) // KA
    ones_bd = jnp.asarray((head_of_lane[:, None] == head_of_lane[None, :]).astype(np.float32), BF16)
    ya, t_fin = pl.pallas_call(
        functools.partial(_wkv_chunk_kernel, ct=ct),
        grid=(nseq // nbatch, t // ct),
        in_specs=[row_spec] * 5 + [_const_spec((1, D))] * 5 + [_const_spec((CH, CH)), _const_spec((LANES, LANES))],
        out_specs=[row_spec, pl.BlockSpec((nbatch, NPAIR, LANES, LANES), lambda n, j: (n, 0, 0, 0))],
        out_shape=[jax.ShapeDtypeStruct((nseq, t, D), F32),
                   jax.ShapeDtypeStruct((nseq, NPAIR, LANES, LANES), F32)],
        scratch_shapes=[pltpu.VMEM((nbatch, NPAIR, LANES, LANES), F32)],
        compiler_params=pltpu.CompilerParams(dimension_semantics=("arbitrary", "arbitrary"),
                                             vmem_limit_bytes=VMEM_LIMIT),
        name="wkv_chunk",
    )(*[z.reshape(nseq, t, D) for z in vecs], k_k, k_a, gain, bias, rk, tri, ones_bd)
    t6 = t_fin.reshape(nseq, NPAIR, 2, KA, 2, KA)
    diag = jnp.stack([t6[:, :, 0, :, 0, :], t6[:, :, 1, :, 1, :]], axis=2)
    return ya.reshape(nseq * t, D), diag.reshape(nseq, HA, KA, KA).transpose(0, 1, 3, 2)


def _log_gamma():
    return [math.log1p(-2.0 ** (-5.0 - h)) for h in range(HB)]


def _rope_tables(pos):
    half = DK // 2
    theta = 1.0 / (ROPE_BASE ** jnp.linspace(0.0, 1.0, half, dtype=F32))
    ang = pos[:, None] * theta[None, :]
    cos, sin = jnp.cos(ang), jnp.sin(ang)
    return jnp.concatenate([cos, cos], axis=1), jnp.concatenate([-sin, sin], axis=1)


def _decay_tables(chunk):
    lg = jnp.asarray(_log_gamma(), F32)
    idx = jnp.arange(chunk, dtype=F32)
    diff = idx[:, None] - idx[None, :]
    dmat = jnp.exp(jnp.maximum(diff, 0.0)[None] * lg[:, None, None]) * (diff >= 0)[None]
    cross = jnp.exp((idx[None, :] + 1.0) * lg[:, None])[:, :, None]
    kdec = jnp.exp((chunk - 1.0 - idx)[None, :] * lg[:, None])[:, :, None]
    sdec = jnp.exp(chunk * lg)
    return dmat.astype(F32), cross, kdec, sdec


def _rotary(x, cos2, sin2):
    return x * cos2 + pltpu.roll(x, DK // 2, axis=1) * sin2


def _ln_rows(y, eps):
    m = jnp.mean(y, axis=-1, keepdims=True)
    yc = y - m
    var = jnp.mean(yc * yc, axis=-1, keepdims=True)
    return yc * lax.rsqrt(var + eps)


def _ret_post_kernel(qk_ref, vb_ref, ga_ref, ya_ref, gb_ref, x_ref, gate_ref, cos_ref, sin_ref,
                     dmat_ref, cross_ref, kdec_ref, sdec_ref, wout_ref, lng_ref, lnb_ref,
                     y_ref, sfin_ref, s_scr, *, tt, alpha):
    j = pl.program_id(1)

    @pl.when(j == 0)
    def _():
        s_scr[...] = jnp.zeros_like(s_scr)

    nb = x_ref.shape[0]
    B = range(nb)
    U = [(b, h) for b in B for h in range(HB)]
    nchunk = tt // RET_CHUNK

    ctx = [dict() for _ in range(nchunk)]

    def rotate(c):
        rows = slice(c * RET_CHUNK, (c + 1) * RET_CHUNK)
        cos2, sin2 = cos_ref[rows, :], sin_ref[rows, :]
        ctx[c]["qb"] = [_rotary(qk_ref[b, rows, h * DK:(h + 1) * DK].astype(F32), cos2, sin2).astype(BF16)
                        for b, h in U]
        yield
        k = [_rotary(qk_ref[b, rows, (HB + h) * DK:(HB + h + 1) * DK].astype(F32), cos2, sin2) for b, h in U]
        ctx[c]["kb"] = [k[u].astype(BF16) for u in range(len(U))]
        yield
        ctx[c]["kd"] = [(k[u] * kdec_ref[h]).astype(BF16) for u, (b, h) in enumerate(U)]
        yield

    def retention(c):
        rows = slice(c * RET_CHUNK, (c + 1) * RET_CHUNK)
        qb, kb, kd = ctx[c]["qb"], ctx[c]["kb"], ctx[c]["kd"]
        v = [vb_ref[b, rows, h * DV:(h + 1) * DV] for b, h in U]
        dn = (((1,), (1,)), ((), ()))
        scores = [lax.dot_general(qb[u], kb[u], dn, preferred_element_type=F32) * dmat_ref[h]
                  for u, (b, h) in enumerate(U)]
        yield
        s = [s_scr[b, h] for b, h in U]
        cross = [jnp.dot(qb[u], s[u].astype(BF16), preferred_element_type=F32) * cross_ref[h]
                 for u, (b, h) in enumerate(U)]
        yield
        upd = [lax.dot_general(kd[u], v[u], (((0,), (0,)), ((), ())), preferred_element_type=F32)
               for u in range(len(U))]
        yield
        inner = [jnp.dot(scores[u].astype(BF16), v[u], preferred_element_type=F32) for u in range(len(U))]
        for u, (b, h) in enumerate(U):
            s_scr[b, h] = s[u] * sdec_ref[h] + upd[u]
        ctx[c]["yb"] = [inner[u] + cross[u] for u in range(len(U))]
        yield

    def project(c):
        rows = slice(c * RET_CHUNK, (c + 1) * RET_CHUNK)
        ob = [_ln_rows(yb, LN_EPS) for yb in ctx[c]["yb"]]
        yield
        lhs = jnp.concatenate(
            [(ga_ref[b, rows, :].astype(F32) * ya_ref[b, rows, :]
              + gb_ref[b, rows, :].astype(F32) * jnp.concatenate(ob[b * HB:(b + 1) * HB], axis=1)).astype(BF16)
             for b in B], axis=0)
        cols = []
        for n0 in range(0, D, 2 * LANES):
            cols.append(jnp.dot(lhs, wout_ref[:, n0:n0 + 2 * LANES], preferred_element_type=F32))
            yield
        sub = jnp.concatenate(cols, axis=1)
        for b in B:
            z = alpha * x_ref[b, rows, :] + gate_ref[b] * sub[b * RET_CHUNK:(b + 1) * RET_CHUNK]
            y_ref[b, rows, :] = _ln_rows(z, LN_EPS) * lng_ref[...] + lnb_ref[...]
        yield

    for c in range(-2, nchunk):
        _interleave(*([rotate(c + 2)] if c + 2 < nchunk else []),
                    *([retention(c + 1)] if 0 <= c + 1 < nchunk else []),
                    *([project(c)] if c >= 0 else []))

    @pl.when(j == pl.num_programs(1) - 1)
    def _():
        sfin_ref[...] = s_scr[...]


def _ret_post(qk, vb, ga, ya, gb, x2, gate, wout, ln_gain, ln_bias, *, nseq, t, tt, nbatch, alpha):
    cos2, sin2 = _rope_tables(jnp.arange(t, dtype=F32) + 0)
    dmat, cross, kdec, sdec = _decay_tables(RET_CHUNK)
    key_scale = DK ** -0.5
    row_spec = pl.BlockSpec((nbatch, tt, D), lambda n, j: (n, j, 0))
    tab_spec = pl.BlockSpec((tt, DK), lambda n, j: (j, 0))
    rows3 = lambda z: z.reshape(nseq, t, D)
    y, s_fin = pl.pallas_call(
        functools.partial(_ret_post_kernel, tt=tt, alpha=alpha),
        grid=(nseq // nbatch, t // tt),
        in_specs=[row_spec] * 6 + [pl.BlockSpec((nbatch, 1, D), lambda n, j: (n, 0, 0)), tab_spec, tab_spec,
                                   _const_spec((HB, RET_CHUNK, RET_CHUNK)), _const_spec((HB, RET_CHUNK, 1)),
                                   _const_spec((HB, RET_CHUNK, 1)), pl.BlockSpec(memory_space=pltpu.SMEM),
                                   _const_spec((D, D)), _const_spec((1, D)), _const_spec((1, D))],
        out_specs=[row_spec, pl.BlockSpec((nbatch, HB, DK, DV), lambda n, j: (n, 0, 0, 0))],
        out_shape=[jax.ShapeDtypeStruct((nseq, t, D), F32),
                   jax.ShapeDtypeStruct((nseq, HB, DK, DV), F32)],
        scratch_shapes=[pltpu.VMEM((nbatch, HB, DK, DV), F32)],
        compiler_params=pltpu.CompilerParams(dimension_semantics=("arbitrary", "arbitrary"),
                                             vmem_limit_bytes=VMEM_LIMIT),
        name="ret_post",
    )(rows3(qk), rows3(vb), rows3(ga), rows3(ya), rows3(gb), rows3(x2), gate, cos2, sin2,
      dmat * key_scale, cross, kdec * key_scale, sdec, wout, ln_gain, ln_bias)
    return y.reshape(nseq * t, D), s_fin


def _ret_sample_kernel(qk_ref, vb_ref, cos_ref, sin_ref, s_ref, dmat_ref, cross_ref, kdec_ref, sdec_ref,
                       o_ref, snew_ref, *, nb):
    cos2, sin2 = cos_ref[...], sin_ref[...]
    U = [(n, h) for n in range(nb) for h in range(HB)]
    idx = range(len(U))
    qb = [_rotary(qk_ref[:, n, h * DK:(h + 1) * DK], cos2, sin2).astype(BF16) for n, h in U]
    k = [_rotary(qk_ref[:, n, (HB + h) * DK:(HB + h + 1) * DK], cos2, sin2) * (DK ** -0.5) for n, h in U]
    v = [vb_ref[:, n, h * DV:(h + 1) * DV].astype(BF16) for n, h in U]
    s = [s_ref[n, h] for n, h in U]
    cross = [jnp.dot(qb[u], s[u].astype(BF16), preferred_element_type=F32) * cross_ref[U[u][1]] for u in idx]
    scores = [_bdot_nt(qb[u], k[u]) * dmat_ref[U[u][1]] for u in idx]
    upd = [_bdot_tn(k[u] * kdec_ref[U[u][1]], v[u]) for u in idx]
    inner = [jnp.dot(scores[u].astype(BF16), v[u], preferred_element_type=F32) for u in idx]
    for u, (n, h) in enumerate(U):
        snew_ref[n, h] = s[u] * sdec_ref[h] + upd[u]
        o_ref[:, n, h * DV:(h + 1) * DV] = _ln_rows(inner[u] + cross[u], LN_EPS)


def _ret_sample(qk3, vb3, s0, pos0, *, nb):
    t, nseq, _ = qk3.shape
    cos2, sin2 = _rope_tables(jnp.arange(t, dtype=F32) + pos0)
    dmat, cross, kdec, sdec = _decay_tables(t)
    row_spec = pl.BlockSpec((t, nb, D), lambda i: (0, i, 0))
    st_spec = pl.BlockSpec((nb, HB, DK, DV), lambda i: (i, 0, 0, 0))
    return pl.pallas_call(
        functools.partial(_ret_sample_kernel, nb=nb),
        grid=(nseq // nb,),
        in_specs=[row_spec, row_spec, _const_spec((t, DK)), _const_spec((t, DK)), st_spec,
                  _const_spec((HB, t, t)), _const_spec((HB, t, 1)), _const_spec((HB, t, 1)),
                  pl.BlockSpec(memory_space=pltpu.SMEM)],
        out_specs=[row_spec, st_spec],
        out_shape=[jax.ShapeDtypeStruct(qk3.shape, F32), jax.ShapeDtypeStruct(s0.shape, F32)],
        compiler_params=pltpu.CompilerParams(dimension_semantics=("arbitrary",),
                                             vmem_limit_bytes=VMEM_LIMIT),
        name="ret_sample",
    )(qk3, vb3, cos2, sin2, s0, dmat, cross, kdec, sdec)


def _unpair_tokens(ya_ref, ya_scr, pair_rows):
    half = ya_ref.shape[0]
    groups = half // pair_rows
    low = lax.broadcasted_iota(jnp.int32, (half, LANES), 1) < KA
    for m in range(D // LANES):
        ha = ya_ref[:, 2 * m, :]
        hb = ya_ref[:, 2 * m + 1, :]
        lo = jnp.where(low, ha, pltpu.roll(hb, KA, axis=1))
        hi = jnp.where(low, pltpu.roll(ha, KA, axis=1), hb)
        ya_scr[:, 0, :, m * LANES:(m + 1) * LANES] = lo.reshape(groups, pair_rows, LANES)
        ya_scr[:, 1, :, m * LANES:(m + 1) * LANES] = hi.reshape(groups, pair_rows, LANES)
    return ya_scr[...].reshape(2 * half, D)


def _post_kernel(ga_ref, ya_ref, gb_ref, ob_ref, x_ref, gate_ref, wout_ref, lng_ref, lnb_ref, y_ref,
                 ya_scr, *, gs, alpha, pair_rows):
    merged = ga_ref[...] * _unpair_tokens(ya_ref, ya_scr, pair_rows) + gb_ref[...] * ob_ref[...]
    sub = jnp.dot(merged.astype(BF16), wout_ref[...], preferred_element_type=F32)
    x = x_ref[...]
    gate = gate_ref[...]
    if gs == 1:
        z = alpha * x + gate * sub
    else:
        tt = x.shape[0]
        z = (alpha * x.reshape(tt // gs, gs, D) + gate[None] * sub.reshape(tt // gs, gs, D)).reshape(tt, D)
    y_ref[...] = _ln_rows(z, LN_EPS) * lng_ref[...] + lnb_ref[...]


def _post(ga, ya, gb, ob, x2, gate, wout, ln_gain, ln_bias, *, nseq_grid, tiles, tt, gs, alpha, pair_rows):
    row_spec = pl.BlockSpec((tt, D), lambda n, j: (n * tiles + j, 0))
    ya_spec = pl.BlockSpec((tt // 2, HA, LANES), lambda n, j: (n * tiles + j, 0, 0))
    scratch = [pltpu.VMEM((tt // 2 // pair_rows, 2, pair_rows, D), F32)]
    return pl.pallas_call(
        functools.partial(_post_kernel, gs=gs, alpha=alpha, pair_rows=pair_rows),
        grid=(nseq_grid, tiles),
        in_specs=[row_spec, ya_spec, row_spec, row_spec, row_spec,
                  pl.BlockSpec((None, gs, D), lambda n, j: (n, 0, 0)),
                  _const_spec((D, D)), _const_spec((1, D)), _const_spec((1, D))],
        out_specs=row_spec,
        out_shape=jax.ShapeDtypeStruct(x2.shape, F32),
        scratch_shapes=scratch,
        compiler_params=pltpu.CompilerParams(dimension_semantics=("arbitrary", "arbitrary"),
                                             vmem_limit_bytes=VMEM_LIMIT),
        name="post",
    )(ga, ya, gb, ob, x2, gate, wout, ln_gain, ln_bias)


def _head_param_lanes(p, nseq):
    return jnp.tile(p.reshape(HA, KA).T, (1, nseq))


def _layer(x2, ada, first_prev, s_wkv, s_ret, pos0, weights, *, nseq, t, seq_major, tt, tc, tp, depth, wkv_seqs=1):
    (wsh, wrest, pre_params, gn_gain, gn_bias, r_k, wout, ln_gain, ln_bias) = weights
    if seq_major:
        nseq_grid, gs, tiles = nseq, 1, t // tt
    else:
        nseq_grid, gs, tiles = 1, nseq, (t * nseq) // tt
    shift_c = ada[:, :D].reshape(nseq_grid, gs, D)
    scale_c = ada[:, D:2 * D].reshape(nseq_grid, gs, D)
    gate_c = ada[:, 2 * D:].reshape(nseq_grid, gs, D)
    cfg = dict(nseq_grid=nseq_grid, tiles=tiles, tt=tt, gs=gs)

    *vecs, ga, gb, qk, vb, new_shift = _pre(
        x2, scale_c, shift_c, first_prev.reshape(nseq_grid, gs, SHIFT_W), wsh, wrest, pre_params,
        pair_out=not seq_major, gate_dtype=BF16 if seq_major else F32, **cfg)

    if seq_major:
        row = lambda p: p.reshape(1, D)
        ya, new_wkv = _wkv_chunked(vecs, pre_params[3], pre_params[4], row(gn_gain), row(gn_bias), row(r_k),
                                   nseq=nseq, t=t, ct=tc, nbatch=wkv_seqs)
        pair_rows = None
    else:
        s0 = s_wkv.transpose(3, 2, 0, 1).reshape(KA, KA, nseq * HA)
        ya, s_fin = _wkv(vecs, s0, _head_param_lanes(gn_gain, nseq), _head_param_lanes(gn_bias, nseq),
                         _head_param_lanes(r_k.reshape(-1), nseq), nseq=nseq, t=t, tc=tc)
        pair_rows = (tc // 2) * gs
        new_wkv = s_fin.reshape(KA, KA, nseq, HA).transpose(2, 3, 1, 0)

    alpha = (2.0 * depth) ** 0.25
    if seq_major:
        y, new_ret = _ret_post(qk, vb, ga, ya, gb, x2, gate_c, wout, ln_gain, ln_bias,
                               nseq=nseq, t=t, tt=tp, nbatch=wkv_seqs, alpha=alpha)
    else:
        ob3, new_ret = _ret_sample(qk.reshape(t, nseq, D), vb.reshape(t, nseq, D), s_ret, pos0, nb=8)
        y = _post(ga, ya, gb, ob3.reshape(t * nseq, D), x2, gate_c, wout, ln_gain, ln_bias, alpha=alpha,
                  pair_rows=pair_rows, nseq_grid=nseq_grid, tiles=(t * nseq) // tp, tt=tp, gs=gs)
    return y, new_shift.reshape(nseq, SHIFT_W), new_wkv, new_ret


def kernel(x_prompt, x_sample, c_prompt, c_sample, state_shift, state_wkv, state_ret, w_ada, b_ada, w_in,
           mu_shift, w0, w_decay_up, a0, w_icl_up, k_k, k_a, r_k, gn_a_gain, gn_a_bias, w_out, ln_gain, ln_bias):
    depth = w_in.shape[0]
    nb, t_p, _ = x_prompt.shape
    ns, t_s, _ = x_sample.shape
    past_len = 16384
    h_p = x_prompt.reshape(nb * t_p, D)
    h_s = x_sample.transpose(1, 0, 2).reshape(t_s * ns, D)
    outs = [[] for _ in range(6)]
    zero_lora = jnp.zeros((LORA, D), F32)
    for l in range(depth):
        row = lambda p: p[l].reshape(1, -1)
        wsh, wrest = _cast_w_in(w_in, l)
        wd_pad = jnp.concatenate([w_decay_up[l], zero_lora], axis=0).astype(BF16)
        wa_pad = jnp.concatenate([zero_lora, w_icl_up[l]], axis=0).astype(BF16)
        pre_params = (row(mu_shift), row(w0), row(a0), row(k_k), row(k_a), wd_pad, wa_pad)
        weights = (wsh, wrest, pre_params, gn_a_gain[l], gn_a_bias[l], r_k[l], w_out[l].astype(BF16),
                   row(ln_gain), row(ln_bias))
        ada = _ada(jnp.concatenate([c_prompt, c_sample], axis=0), w_ada[l], b_ada[l])
        h_p, s1, s2, s3 = _layer(
            h_p, ada[:nb], jnp.zeros((nb, SHIFT_W), F32), None, None, 0,
            weights, nseq=nb, t=t_p, seq_major=True, depth=depth, **PROMPT_TILES)
        h_s, t1, t2, t3 = _layer(
            h_s, ada[nb:], state_shift[l], state_wkv[l], state_ret[l], past_len,
            weights, nseq=ns, t=t_s, seq_major=False, tt=ns, tc=t_s, tp=t_s * ns, depth=depth)
        for lst, val in zip(outs, (s1, s2, s3, t1, t2, t3)):
            lst.append(val)
    y_p = h_p.reshape(nb, t_p, D)
    y_s = h_s.reshape(t_s, ns, D).transpose(1, 0, 2)
    return (y_p, y_s) + tuple(jnp.stack(o) for o in outs)
```
